```python
import math
import jax
import jax.numpy as jnp
from jax import lax
import numpy as np

D_MODEL = 1024
BATCH = 8
SEQ = 2048
DEPTH = 2
DEC_BATCH = 32
DEC_SEQ = 1
PAST_LEN = 16384
PAGE_SIZE = 128

HEAD_DIM = D_MODEL // 16
H_SB = 4
H_MB = 4
H_ML = 4
ML_DQK = 2 * HEAD_DIM
ML_DV = 2 * HEAD_DIM
D_SB = H_SB * HEAD_DIM
D_MB = H_MB * HEAD_DIM
D_ML = H_ML * ML_DV
D_MIX = D_SB + D_MB + D_ML
D_IN_PROJ = 4 * D_SB + 4 * D_MB + 2 * H_ML * ML_DQK + 3 * D_ML + 2 * H_ML
Q_BLOCK = 128
MOBA_BLOCK = 256
MOBA_TOPK = 3
N_BUCKETS = 32
MAX_DISTANCE = 128
ML_CHUNK = 128
EPS = 1e-6

kernel_name = 'hybrid_sb_moba_mlstm_decoder_step'


def _proj_offsets():
    sizes = [D_SB] * 4 + [D_MB] * 4 + [H_ML * ML_DQK] * 2 + [D_ML] * 3 + [H_ML] * 2
    return np.cumsum(sizes)[:-1].tolist()


def rms_norm(x, w):
    xf = x.astype(jnp.float32)
    y = xf * lax.rsqrt(jnp.mean(xf * xf, axis=-1, keepdims=True) + EPS)
    return (y * w.astype(jnp.float32)).astype(x.dtype)


def t5_bucket(dist):
    max_exact = N_BUCKETS // 2
    d = jnp.maximum(dist, 0)
    df = jnp.maximum(d, 1).astype(jnp.float32)
    large = max_exact + (jnp.log(df / max_exact) / math.log(MAX_DISTANCE / max_exact)
                         * (N_BUCKETS - max_exact)).astype(jnp.int32)
    large = jnp.minimum(large, N_BUCKETS - 1)
    return jnp.where(d < max_exact, d, large)


def stick_breaking_attend(q, k, v, q_pos, k_pos):
    z = jnp.einsum('bqhd,bkhd->bhqk', q.astype(jnp.float32), k.astype(jnp.float32)) / math.sqrt(q.shape[-1])
    mask = k_pos[None, :] < q_pos[:, None]
    log_fail = jnp.where(mask, jax.nn.log_sigmoid(-z), 0.0)
    later = lax.cumsum(log_fail, axis=3, reverse=True) - log_fail
    w = jnp.where(mask, jnp.exp(jax.nn.log_sigmoid(z) + later), 0.0)
    return jnp.einsum('bhqk,bkhd->bqhd', w, v.astype(jnp.float32)).astype(v.dtype)


def sb_sweep(q, k, v, q_start):
    B, T, H, D = q.shape
    bs = math.gcd(T, Q_BLOCK)
    nqb = T // bs
    k_pos = jnp.arange(k.shape[1])
    qb = q.reshape(B, nqb, bs, H, D).swapaxes(0, 1)

    def one(args):
        qi, i = args
        return stick_breaking_attend(qi, k, v, q_start + i * bs + jnp.arange(bs), k_pos)

    out = lax.map(one, (qb, jnp.arange(nqb)))
    return out.swapaxes(0, 1).reshape(B, T, H, D)


def moba_attend_seq(q, q_pos, k, v, rel_bias):
    Tq, H, D = q.shape
    nb = k.shape[0] // MOBA_BLOCK
    kb = k.reshape(nb, MOBA_BLOCK, H, D)
    vb = v.reshape(nb, MOBA_BLOCK, H, D)
    qf = q.astype(jnp.float32)
    k_mean = jnp.mean(kb.astype(jnp.float32), axis=1)
    gate = jnp.einsum('qhd,nhd->qhn', qf, k_mean)
    own = q_pos // MOBA_BLOCK
    fully_past = jnp.arange(nb)[None, None, :] < own[:, None, None]
    gate = jnp.where(fully_past, gate, -jnp.inf)
    topk = min(MOBA_TOPK, nb)
    _, sel = lax.top_k(gate, topk)
    sel_valid = jnp.arange(topk)[None, None, :] < own[:, None, None]
    idx = jnp.concatenate([sel, jnp.broadcast_to(own[:, None, None], (Tq, H, 1))], axis=-1)
    valid = jnp.concatenate([jnp.broadcast_to(sel_valid, (Tq, H, topk)), jnp.ones((Tq, H, 1), bool)], axis=-1)
    gather = jax.vmap(lambda blocks, i: blocks[i], in_axes=(0, 1), out_axes=1)
    k_sel = gather(kb.transpose(2, 0, 1, 3), idx)
    v_sel = gather(vb.transpose(2, 0, 1, 3), idx)
    k_pos = idx[..., None] * MOBA_BLOCK + jnp.arange(MOBA_BLOCK)
    dist = q_pos[:, None, None, None] - k_pos
    bias = rel_bias.astype(jnp.float32).T[jnp.arange(H)[None, :, None, None], t5_bucket(dist)]
    logits = jnp.einsum('qhd,qhrkd->qhrk', qf, k_sel.astype(jnp.float32)) / math.sqrt(D) + bias
    logits = jnp.where(valid[..., None] & (dist >= 0), logits, -jnp.inf)
    p = jax.nn.softmax(logits.reshape(Tq, H, -1), axis=-1).reshape(logits.shape)
    return jnp.einsum('qhrk,qhrkd->qhd', p, v_sel.astype(jnp.float32)).astype(v.dtype)


def moba_sweep(q, k, v, q_start, rel_bias):
    B, T, H, D = q.shape
    L = k.shape[1]
    Lp = -(-L // MOBA_BLOCK) * MOBA_BLOCK
    pad = ((0, 0), (0, Lp - L), (0, 0), (0, 0))
    kp = jnp.pad(k, pad)
    vp = jnp.pad(v, pad)
    bs = math.gcd(T, Q_BLOCK)
    nqb = T // bs
    qb = q.reshape(B * nqb, bs, H, D)
    b_idx = jnp.repeat(jnp.arange(B), nqb)
    blk = jnp.tile(jnp.arange(nqb), B)

    def one(args):
        qi, b, i = args
        return moba_attend_seq(qi, q_start + i * bs + jnp.arange(bs), kp[b], vp[b], rel_bias)

    out = lax.map(one, (qb, b_idx, blk))
    return out.reshape(B, T, H, D)


def mlstm_chunkwise(q, k, v, i_pre, logf, C0, n0, m0):
    B, T, H, Dk = q.shape
    Dv = v.shape[-1]
    Lc = math.gcd(T, ML_CHUNK)
    nc = T // Lc
    f32 = jnp.float32

    def chunks(a):
        return a.reshape((B, nc, Lc) + a.shape[2:]).swapaxes(0, 1)

    xs = (chunks(q.astype(f32)), chunks(k.astype(f32) * (Dk ** -0.5)), chunks(v.astype(f32)),
          chunks(i_pre.astype(f32)), chunks(logf.astype(f32)))
    causal = jnp.tril(jnp.ones((Lc, Lc), bool))

    def step(carry, xc):
        C, n, m = carry
        qc, kc, vc, ic, fc = xc
        bh = jnp.cumsum(fc, axis=1).transpose(0, 2, 1)
        ih = ic.transpose(0, 2, 1)
        Dm = jnp.where(causal, bh[..., :, None] - bh[..., None, :] + ih[..., None, :], -jnp.inf)
        g = bh + m[..., None]
        m_row = jnp.maximum(g, jnp.max(Dm, axis=-1))
        w_intra = jnp.exp(Dm - m_row[..., None])
        w_inter = jnp.exp(g - m_row)
        s = jnp.einsum('bthd,bshd->bhts', qc, kc) * w_intra
        num = (jnp.einsum('bhts,bshd->bthd', s, vc)
               + w_inter.transpose(0, 2, 1)[..., None] * jnp.einsum('bhvk,bthk->bthv', C, qc))
        den = jnp.sum(s, axis=-1) + w_inter * jnp.einsum('bhk,bthk->bht', n, qc)
        h = num / jnp.maximum(jnp.abs(den), jnp.exp(-m_row)).transpose(0, 2, 1)[..., None]
        bL = bh[..., -1]
        decay_s = bL[..., None] - bh + ih
        m_new = jnp.maximum(bL + m, jnp.max(decay_s, axis=-1))
        ws = jnp.exp(decay_s - m_new[..., None])
        wc = jnp.exp(bL + m - m_new)
        C_new = wc[..., None, None] * C + jnp.einsum('bhs,bshv,bshk->bhvk', ws, vc, kc)
        n_new = wc[..., None] * n + jnp.einsum('bhs,bshk->bhk', ws, kc)
        return (C_new, n_new, m_new), h

    init = (C0.astype(f32), n0.astype(f32), m0.astype(f32))
    (C, n, m), hs = lax.scan(step, init, xs)
    h = hs.swapaxes(0, 1).reshape(B, T, H, Dv)
    return h.astype(v.dtype), C.astype(C0.dtype), n.astype(n0.dtype), m.astype(m0.dtype)


def mixer_inputs(x, c, norm_w, w_ada, b_ada, w_in, b_gates, q_norm_w, k_norm_w):
    B, T, _ = x.shape
    shift, scale, gate = jnp.split(jax.nn.silu(c) @ w_ada + b_ada, 3, axis=-1)
    h = rms_norm(x, norm_w) * (1 + scale[:, None]) + shift[:, None]
    (q_sb, k_sb, v_sb, g_sb, q_mb, k_mb, v_mb, g_mb,
     q_ml, k_ml, v_ml, o_ml, g_ml, i_ml, f_ml) = jnp.split(h @ w_in, _proj_offsets(), axis=-1)

    def hd(a, n):
        return a.reshape(B, T, n, -1)

    q_mb = rms_norm(hd(q_mb, H_MB), q_norm_w)
    k_mb = rms_norm(hd(k_mb, H_MB), k_norm_w)
    i_pre = i_ml + b_gates[:H_ML]
    logf = jax.nn.log_sigmoid((f_ml + b_gates[H_ML:]).astype(jnp.float32))
    return gate, (hd(q_sb, H_SB), hd(k_sb, H_SB), hd(v_sb, H_SB), g_sb,
                  q_mb, k_mb, hd(v_mb, H_MB), g_mb,
                  hd(q_ml, H_ML), hd(k_ml, H_ML), hd(v_ml, H_ML), o_ml, g_ml, i_pre, logf)


def layer_step(x, c, past_sb_k, past_sb_v, past_mb_k, past_mb_v, C0, n0, m0, rel_bias,
               norm_w, w_ada, b_ada, w_in, b_gates, q_norm_w, k_norm_w, ml_norm_w, w_out):
    B, T, _ = x.shape
    q_start = past_sb_k.shape[1]
    gate, (q_sb, k_sb, v_sb, g_sb, q_mb, k_mb, v_mb, g_mb,
           q_ml, k_ml, v_ml, o_ml, g_ml, i_pre, logf) = mixer_inputs(
        x, c, norm_w, w_ada, b_ada, w_in, b_gates, q_norm_w, k_norm_w)
    y_sb = sb_sweep(q_sb, jnp.concatenate([past_sb_k, k_sb], axis=1),
                    jnp.concatenate([past_sb_v, v_sb], axis=1), q_start)
    y_mb = moba_sweep(q_mb, jnp.concatenate([past_mb_k, k_mb], axis=1),
                      jnp.concatenate([past_mb_v, v_mb], axis=1), q_start, rel_bias)
    h_ml, C, n, m = mlstm_chunkwise(q_ml, k_ml, v_ml, i_pre, logf, C0, n0, m0)
    h_ml = rms_norm(h_ml, ml_norm_w).reshape(B, T, D_ML) * jax.nn.sigmoid(o_ml)
    y = jnp.concatenate([y_sb.reshape(B, T, D_SB) * jax.nn.silu(g_sb),
                         y_mb.reshape(B, T, D_MB) * jax.nn.silu(g_mb),
                         h_ml * jax.nn.silu(g_ml)], axis=-1)
    x = x + gate[:, None] * (y @ w_out)
    return x, k_sb, v_sb, k_mb, v_mb, C, n, m


def setup_inputs(seed: int = 0) -> dict:
    key = jax.random.key(seed)
    ks = jax.random.split(key, 24)
    nrm = jax.random.normal
    n_pages = PAST_LEN // PAGE_SIZE
    n_used = DEC_BATCH * n_pages
    n_phys = n_used + n_used // 4 + 1
    perm = jax.random.permutation(ks[0], n_phys).astype(jnp.int32)
    page_table = perm[:n_used].reshape(DEC_BATCH, n_pages)
    sb_shape = (DEPTH, n_phys, PAGE_SIZE, H_SB, HEAD_DIM)
    mb_shape = (DEPTH, n_phys, PAGE_SIZE, H_MB, HEAD_DIM)
    f_bias = jnp.linspace(3.0, 6.0, H_ML, dtype=jnp.float32)
    b_gates = jnp.concatenate([0.1 * nrm(ks[1], (DEPTH, H_ML)),
                               f_bias + 0.1 * nrm(ks[2], (DEPTH, H_ML))], axis=-1)
    return {
        'x_prompt': nrm(ks[3], (BATCH, SEQ, D_MODEL)),
        'x_sample': nrm(ks[4], (DEC_BATCH, DEC_SEQ, D_MODEL)),
        'cache_sb_k': nrm(ks[5], sb_shape),
        'cache_sb_v': nrm(ks[6], sb_shape),
        'cache_moba_k': nrm(ks[7], mb_shape),
        'cache_moba_v': nrm(ks[8], mb_shape),
        'state_mlstm_C': nrm(ks[9], (DEPTH, DEC_BATCH, H_ML, ML_DV, ML_DQK)) * ML_DQK ** -0.5,
        'state_mlstm_n': nrm(ks[10], (DEPTH, DEC_BATCH, H_ML, ML_DQK)) * ML_DQK ** -0.5,
        'state_mlstm_m': nrm(ks[11], (DEPTH, DEC_BATCH, H_ML)),
        'page_table': page_table,
        'c_prompt': nrm(ks[12], (BATCH, D_MODEL)),
        'c_sample': nrm(ks[13], (DEC_BATCH, D_MODEL)),
        'norm_w': 1.0 + 0.02 * nrm(ks[14], (DEPTH, D_MODEL)),
        'w_ada': nrm(ks[15], (DEPTH, D_MODEL, 3 * D_MODEL)) * D_MODEL ** -0.5,
        'b_ada': 0.02 * nrm(ks[16], (DEPTH, 3 * D_MODEL)),
        'w_in': nrm(ks[17], (DEPTH, D_MODEL, D_IN_PROJ)) * D_MODEL ** -0.5,
        'b_gates': b_gates,
        'q_norm_w': 1.0 + 0.02 * nrm(ks[18], (DEPTH, HEAD_DIM)),
        'k_norm_w': 1.0 + 0.02 * nrm(ks[19], (DEPTH, HEAD_DIM)),
        'rel_bias': 0.2 * nrm(ks[20], (N_BUCKETS, H_MB)),
        'ml_norm_w': 1.0 + 0.02 * nrm(ks[21], (DEPTH, H_ML, ML_DV)),
        'w_out': nrm(ks[22], (DEPTH, D_MIX, D_MODEL)) * D_MIX ** -0.5,
    }


def reference(x_prompt, x_sample, cache_sb_k, cache_sb_v, cache_moba_k, cache_moba_v,
              state_mlstm_C, state_mlstm_n, state_mlstm_m, page_table, c_prompt, c_sample,
              norm_w, w_ada, b_ada, w_in, b_gates, q_norm_w, k_norm_w, rel_bias, ml_norm_w, w_out):
    Bp, Tp, _ = x_prompt.shape
    Bs = x_sample.shape[0]
    n_pages = page_table.shape[1]
    page = cache_sb_k.shape[2]
    past_len = n_pages * page
    dt = x_prompt.dtype
    empty_sb = jnp.zeros((Bp, 0, H_SB, HEAD_DIM), dt)
    empty_mb = jnp.zeros((Bp, 0, H_MB, HEAD_DIM), dt)
    zC = jnp.zeros((Bp, H_ML, ML_DV, ML_DQK), dt)
    zn = jnp.zeros((Bp, H_ML, ML_DQK), dt)
    zm = jnp.zeros((Bp, H_ML), dt)
    xp, xs = x_prompt, x_sample
    p_out = [[] for _ in range(7)]
    s_out = [[] for _ in range(7)]
    for l in range(DEPTH):
        lw = (norm_w[l], w_ada[l], b_ada[l], w_in[l], b_gates[l], q_norm_w[l], k_norm_w[l],
              ml_norm_w[l], w_out[l])
        xp, k_sb, v_sb, k_mb, v_mb, C, n, m = layer_step(
            xp, c_prompt, empty_sb, empty_sb, empty_mb, empty_mb, zC, zn, zm, rel_bias, *lw)
        for lst, a in zip(p_out, (k_sb.reshape(Bp, Tp // page, page, H_SB, HEAD_DIM),
                                  v_sb.reshape(Bp, Tp // page, page, H_SB, HEAD_DIM),
                                  k_mb.reshape(Bp, Tp // page, page, H_MB, HEAD_DIM),
                                  v_mb.reshape(Bp, Tp // page, page, H_MB, HEAD_DIM), C, n, m)):
            lst.append(a)
        past_sb_k = cache_sb_k[l, page_table].reshape(Bs, past_len, H_SB, HEAD_DIM)
        past_sb_v = cache_sb_v[l, page_table].reshape(Bs, past_len, H_SB, HEAD_DIM)
        past_mb_k = cache_moba_k[l, page_table].reshape(Bs, past_len, H_MB, HEAD_DIM)
        past_mb_v = cache_moba_v[l, page_table].reshape(Bs, past_len, H_MB, HEAD_DIM)
        xs, k_sb, v_sb, k_mb, v_mb, C, n, m = layer_step(
            xs, c_sample, past_sb_k, past_sb_v, past_mb_k, past_mb_v,
            state_mlstm_C[l], state_mlstm_n[l], state_mlstm_m[l], rel_bias, *lw)
        for lst, a in zip(s_out, (k_sb, v_sb, k_mb, v_mb, C, n, m)):
            lst.append(a)
    P = [jnp.stack(a) for a in p_out]
    S = [jnp.stack(a) for a in s_out]
    return (xp, xs, P[0], P[1], P[2], P[3], P[4], P[5], P[6], S[0], S[1], S[2], S[3], S[4], S[5], S[6])
```

```python
import functools
import math

import numpy as np
import jax
import jax.numpy as jnp
from jax import lax
from jax.experimental import pallas as pl
from jax.experimental.pallas import tpu as pltpu

F32 = jnp.float32
BF16 = jnp.bfloat16
I32 = jnp.int32
HI = lax.Precision.HIGHEST

EPS = 1e-6
HEAD_DIM = 64
N_HEADS = 4
D_ATT = N_HEADS * HEAD_DIM
ML_HEADS = 4
ML_D = 128
D_ML = ML_HEADS * ML_D
Q_BLOCK = 128
KV_TILE = 256
MOBA_TOPK = 3
N_BUCKETS = 32
MAX_DISTANCE = 128
ML_CHUNK = 128
PAGE = 128
HEAD_SHIFT = 6
NEG_INF = float("-inf")

VMEM_LIMIT = 56 * 1024 * 1024

_SIZES = [D_ATT] * 8 + [D_ML] * 5 + [ML_HEADS] * 2
_OFF = np.concatenate([[0], np.cumsum(_SIZES)]).tolist()
(O_QSB, O_KSB, O_VSB, O_GSB, O_QMB, O_KMB, O_VMB, O_GMB,
 O_QML, O_KML, O_VML, O_OML, O_GML, O_IML, O_FML, O_END) = _OFF


def _params(sem, vmem=VMEM_LIMIT):
    return pltpu.CompilerParams(dimension_semantics=sem, vmem_limit_bytes=vmem)


def _dot(a, b):
    return jnp.dot(a, b, preferred_element_type=F32)


def _dot_nt(a, b, precision=None):
    return lax.dot_general(a, b, (((1,), (1,)), ((), ())), precision=precision,
                           preferred_element_type=F32)


def _softplus(z):
    return jnp.maximum(z, 0.0) + jnp.log1p(jnp.exp(-jnp.abs(z)))


def _head_of_lane(shape, axis):
    return lax.shift_right_logical(lax.broadcasted_iota(I32, shape, axis), HEAD_SHIFT)


def _stack_heads(x):
    head = _head_of_lane((1, D_ATT), 1)
    return jnp.concatenate([jnp.where(head == h, x, jnp.zeros_like(x)) for h in range(N_HEADS)], axis=0)


def _unstack_heads(acc, rows):
    head = _head_of_lane((1, D_ATT), 1)
    out = jnp.zeros((rows, D_ATT), F32)
    for h in range(N_HEADS):
        out = out + jnp.where(head == h, acc[h * rows:(h + 1) * rows], 0.0)
    return out


def _ada_kernel(c_ref, w_ref, b_ref, o_ref):
    c = c_ref[...]
    a = c * jax.nn.sigmoid(c)
    o_ref[...] = jnp.dot(a, w_ref[...], precision=HI, preferred_element_type=F32) + b_ref[...]


def ada_modulation(c_all, w_ada, b_ada):
    depth, d, d3 = w_ada.shape
    rows = c_all.shape[0]
    nt = d3 // d
    return pl.pallas_call(
        _ada_kernel,
        out_shape=jax.ShapeDtypeStruct((depth, rows, d3), F32),
        grid=(depth, nt),
        in_specs=[pl.BlockSpec((rows, d), lambda l, j: (0, 0)),
                  pl.BlockSpec((None, d, d), lambda l, j: (l, 0, j)),
                  pl.BlockSpec((None, 1, d), lambda l, j: (l, 0, j))],
        out_specs=pl.BlockSpec((None, rows, d), lambda l, j: (l, 0, j)),
        compiler_params=_params(("parallel", "parallel")),
        name="ada_modulation",
    )(c_all, w_ada, b_ada.reshape(depth, 1, d3))


def _bucket_thresholds():
    d = np.arange(0, 4 * MAX_DISTANCE, dtype=np.int32)
    max_exact = N_BUCKETS // 2
    df = np.maximum(d, 1).astype(np.float32)
    large = max_exact + (np.log(df / np.float32(max_exact)) / np.float32(math.log(MAX_DISTANCE / max_exact))
                         * np.float32(N_BUCKETS - max_exact)).astype(np.int32)
    large = np.minimum(large, N_BUCKETS - 1)
    bucket = np.where(d < max_exact, d, large)
    thr = []
    for k in range(1, N_BUCKETS):
        idx = np.nonzero(bucket >= k)[0]
        thr.append(int(idx[0]))
    assert all(np.all((bucket >= k) == (d >= t)) for k, t in zip(range(1, N_BUCKETS), thr))
    return thr


_BUCKET_THR = _bucket_thresholds()

_PATTERN_OFFSETS = (0, Q_BLOCK, KV_TILE)
P_FAR = len(_PATTERN_OFFSETS)


def _bias_from_dist(dist, rb_ref, h):
    b = jnp.full(dist.shape, rb_ref[0, h], F32)
    for k, t in zip(range(1, N_BUCKETS), _BUCKET_THR):
        b = jnp.where(dist >= t, rb_ref[k, h], b)
    return jnp.where(dist >= 0, b, NEG_INF)


def _bias_kernel(rb_ref, tile_ref, dec_ref):
    i = lax.broadcasted_iota(I32, (Q_BLOCK, KV_TILE), 0)
    j = lax.broadcasted_iota(I32, (Q_BLOCK, KV_TILE), 1)
    for p, off in enumerate(_PATTERN_OFFSETS):
        for h in range(N_HEADS):
            tile_ref[p, h * Q_BLOCK:(h + 1) * Q_BLOCK, :] = _bias_from_dist(off + i - j, rb_ref, h)
    for h in range(N_HEADS):
        tile_ref[P_FAR, h * Q_BLOCK:(h + 1) * Q_BLOCK, :] = jnp.full((Q_BLOCK, KV_TILE), rb_ref[N_BUCKETS - 1, h], F32)
    jr = lax.broadcasted_iota(I32, (8, KV_TILE), 1)
    for h in range(N_HEADS):
        dec_ref[h, 0] = jnp.full((8, KV_TILE), rb_ref[N_BUCKETS - 1, h], F32)
        dec_ref[h, 1] = _bias_from_dist(KV_TILE - jr, rb_ref, h)
        dec_ref[h, 2] = jnp.full((8, KV_TILE), rb_ref[0, h], F32)


def bias_tiles(rel_bias):
    return pl.pallas_call(
        _bias_kernel,
        out_shape=(jax.ShapeDtypeStruct((P_FAR + 1, N_HEADS * Q_BLOCK, KV_TILE), F32),
                   jax.ShapeDtypeStruct((N_HEADS, 3, 8, KV_TILE), F32)),
        in_specs=[pl.BlockSpec(memory_space=pltpu.SMEM)],
        name="bias_tiles",
    )(rel_bias)


def _inproj_kernel(x_ref, nw_ref, sc_ref, sh_ref, w_ref, wlo_ref, bg_ref, qnw_ref, knw_ref,
                   qsb_ref, ksb_ref, ksbh_ref, vsb_ref, vsbh_ref,
                   qmb_ref, qmbh_ref, kmb_ref, kmbh_ref, vmb_ref, vmbh_ref,
                   qml_ref, kml_ref, vml_ref, gact_ref, gates_ref, *maybe_kmean, paged_kv):
    def store_kv(ref, p):
        if paged_kv:
            for pg in range(p.shape[0] // PAGE):
                ref[pg] = p[pg * PAGE:(pg + 1) * PAGE].T
        else:
            ref[...] = p

    x = x_ref[...]
    ms = jnp.mean(x * x, axis=-1, keepdims=True)
    h = (x * lax.rsqrt(ms + EPS) * nw_ref[...]) * (1.0 + sc_ref[...]) + sh_ref[...]
    hb = h.astype(BF16)
    hlo = (h - hb.astype(F32)).astype(BF16)

    def proj(a, b):
        return _dot(hb, w_ref[:, a:b])

    def proj3(a, b, la, lb):
        wh = w_ref[:, a:b]
        return _dot(hb, wh) + (_dot(hlo, wh) + _dot(hb, wlo_ref[:, la:lb]))

    qsb_ref[...] = (proj(O_QSB, O_KSB) * (HEAD_DIM ** -0.5)).astype(BF16)
    p = proj(O_KSB, O_VSB)
    store_kv(ksb_ref, p)
    ksbh_ref[...] = p.astype(BF16)
    p = proj(O_VSB, O_GSB)
    store_kv(vsb_ref, p)
    vsbh_ref[...] = p.astype(BF16)
    g = proj(O_GSB, O_QMB)
    gact_ref[:, 0:D_ATT] = (g * jax.nn.sigmoid(g)).astype(BF16)

    r = lax.broadcasted_iota(I32, (D_ATT, D_ATT), 0)
    c = lax.broadcasted_iota(I32, (D_ATT, D_ATT), 1)
    same_head = lax.shift_right_logical(r, HEAD_SHIFT) == lax.shift_right_logical(c, HEAD_SHIFT)
    head_mean = jnp.where(same_head, 1.0 / HEAD_DIM, 0.0).astype(F32)

    def head_norm(p, w):
        ms = jnp.dot(p * p, head_mean, precision=HI, preferred_element_type=F32)
        return p * lax.rsqrt(ms + EPS) * w

    qn = head_norm(proj3(O_QMB, O_KMB, 0, D_ATT), qnw_ref[...])
    qmb_ref[...] = qn
    qmbh_ref[...] = (qn * (HEAD_DIM ** -0.5)).astype(BF16)
    kn = head_norm(proj3(O_KMB, O_VMB, D_ATT, 2 * D_ATT), knw_ref[...])
    store_kv(kmb_ref, kn)
    kmbh_ref[...] = kn.astype(BF16)
    if maybe_kmean:
        kmean_ref, = maybe_kmean
        rows = kn.shape[0]
        for i in range(rows // KV_TILE):
            kmean_ref[i] = jnp.mean(kn[i * KV_TILE:(i + 1) * KV_TILE], axis=0, keepdims=True)
    p = proj(O_VMB, O_GMB)
    store_kv(vmb_ref, p)
    vmbh_ref[...] = p.astype(BF16)
    g = proj(O_GMB, O_QML)
    gact_ref[:, D_ATT:2 * D_ATT] = (g * jax.nn.sigmoid(g)).astype(BF16)

    qml_ref[...] = proj(O_QML, O_KML).astype(BF16)
    kml_ref[...] = (proj(O_KML, O_VML) * (ML_D ** -0.5)).astype(BF16)
    vml_ref[...] = proj(O_VML, O_OML).astype(BF16)
    o = proj(O_OML, O_GML)
    g = proj(O_GML, O_IML)
    gact_ref[:, 2 * D_ATT:] = (jax.nn.sigmoid(o) * (g * jax.nn.sigmoid(g))).astype(BF16)
    pre = proj3(O_IML, O_END, 2 * D_ATT, 2 * D_ATT + 2 * ML_HEADS) + bg_ref[...]
    is_f = lax.broadcasted_iota(I32, pre.shape, 1) >= ML_HEADS
    logf = jnp.minimum(pre, 0.0) - jnp.log1p(jnp.exp(-jnp.abs(pre)))
    gates_ref[...] = jnp.where(is_f, logf, pre)


def input_projection(x, norm_w, scale, shift, w_bf, w_lo, b_gates, qnw, knw, *, tm, rows_per_mod, prompt):
    m, d = x.shape
    n_tiles = m // tm
    rmod = scale.shape[1]
    row = lambda i: (i, 0)
    const = lambda i: (0, 0)
    mod = lambda i: (i // rows_per_mod, 0, 0)

    def out(width, dtype):
        return jax.ShapeDtypeStruct((m, width), dtype), pl.BlockSpec((tm, width), row)

    def kv():
        if not prompt:
            return out(D_ATT, F32)
        return (jax.ShapeDtypeStruct((m // PAGE, D_ATT, PAGE), F32),
                pl.BlockSpec((tm // PAGE, D_ATT, PAGE), lambda i: (i, 0, 0)))

    outs = [out(D_ATT, BF16),
            kv(), out(D_ATT, BF16),
            kv(), out(D_ATT, BF16),
            out(D_ATT, F32), out(D_ATT, BF16),
            kv(), out(D_ATT, BF16),
            kv(), out(D_ATT, BF16),
            out(D_ML, BF16), out(D_ML, BF16), out(D_ML, BF16),
            out(2 * D_ATT + D_ML, BF16),
            out(2 * ML_HEADS, F32)]
    if prompt:
        nb = tm // KV_TILE
        outs.append((jax.ShapeDtypeStruct((n_tiles, nb, 1, D_ATT), F32),
                     pl.BlockSpec((None, nb, 1, D_ATT), lambda i: (i, 0, 0, 0))))
    return pl.pallas_call(
        functools.partial(_inproj_kernel, paged_kv=prompt),
        out_shape=tuple(o[0] for o in outs),
        grid=(n_tiles,),
        in_specs=[pl.BlockSpec((tm, d), row),
                  pl.BlockSpec((1, d), const),
                  pl.BlockSpec((None, rmod, d), mod),
                  pl.BlockSpec((None, rmod, d), mod),
                  pl.BlockSpec(w_bf.shape, const),
                  pl.BlockSpec(w_lo.shape, const),
                  pl.BlockSpec((1, 2 * ML_HEADS), const),
                  pl.BlockSpec((1, D_ATT), const),
                  pl.BlockSpec((1, D_ATT), const)],
        out_specs=tuple(o[1] for o in outs),
        compiler_params=_params(("parallel",)),
        name="input_projection",
    )(x, norm_w, scale, shift, w_bf, w_lo, b_gates, qnw, knw)


def _outproj_kernel(ysb_ref, ymb_ref, yml_ref, x_ref, gate_ref, w_ref, o_ref):
    y = (_dot(ysb_ref[...], w_ref[0:D_ATT, :]) + _dot(ymb_ref[...], w_ref[D_ATT:2 * D_ATT, :])
         + _dot(yml_ref[...], w_ref[2 * D_ATT:, :]))
    o_ref[...] = x_ref[...] + gate_ref[...] * y


def output_projection(ysb, ymb, yml, x, gate, w_bf, *, tm, rows_per_mod):
    m, d = x.shape
    rmod = gate.shape[1]
    row = lambda i: (i, 0)
    return pl.pallas_call(
        _outproj_kernel,
        out_shape=jax.ShapeDtypeStruct((m, d), F32),
        grid=(m // tm,),
        in_specs=[pl.BlockSpec((tm, D_ATT), row), pl.BlockSpec((tm, D_ATT), row), pl.BlockSpec((tm, D_ML), row),
                  pl.BlockSpec((tm, d), row),
                  pl.BlockSpec((None, rmod, d), lambda i: (i // rows_per_mod, 0, 0)),
                  pl.BlockSpec(w_bf.shape, lambda i: (0, 0))],
        out_specs=pl.BlockSpec((tm, d), row),
        compiler_params=_params(("parallel",)),
        name="output_projection",
    )(ysb, ymb, yml, x, gate, w_bf)


def _suffix_matrix(n):
    s = lax.broadcasted_iota(I32, (n, n), 0)
    j = lax.broadcasted_iota(I32, (n, n), 1)
    return jnp.where(s > j, 1.0, 0.0).astype(BF16)


def _split_dot(x, m):
    hi = x.astype(BF16)
    lo = (x - hi.astype(F32)).astype(BF16)
    return _dot(hi, m) + _dot(lo, m)


def _sb_tile(q4, kt, vt, suffix, carry, mask, keys_on_lanes=False):
    z = _dot(q4, kt) if keys_on_lanes else _dot_nt(q4, kt)
    sp = _softplus(z)
    lf = -sp if mask is None else jnp.where(mask, -sp, 0.0)
    logw = (z - sp) + _split_dot(lf, suffix) + carry
    w = jnp.exp(logw)
    if mask is not None:
        w = jnp.where(mask, w, 0.0)
    wb = w.astype(BF16)
    pv = _dot_nt(wb, vt) if keys_on_lanes else _dot(wb, vt)
    return pv, carry + jnp.sum(lf, axis=-1, keepdims=True)


def _sb_prompt_kernel(q_ref, k_ref, v_ref, g_ref, o_ref, acc_ref):
    i = pl.program_id(1)
    rows = N_HEADS * Q_BLOCK
    q4 = _stack_heads(q_ref[...])
    suffix = _suffix_matrix(KV_TILE)
    last = (i * Q_BLOCK) // KV_TILE
    start = pl.multiple_of(last * KV_TILE, KV_TILE)
    qpos = i * Q_BLOCK + (lax.broadcasted_iota(I32, (rows, KV_TILE), 0) & (Q_BLOCK - 1))
    kpos = start + lax.broadcasted_iota(I32, (rows, KV_TILE), 1)
    pv, carry = _sb_tile(q4, k_ref[pl.ds(start, KV_TILE), :], v_ref[pl.ds(start, KV_TILE), :], suffix,
                         jnp.zeros((rows, 1), F32), kpos < qpos)
    acc_ref[...] = pv

    def body(s, carry):
        st = pl.multiple_of((last - 1 - s) * KV_TILE, KV_TILE)
        pv, carry = _sb_tile(q4, k_ref[pl.ds(st, KV_TILE), :], v_ref[pl.ds(st, KV_TILE), :], suffix, carry, None)
        acc_ref[...] += pv
        return carry

    lax.fori_loop(0, last, body, carry)
    o_ref[...] = (_unstack_heads(acc_ref[...], Q_BLOCK) * g_ref[...].astype(F32)).astype(BF16)


def sb_prompt(q, k, v, gact):
    b, t, _ = q.shape
    blk = pl.BlockSpec((None, Q_BLOCK, D_ATT), lambda bi, i: (bi, i, 0))
    full = pl.BlockSpec((None, t, D_ATT), lambda bi, i: (bi, 0, 0))
    return pl.pallas_call(
        _sb_prompt_kernel,
        out_shape=jax.ShapeDtypeStruct((b, t, D_ATT), BF16),
        grid=(b, t // Q_BLOCK),
        in_specs=[blk, full, full, blk],
        out_specs=blk,
        scratch_shapes=[pltpu.VMEM((N_HEADS * Q_BLOCK, D_ATT), F32)],
        compiler_params=_params(("parallel", "arbitrary")),
        name="sb_prompt",
    )(q, k, v, gact)


def _moba_select(gate, valid, idx):
    n = gate.shape[1]
    gate = jnp.where(valid, gate, NEG_INF)
    rank = jnp.zeros(gate.shape, I32)
    for m in range(n):
        gm = gate[:, m:m + 1]
        beats = (gm > gate) | ((gm == gate) & (idx > m))
        rank = rank + beats.astype(I32)
    return valid & (rank < MOBA_TOPK)


def _moba_prompt_kernel(qf_ref, qh_ref, k_ref, v_ref, km_ref, bias_ref, g_ref, o_ref, acc_ref, m_ref, l_ref, sel_ref):
    i = pl.program_id(1)
    nb = km_ref.shape[0]
    own = (i * Q_BLOCK) // KV_TILE
    odd = (i * Q_BLOCK) % KV_TILE != 0
    q4 = _stack_heads(qh_ref[...])
    gate = _dot_nt(_stack_heads(qf_ref[...]), km_ref[...], precision=HI)
    idx = lax.broadcasted_iota(I32, (1, nb), 1)
    sel = _moba_select(gate, idx < own, idx)
    sel_ref[...] = jnp.where(sel, 0.0, NEG_INF)

    start = pl.multiple_of(own * KV_TILE, KV_TILE)
    s = _dot_nt(q4, k_ref[pl.ds(start, KV_TILE), :]) + bias_ref[jnp.where(odd, 1, 0)]
    m = jnp.max(s, axis=-1, keepdims=True)
    p = jnp.exp(s - m)
    m_ref[...] = m
    l_ref[...] = jnp.sum(p, axis=-1, keepdims=True)
    acc_ref[...] = _dot(p.astype(BF16), v_ref[pl.ds(start, KV_TILE), :])

    for n in range(nb - 1):
        @pl.when(n < own)
        def _():
            pat = jnp.where((n == own - 1) & jnp.logical_not(odd), 2, P_FAR)
            s = (_dot_nt(q4, k_ref[n * KV_TILE:(n + 1) * KV_TILE, :]) + bias_ref[pat]) + sel_ref[:, n:n + 1]
            m_old = m_ref[...]
            m_new = jnp.maximum(m_old, jnp.max(s, axis=-1, keepdims=True))
            alpha = jnp.exp(m_old - m_new)
            p = jnp.exp(s - m_new)
            m_ref[...] = m_new
            l_ref[...] = alpha * l_ref[...] + jnp.sum(p, axis=-1, keepdims=True)
            acc_ref[...] = alpha * acc_ref[...] + _dot(p.astype(BF16), v_ref[n * KV_TILE:(n + 1) * KV_TILE, :])

    y = _unstack_heads(acc_ref[...] / l_ref[...], Q_BLOCK)
    o_ref[...] = (y * g_ref[...].astype(F32)).astype(BF16)


def moba_prompt(qf, qh, k, v, kmean, bias, gact):
    b, t, _ = qh.shape
    nb = t // KV_TILE
    rows = N_HEADS * Q_BLOCK
    blk = pl.BlockSpec((None, Q_BLOCK, D_ATT), lambda bi, i: (bi, i, 0))
    full = pl.BlockSpec((None, t, D_ATT), lambda bi, i: (bi, 0, 0))
    return pl.pallas_call(
        _moba_prompt_kernel,
        out_shape=jax.ShapeDtypeStruct((b, t, D_ATT), BF16),
        grid=(b, t // Q_BLOCK),
        in_specs=[blk, blk, full, full,
                  pl.BlockSpec((None, nb, D_ATT), lambda bi, i: (bi, 0, 0)),
                  pl.BlockSpec(bias.shape, lambda bi, i: (0, 0, 0)),
                  pl.BlockSpec((None, Q_BLOCK, D_ATT), lambda bi, i: (bi, i, 1))],
        out_specs=blk,
        scratch_shapes=[pltpu.VMEM((rows, D_ATT), F32), pltpu.VMEM((rows, 1), F32), pltpu.VMEM((rows, 1), F32),
                        pltpu.VMEM((rows, nb), F32)],
        compiler_params=_params(("parallel", "arbitrary")),
        name="moba_prompt",
    )(qf, qh, k, v, kmean, bias, gact)


def _gated_head_norm(h, w, g):
    hn = h * lax.rsqrt(jnp.mean(h * h, axis=-1, keepdims=True) + EPS) * w
    return (hn * g.astype(F32)).astype(BF16)


def _mlstm_prompt_kernel(q_ref, k_ref, v_ref, gc_ref, gr_ref, g_ref, nw_ref, h_ref, c_ref, n_ref, m_ref,
                         st_ref, ms_ref):
    ci = pl.program_id(1)
    L = ML_CHUNK

    @pl.when(ci == 0)
    def _():
        st_ref[...] = jnp.zeros_like(st_ref)
        ms_ref[...] = jnp.zeros_like(ms_ref)

    t = lax.broadcasted_iota(I32, (L, L), 0)
    s = lax.broadcasted_iota(I32, (L, L), 1)
    causal = s <= t
    lower = jnp.where(causal, 1.0, 0.0).astype(F32)
    upper = jnp.where(t <= s, 1.0, 0.0).astype(F32)
    gc = gc_ref[...]
    gr = gr_ref[...]
    bh_c = jnp.dot(lower, gc, precision=HI, preferred_element_type=F32)
    bh_r = jnp.dot(gr, upper, precision=HI, preferred_element_type=F32)
    ones_col = jnp.where(lax.broadcasted_iota(I32, (L, ML_D), 1) == 0, 1.0, 0.0).astype(BF16)

    for h in range(ML_HEADS):
        sl = slice(h * ML_D, (h + 1) * ML_D)
        q, k, v = q_ref[:, sl], k_ref[:, sl], v_ref[:, sl]
        vext = jnp.concatenate([v, ones_col], axis=1)
        m_prev = ms_ref[h][0:1, 0:1]
        b_c = bh_c[:, ML_HEADS + h:ML_HEADS + h + 1]
        a_c = gc[:, h:h + 1] - b_c
        a_r = gr[h:h + 1, :] - bh_r[ML_HEADS + h:ML_HEADS + h + 1, :]
        dm = jnp.where(causal, b_c + a_r, NEG_INF)
        g = b_c + m_prev
        m_row = jnp.maximum(g, jnp.max(dm, axis=-1, keepdims=True))
        w_intra = jnp.exp(dm - m_row)
        w_inter = jnp.exp(g - m_row)
        sc = _dot_nt(q, k) * w_intra
        st = st_ref[h]
        numden = _dot(sc.astype(BF16), vext) + w_inter * _dot(q, st.astype(BF16))
        den = numden[:, ML_D:ML_D + 1]
        hout = numden[:, :ML_D] / jnp.maximum(jnp.abs(den), jnp.exp(-m_row))
        h_ref[:, sl] = _gated_head_norm(hout, nw_ref[:, sl], g_ref[:, sl])

        b_last = b_c[L - 1:L, :]
        m_new = jnp.maximum(b_last + m_prev, jnp.max(b_last + a_r, axis=-1, keepdims=True))
        ws = jnp.exp(b_last + a_c - m_new)
        wc = jnp.exp(b_last + m_prev - m_new)
        kt = k.astype(F32).T.astype(BF16)
        st_new = wc * st + _dot(kt, (ws * vext.astype(F32)).astype(BF16))
        st_ref[h] = st_new
        ms_ref[h] = jnp.broadcast_to(m_new, ms_ref.shape[1:])

    @pl.when(ci == pl.num_programs(1) - 1)
    def _():
        for h in range(ML_HEADS):
            stt = st_ref[h].T
            c_ref[h] = stt[:ML_D]
            n_ref[h:h + 1, :] = stt[ML_D:ML_D + 1]
            m_ref[:, h:h + 1] = ms_ref[h][0:1, 0:1]


def mlstm_prompt(q, k, v, gates, gates_t, gact, norm_w):
    b, t, _ = q.shape
    blk = pl.BlockSpec((None, ML_CHUNK, D_ML), lambda bi, c: (bi, c, 0))
    return pl.pallas_call(
        _mlstm_prompt_kernel,
        out_shape=(jax.ShapeDtypeStruct((b, t, D_ML), BF16),
                   jax.ShapeDtypeStruct((b, ML_HEADS, ML_D, ML_D), F32),
                   jax.ShapeDtypeStruct((b, ML_HEADS, ML_D), F32),
                   jax.ShapeDtypeStruct((b, 1, ML_HEADS), F32)),
        grid=(b, t // ML_CHUNK),
        in_specs=[blk, blk, blk,
                  pl.BlockSpec((None, ML_CHUNK, 2 * ML_HEADS), lambda bi, c: (bi, c, 0)),
                  pl.BlockSpec((None, 2 * ML_HEADS, ML_CHUNK), lambda bi, c: (bi, 0, c)),
                  pl.BlockSpec((None, ML_CHUNK, D_ML), lambda bi, c: (bi, c, 1)),
                  pl.BlockSpec((1, D_ML), lambda bi, c: (0, 0))],
        out_specs=(blk,
                   pl.BlockSpec((None, ML_HEADS, ML_D, ML_D), lambda bi, c: (bi, 0, 0, 0)),
                   pl.BlockSpec((None, ML_HEADS, ML_D), lambda bi, c: (bi, 0, 0)),
                   pl.BlockSpec((None, 1, ML_HEADS), lambda bi, c: (bi, 0, 0))),
        scratch_shapes=[pltpu.VMEM((ML_HEADS, ML_D, 2 * ML_D), F32), pltpu.VMEM((ML_HEADS, 8, 128), F32)],
        compiler_params=_params(("parallel", "arbitrary")),
        name="mlstm_prompt",
    )(q, k, v, gates, gates_t, gact, norm_w)


def _page_rows(ref):
    return ref[...].reshape(D_ATT, PAGE)


def _sb_decode_kernel(pt_ref, q_ref, g_ref, *rest, pages):
    k_refs, v_refs = rest[:pages], rest[pages:2 * pages]
    o_ref, acc_ref, carry_ref = rest[2 * pages:]
    j = pl.program_id(1)

    @pl.when(j == 0)
    def _():
        acc_ref[...] = jnp.zeros_like(acc_ref)
        carry_ref[...] = jnp.zeros_like(carry_ref)

    rowi = lax.broadcasted_iota(I32, (8, D_ATT), 0)
    own_head = _head_of_lane((8, D_ATT), 1) == rowi
    qm = jnp.where(own_head, jnp.broadcast_to(q_ref[...].astype(F32), (8, D_ATT)), 0.0).astype(BF16)
    suffix = _suffix_matrix(KV_TILE)
    acc = acc_ref[...]
    carry = carry_ref[...]
    for r in range(0, pages, 2):
        kt = jnp.concatenate([_page_rows(k_refs[r + 1]), _page_rows(k_refs[r])], axis=1).astype(BF16)
        vt = jnp.concatenate([_page_rows(v_refs[r + 1]), _page_rows(v_refs[r])], axis=1).astype(BF16)
        pv, carry = _sb_tile(qm, kt, vt, suffix, carry, None, keys_on_lanes=True)
        acc = acc + pv
    acc_ref[...] = acc
    carry_ref[...] = carry

    @pl.when(j == pl.num_programs(1) - 1)
    def _():
        y = jnp.sum(jnp.where(own_head, acc, 0.0), axis=0, keepdims=True)
        o_ref[...] = (y * g_ref[...].astype(F32)).astype(BF16)


def _cache_page_spec(index_map):
    return pl.BlockSpec((None, None, N_HEADS, HEAD_DIM, PAGE), index_map)


def sb_decode(page_table, q, gact, cache_k, cache_v, layer, *, pages):
    b, n_pages = page_table.shape
    steps = n_pages // pages

    def page_spec(r):
        return _cache_page_spec(lambda bi, j, pt: (layer, pt[bi, n_pages - 1 - (j * pages + r)], 0, 0, 0))

    row = pl.BlockSpec((None, 1, D_ATT), lambda bi, j, pt: (bi, 0, 0))
    grid_spec = pltpu.PrefetchScalarGridSpec(
        num_scalar_prefetch=1,
        grid=(b, steps),
        in_specs=[row, row] + [page_spec(r) for r in range(pages)] * 2,
        out_specs=row,
        scratch_shapes=[pltpu.VMEM((8, D_ATT), F32), pltpu.VMEM((8, 1), F32)],
    )
    return pl.pallas_call(
        functools.partial(_sb_decode_kernel, pages=pages),
        out_shape=jax.ShapeDtypeStruct((b, 1, D_ATT), BF16),
        grid_spec=grid_spec,
        compiler_params=_params(("parallel", "arbitrary")),
        name="sb_decode",
    )(page_table, q, gact, *([cache_k] * pages), *([cache_v] * pages))


def _moba_gate_kernel(pt_ref, q_ref, *rest, pages):
    k_refs = rest[:pages]
    idx_ref, gs_ref = rest[pages:]
    j = pl.program_id(1)
    per_step = pages // 2
    qc = q_ref[...]
    for r in range(per_step):
        prod = (_page_rows(k_refs[2 * r]) + _page_rows(k_refs[2 * r + 1])) * qc
        for h in range(N_HEADS):
            gs_ref[h, pl.ds(j * per_step + r, 1), :] = jnp.sum(prod[h * HEAD_DIM:(h + 1) * HEAD_DIM], axis=0, keepdims=True)

    @pl.when(j == pl.num_programs(1) - 1)
    def _():
        nb = gs_ref.shape[1]
        lane = lax.broadcasted_iota(I32, (nb, 128), 1)
        gate = jnp.zeros((nb, 128), F32)
        for h in range(N_HEADS):
            gate = gate + jnp.where(lane == h, jnp.sum(gs_ref[h], axis=-1, keepdims=True) * (1.0 / KV_TILE), 0.0)
        blk = lax.broadcasted_iota(I32, (nb, 128), 0)
        rank = jnp.zeros((nb, 128), I32)
        for m in range(nb):
            gm = gate[m:m + 1, :]
            rank = rank + ((gm > gate) | ((gm == gate) & (blk > m))).astype(I32)
        blk_f = blk.astype(F32)
        rows = [jnp.sum(jnp.where(rank == t, blk_f, 0.0), axis=0, keepdims=True) for t in range(MOBA_TOPK)]
        idx_ref[...] = jnp.concatenate(rows + [jnp.zeros((8 - MOBA_TOPK, 128), F32)], axis=0).astype(I32)


def moba_decode_select(page_table, q_col, cache_k, layer, *, pages):
    b, n_pages = page_table.shape
    steps = n_pages // pages
    nb = n_pages * PAGE // KV_TILE

    def page_spec(r):
        return _cache_page_spec(lambda bi, j, pt: (layer, pt[bi, j * pages + r], 0, 0, 0))

    grid_spec = pltpu.PrefetchScalarGridSpec(
        num_scalar_prefetch=1,
        grid=(b, steps),
        in_specs=[pl.BlockSpec((None, D_ATT, 1), lambda bi, j, pt: (bi, 0, 0))] + [page_spec(r) for r in range(pages)],
        out_specs=pl.BlockSpec((None, 8, 128), lambda bi, j, pt: (bi, 0, 0)),
        scratch_shapes=[pltpu.VMEM((N_HEADS, nb, PAGE), F32)],
    )
    return pl.pallas_call(
        functools.partial(_moba_gate_kernel, pages=pages),
        out_shape=jax.ShapeDtypeStruct((b, 8, 128), I32),
        grid_spec=grid_spec,
        compiler_params=_params(("parallel", "arbitrary")),
        name="moba_decode_select",
    )(page_table, q_col, *([cache_k] * pages))


def _moba_decode_kernel(pt_ref, sel_ref, q_ref, kn_ref, vn_ref, bias_ref, g_ref, *rest, last_block):
    n_sel = 2 * MOBA_TOPK
    k_refs, v_refs = rest[:n_sel], rest[n_sel:2 * n_sel]
    o_ref, = rest[2 * n_sel:]
    bi, h = pl.program_id(0), pl.program_id(1)
    qf = jnp.broadcast_to(q_ref[...].astype(F32), (8, HEAD_DIM))
    q = qf.astype(BF16)
    s_self = jnp.sum(qf * kn_ref[...].astype(BF16).astype(F32), axis=-1, keepdims=True) + bias_ref[2][:, 0:1]
    scores = []
    for t in range(MOBA_TOPK):
        kt = jnp.concatenate([k_refs[2 * t][...], k_refs[2 * t + 1][...]], axis=1).astype(BF16)
        near = sel_ref[bi, t * N_HEADS + h] == last_block
        scores.append(_dot(q, kt) + bias_ref[jnp.where(near, 1, 0)])
    m = s_self
    for s in scores:
        m = jnp.maximum(m, jnp.max(s, axis=-1, keepdims=True))
    p_self = jnp.exp(s_self - m)
    l = p_self
    acc = p_self * vn_ref[...].astype(BF16).astype(F32)
    for t, s in enumerate(scores):
        p = jnp.exp(s - m)
        l = l + jnp.sum(p, axis=-1, keepdims=True)
        vt = jnp.concatenate([v_refs[2 * t][...], v_refs[2 * t + 1][...]], axis=1).astype(BF16)
        acc = acc + _dot_nt(p.astype(BF16), vt)
    o_ref[...] = ((acc / l)[0:1] * g_ref[...].astype(F32)).astype(BF16)


def moba_decode(page_table, sel, qh, k_new, v_new, dec_bias, gact_mb, cache_k, cache_v, layer):
    b, n_pages = page_table.shape
    last_block = n_pages * PAGE // KV_TILE - 1

    def page_spec(t, half):
        return pl.BlockSpec((None, None, None, HEAD_DIM, PAGE),
                            lambda bi, h, pt, sl: (layer, pt[bi, 2 * sl[bi, t * N_HEADS + h] + half], h, 0, 0))

    vec = pl.BlockSpec((None, None, 1, HEAD_DIM), lambda bi, h, pt, sl: (bi, h, 0, 0))
    pages = [page_spec(t, half) for t in range(MOBA_TOPK) for half in range(2)]
    grid_spec = pltpu.PrefetchScalarGridSpec(
        num_scalar_prefetch=2,
        grid=(b, N_HEADS),
        in_specs=[vec, vec, vec,
                  pl.BlockSpec((None, 3, 8, KV_TILE), lambda bi, h, pt, sl: (h, 0, 0, 0)),
                  vec] + pages + pages,
        out_specs=vec,
    )
    return pl.pallas_call(
        functools.partial(_moba_decode_kernel, last_block=last_block),
        out_shape=jax.ShapeDtypeStruct((b, N_HEADS, 1, HEAD_DIM), BF16),
        grid_spec=grid_spec,
        compiler_params=_params(("parallel", "parallel")),
        name="moba_decode",
    )(page_table, sel, qh, k_new, v_new, dec_bias, gact_mb, *([cache_k] * len(pages)), *([cache_v] * len(pages)))


def _mlstm_decode_kernel(q_ref, k_ref, v_ref, gt_ref, g_ref, nw_ref, c_ref, n_ref, m_ref,
                         h_ref, co_ref, no_ref, mo_ref):
    r = lax.broadcasted_iota(I32, (ML_D, ML_D), 0)
    c = lax.broadcasted_iota(I32, (ML_D, ML_D), 1)
    eye = r == c
    for h in range(ML_HEADS):
        sl = slice(h * ML_D, (h + 1) * ML_D)
        q = q_ref[:, sl].astype(F32)
        k = k_ref[:, sl].astype(F32)
        v = v_ref[:, sl].astype(F32)
        i_pre = gt_ref[:, h:h + 1]
        logf = gt_ref[:, ML_HEADS + h:ML_HEADS + h + 1]
        m_prev = m_ref[:, h:h + 1]
        cm = c_ref[h]
        nv = n_ref[h:h + 1, :]
        g = logf + m_prev
        m_new = jnp.maximum(g, i_pre)
        w_in = jnp.exp(i_pre - m_new)
        w_st = jnp.exp(g - m_new)
        qk = jnp.sum(q * k, axis=-1, keepdims=True)
        cq = _dot_nt(jnp.broadcast_to(q, (8, ML_D)), cm, precision=HI)[0:1]
        nq = jnp.sum(nv * q, axis=-1, keepdims=True)
        num = (qk * w_in) * v + w_st * cq
        den = qk * w_in + w_st * nq
        hout = num / jnp.maximum(jnp.abs(den), jnp.exp(-m_new))
        h_ref[:, sl] = _gated_head_norm(hout, nw_ref[:, sl], g_ref[:, sl])
        v_diag = jnp.where(eye, jnp.broadcast_to(v, (ML_D, ML_D)), 0.0)
        outer = jnp.dot(v_diag, jnp.broadcast_to(k, (ML_D, ML_D)), precision=HI, preferred_element_type=F32)
        co_ref[h] = w_st * cm + w_in * outer
        no_ref[h:h + 1, :] = w_st * nv + w_in * k
        mo_ref[:, h:h + 1] = m_new


def mlstm_decode(q, k, v, gates, gact, norm_w, c0, n0, m0):
    b = q.shape[0]
    vec = pl.BlockSpec((None, 1, D_ML), lambda bi: (bi, 0, 0))
    cs = pl.BlockSpec((None, ML_HEADS, ML_D, ML_D), lambda bi: (bi, 0, 0, 0))
    ns = pl.BlockSpec((None, ML_HEADS, ML_D), lambda bi: (bi, 0, 0))
    msp = pl.BlockSpec((None, 1, ML_HEADS), lambda bi: (bi, 0, 0))
    return pl.pallas_call(
        _mlstm_decode_kernel,
        out_shape=(jax.ShapeDtypeStruct((b, 1, D_ML), BF16),
                   jax.ShapeDtypeStruct(c0.shape, F32),
                   jax.ShapeDtypeStruct(n0.shape, F32),
                   jax.ShapeDtypeStruct(m0.shape, F32)),
        grid=(b,),
        in_specs=[vec, vec, vec,
                  pl.BlockSpec((None, 1, 2 * ML_HEADS), lambda bi: (bi, 0, 0)),
                  pl.BlockSpec((None, 1, D_ML), lambda bi: (bi, 0, 1)),
                  pl.BlockSpec((1, D_ML), lambda bi: (0, 0)),
                  cs, ns, msp],
        out_specs=(vec, cs, ns, msp),
        compiler_params=_params(("parallel",)),
        name="mlstm_decode",
    )(q, k, v, gates, gact, norm_w, c0, n0, m0)


SB_DECODE_PAGES = 8
MOBA_SELECT_PAGES = 16
PROMPT_TM = 256


def _layer_weights(w_in_l, w_out_l):
    w_bf = w_in_l.astype(BF16)
    lo_cols = jnp.concatenate([w_in_l[:, O_QMB:O_VMB], w_in_l[:, O_IML:O_END]], axis=1)
    hi_cols = jnp.concatenate([w_bf[:, O_QMB:O_VMB], w_bf[:, O_IML:O_END]], axis=1)
    w_lo = (lo_cols - hi_cols.astype(F32)).astype(BF16)
    return w_bf, w_lo, w_out_l.astype(BF16)


def kernel(x_prompt, x_sample, cache_sb_k, cache_sb_v, cache_moba_k, cache_moba_v, state_mlstm_C, state_mlstm_n,
           state_mlstm_m, page_table, c_prompt, c_sample, norm_w, w_ada, b_ada, w_in, b_gates, q_norm_w, k_norm_w,
           rel_bias, ml_norm_w, w_out):
    bp, tp, d = x_prompt.shape
    bs = x_sample.shape[0]
    depth = w_in.shape[0]
    mp = bp * tp

    mod = ada_modulation(jnp.concatenate([c_prompt, c_sample], axis=0), w_ada, b_ada)
    bias_p, bias_d = bias_tiles(rel_bias)
    lanes_last = lambda a: jnp.transpose(a, (0, 1, 3, 4, 2))
    sbk, sbv, mbk, mbv = (lanes_last(a) for a in (cache_sb_k, cache_sb_v, cache_moba_k, cache_moba_v))

    xp = x_prompt.reshape(mp, d)
    xs = x_sample.reshape(bs, d)
    p_out = [[] for _ in range(7)]
    s_out = [[] for _ in range(7)]
    for l in range(depth):
        w_bf, w_lo, wo_bf = _layer_weights(w_in[l], w_out[l])
        nw = norm_w[l].reshape(1, d)
        bg = b_gates[l].reshape(1, 2 * ML_HEADS)
        qnw = jnp.tile(q_norm_w[l], N_HEADS).reshape(1, D_ATT)
        knw = jnp.tile(k_norm_w[l], N_HEADS).reshape(1, D_ATT)
        mlw = ml_norm_w[l].reshape(1, D_ML)
        shift, scale, gate = jnp.split(mod[l], 3, axis=-1)

        pm = lambda a: a[:bp].reshape(bp, 1, d)
        (qsb, ksb, ksbh, vsb, vsbh, qmb, qmbh, kmb, kmbh, vmb, vmbh, qml, kml, vml, gact, gates, kmean) = \
            input_projection(xp, nw, pm(scale), pm(shift), w_bf, w_lo, bg, qnw, knw,
                             tm=PROMPT_TM, rows_per_mod=tp // PROMPT_TM, prompt=True)
        seq = lambda a: a.reshape(bp, tp, a.shape[-1])
        gact3 = seq(gact)
        ysb = sb_prompt(seq(qsb), seq(ksbh), seq(vsbh), gact3)
        ymb = moba_prompt(seq(qmb), seq(qmbh), seq(kmbh), seq(vmbh), kmean.reshape(bp, tp // KV_TILE, D_ATT),
                          bias_p, gact3)
        gates3 = seq(gates)
        yml, c_p, n_p, m_p = mlstm_prompt(seq(qml), seq(kml), seq(vml), gates3, gates3.swapaxes(1, 2), gact3, mlw)
        xp = output_projection(ysb.reshape(mp, D_ATT), ymb.reshape(mp, D_ATT), yml.reshape(mp, D_ML), xp, pm(gate),
                               wo_bf, tm=2 * PROMPT_TM, rows_per_mod=tp // (2 * PROMPT_TM))
        paged = lambda a: jnp.transpose(a.reshape(bp, tp // PAGE, N_HEADS, HEAD_DIM, PAGE), (0, 1, 4, 2, 3))
        for lst, a in zip(p_out, (paged(ksb), paged(vsb), paged(kmb), paged(vmb), c_p, n_p, m_p.reshape(bp, ML_HEADS))):
            lst.append(a)

        sm = lambda a: a[bp:].reshape(1, bs, d)
        (qsb, ksb, _, vsb, _, qmb, qmbh, kmb, _, vmb, _, qml, kml, vml, gact, gates) = \
            input_projection(xs, nw, sm(scale), sm(shift), w_bf, w_lo, bg, qnw, knw,
                             tm=bs, rows_per_mod=1, prompt=False)
        tok = lambda a: a.reshape(bs, 1, a.shape[-1])
        heads = lambda a: a.reshape(bs, N_HEADS, 1, HEAD_DIM)
        gact3 = tok(gact)
        ysb = sb_decode(page_table, tok(qsb), gact3[:, :, :D_ATT], sbk, sbv, l, pages=SB_DECODE_PAGES)
        sel = moba_decode_select(page_table, qmb.reshape(bs, D_ATT, 1), mbk, l, pages=MOBA_SELECT_PAGES)
        sel = sel[:, :MOBA_TOPK, :N_HEADS].reshape(bs, MOBA_TOPK * N_HEADS)
        ymb = moba_decode(page_table, sel, heads(qmbh), heads(kmb), heads(vmb), bias_d,
                          heads(gact[:, D_ATT:2 * D_ATT]), mbk, mbv, l)
        yml, c_s, n_s, m_s = mlstm_decode(tok(qml), tok(kml), tok(vml), tok(gates), gact3, mlw,
                                          state_mlstm_C[l], state_mlstm_n[l], state_mlstm_m[l].reshape(bs, 1, ML_HEADS))
        xs = output_projection(ysb.reshape(bs, D_ATT), ymb.reshape(bs, D_ATT), yml.reshape(bs, D_ML), xs, sm(gate),
                               wo_bf, tm=bs, rows_per_mod=1)
        new = lambda a: a.reshape(bs, 1, N_HEADS, HEAD_DIM)
        for lst, a in zip(s_out, (new(ksb), new(vsb), new(kmb), new(vmb), c_s, n_s, m_s.reshape(bs, ML_HEADS))):
            lst.append(a)

    P = [jnp.stack(a) for a in p_out]
    S = [jnp.stack(a) for a in s_out]
    return (xp.reshape(bp, tp, d), xs.reshape(bs, 1, d),
            P[0], P[1], P[2], P[3], P[4], P[5], P[6], S[0], S[1], S[2], S[3], S[4], S[5], S[6])
```

```python
import functools
import math

import numpy as np
import jax
import jax.numpy as jnp
from jax import lax
from jax.experimental import pallas as pl
from jax.experimental.pallas import tpu as pltpu

F32 = jnp.float32
BF16 = jnp.bfloat16
I32 = jnp.int32
HI = lax.Precision.HIGHEST

EPS = 1e-6
HEAD_DIM = 64
N_HEADS = 4
D_ATT = N_HEADS * HEAD_DIM
ML_HEADS = 4
ML_D = 128
D_ML = ML_HEADS * ML_D
Q_BLOCK = 128
KV_TILE = 256
MOBA_TOPK = 3
N_BUCKETS = 32
MAX_DISTANCE = 128
ML_CHUNK = 128
PAGE = 128
HEAD_SHIFT = 6
NEG_INF = float("-inf")

VMEM_LIMIT = 56 * 1024 * 1024

_SIZES = [D_ATT] * 8 + [D_ML] * 5 + [ML_HEADS] * 2
_OFF = np.concatenate([[0], np.cumsum(_SIZES)]).tolist()
(O_QSB, O_KSB, O_VSB, O_GSB, O_QMB, O_KMB, O_VMB, O_GMB,
 O_QML, O_KML, O_VML, O_OML, O_GML, O_IML, O_FML, O_END) = _OFF


def _params(sem, vmem=VMEM_LIMIT):
    return pltpu.CompilerParams(dimension_semantics=sem, vmem_limit_bytes=vmem)


def _dot(a, b):
    return jnp.dot(a, b, preferred_element_type=F32)


def _dot_nt(a, b, precision=None):
    return lax.dot_general(a, b, (((1,), (1,)), ((), ())), precision=precision,
                           preferred_element_type=F32)


def _softplus(z):
    return jnp.maximum(z, 0.0) + jnp.log(1.0 + jnp.exp(-jnp.abs(z)))


def _head_of_lane(shape, axis):
    return lax.shift_right_logical(lax.broadcasted_iota(I32, shape, axis), HEAD_SHIFT)


def _ada_kernel(c_ref, w_ref, b_ref, o_ref):
    c = c_ref[...]
    a = c * jax.nn.sigmoid(c)
    o_ref[...] = jnp.dot(a, w_ref[...], precision=HI, preferred_element_type=F32) + b_ref[...]


def ada_modulation(c_all, w_ada, b_ada):
    depth, d, d3 = w_ada.shape
    rows = c_all.shape[0]
    nt = d3 // d
    return pl.pallas_call(
        _ada_kernel,
        out_shape=jax.ShapeDtypeStruct((depth, rows, d3), F32),
        grid=(depth, nt),
        in_specs=[pl.BlockSpec((rows, d), lambda l, j: (0, 0)),
                  pl.BlockSpec((None, d, d), lambda l, j: (l, 0, j)),
                  pl.BlockSpec((None, 1, d), lambda l, j: (l, 0, j))],
        out_specs=pl.BlockSpec((None, rows, d), lambda l, j: (l, 0, j)),
        compiler_params=_params(("parallel", "parallel")),
        name="ada_modulation",
    )(c_all, w_ada, b_ada.reshape(depth, 1, d3))


def _bucket_thresholds():
    d = np.arange(0, 4 * MAX_DISTANCE, dtype=np.int32)
    max_exact = N_BUCKETS // 2
    df = np.maximum(d, 1).astype(np.float32)
    large = max_exact + (np.log(df / np.float32(max_exact)) / np.float32(math.log(MAX_DISTANCE / max_exact))
                         * np.float32(N_BUCKETS - max_exact)).astype(np.int32)
    large = np.minimum(large, N_BUCKETS - 1)
    bucket = np.where(d < max_exact, d, large)
    thr = []
    for k in range(1, N_BUCKETS):
        idx = np.nonzero(bucket >= k)[0]
        thr.append(int(idx[0]))
    assert all(np.all((bucket >= k) == (d >= t)) for k, t in zip(range(1, N_BUCKETS), thr))
    return thr


_BUCKET_THR = _bucket_thresholds()

_PATTERN_OFFSETS = (0, Q_BLOCK, KV_TILE)
P_FAR = len(_PATTERN_OFFSETS)


def _bias_from_dist(dist, rb_ref, h):
    b = jnp.full(dist.shape, rb_ref[0, h], F32)
    for k, t in zip(range(1, N_BUCKETS), _BUCKET_THR):
        b = jnp.where(dist >= t, rb_ref[k, h], b)
    return jnp.where(dist >= 0, b, NEG_INF)


def _bias_kernel(rb_ref, tile_ref, dec_ref):
    j = lax.broadcasted_iota(I32, (KV_TILE, Q_BLOCK), 0)
    i = lax.broadcasted_iota(I32, (KV_TILE, Q_BLOCK), 1)
    for p, off in enumerate(_PATTERN_OFFSETS):
        for h in range(N_HEADS):
            tile_ref[p, :, h * Q_BLOCK:(h + 1) * Q_BLOCK] = _bias_from_dist(off + i - j, rb_ref, h)
    for h in range(N_HEADS):
        tile_ref[P_FAR, :, h * Q_BLOCK:(h + 1) * Q_BLOCK] = jnp.full((KV_TILE, Q_BLOCK), rb_ref[N_BUCKETS - 1, h], F32)
    jr = lax.broadcasted_iota(I32, (8, KV_TILE), 1)
    for h in range(N_HEADS):
        dec_ref[h, 0] = jnp.full((8, KV_TILE), rb_ref[N_BUCKETS - 1, h], F32)
        dec_ref[h, 1] = _bias_from_dist(KV_TILE - jr, rb_ref, h)
        dec_ref[h, 2] = jnp.full((8, KV_TILE), rb_ref[0, h], F32)


def bias_tiles(rel_bias):
    return pl.pallas_call(
        _bias_kernel,
        out_shape=(jax.ShapeDtypeStruct((P_FAR + 1, KV_TILE, N_HEADS * Q_BLOCK), F32),
                   jax.ShapeDtypeStruct((N_HEADS, 3, 8, KV_TILE), F32)),
        in_specs=[pl.BlockSpec(memory_space=pltpu.SMEM)],
        name="bias_tiles",
    )(rel_bias)


def _inproj_kernel(x_ref, nw_ref, sc_ref, sh_ref, w_ref, wlo_ref, bg_ref, qnw_ref, knw_ref, *outs, prompt):
    outs = list(outs)

    def paged(p):
        return [p[pg * PAGE:(pg + 1) * PAGE].T for pg in range(p.shape[0] // PAGE)]

    def emit(p, *, rows_bf16=False, rows_f32=False, pages_f32=False, pages_bf16=False):
        pages = paged(p) if (pages_f32 or pages_bf16) else None
        for want, dtype, is_pages in ((pages_f32, F32, True), (rows_f32, F32, False),
                                      (pages_bf16, BF16, True), (rows_bf16, BF16, False)):
            if not want:
                continue
            ref = outs.pop(0)
            if is_pages:
                for pg, t in enumerate(pages):
                    ref[pg] = t.astype(dtype)
            else:
                ref[...] = p.astype(dtype)

    x = x_ref[...]
    ms = jnp.mean(x * x, axis=-1, keepdims=True)
    h = (x * lax.rsqrt(ms + EPS) * nw_ref[...]) * (1.0 + sc_ref[...]) + sh_ref[...]
    hb = h.astype(BF16)
    hlo = (h - hb.astype(F32)).astype(BF16)

    def proj(a, b):
        return _dot(hb, w_ref[:, a:b])

    def proj3(a, b, la, lb):
        wh = w_ref[:, a:b]
        return _dot(hb, wh) + (_dot(hlo, wh) + _dot(hb, wlo_ref[:, la:lb]))

    def silu(g):
        return g * jax.nn.sigmoid(g)

    r = lax.broadcasted_iota(I32, (D_ATT, D_ATT), 0)
    c = lax.broadcasted_iota(I32, (D_ATT, D_ATT), 1)
    same_head = lax.shift_right_logical(r, HEAD_SHIFT) == lax.shift_right_logical(c, HEAD_SHIFT)
    head_mean = jnp.where(same_head, 1.0 / HEAD_DIM, 0.0).astype(BF16)

    def head_norm(p, w):
        sq = p * p
        hi = sq.astype(BF16)
        lo = (sq - hi.astype(F32)).astype(BF16)
        ms = _dot(hi, head_mean) + _dot(lo, head_mean)
        return p * lax.rsqrt(ms + EPS) * w

    score_scale = HEAD_DIM ** -0.5
    emit(proj(O_QSB, O_KSB) * score_scale, pages_bf16=prompt, rows_bf16=not prompt)
    emit(proj(O_KSB, O_VSB), pages_f32=prompt, rows_bf16=prompt, rows_f32=not prompt)
    emit(proj(O_VSB, O_GSB), pages_f32=prompt, pages_bf16=prompt, rows_f32=not prompt)
    g_sb = silu(proj(O_GSB, O_QMB))
    qn = head_norm(proj3(O_QMB, O_KMB, 0, D_ATT), qnw_ref[...])
    if prompt:
        qt = paged(qn)
        ref_f, ref_h = outs.pop(0), outs.pop(0)
        for pg, t in enumerate(qt):
            ref_f[pg] = t
            ref_h[pg] = (t * score_scale).astype(BF16)
    else:
        emit(qn, rows_f32=True)
        emit(qn * score_scale, rows_bf16=True)
    kn = head_norm(proj3(O_KMB, O_VMB, D_ATT, 2 * D_ATT), knw_ref[...])
    emit(kn, pages_f32=prompt, rows_bf16=prompt, rows_f32=not prompt)
    emit(proj(O_VMB, O_GMB), pages_f32=prompt, pages_bf16=prompt, rows_f32=not prompt)
    g_mb = silu(proj(O_GMB, O_QML))
    emit(proj(O_QML, O_KML), rows_bf16=True)
    emit(proj(O_KML, O_VML) * (ML_D ** -0.5), rows_bf16=True)
    emit(proj(O_VML, O_OML), rows_bf16=True)
    o = proj(O_OML, O_GML)
    g_ml = jax.nn.sigmoid(o) * silu(proj(O_GML, O_IML))
    gact_ref = outs.pop(0)
    gact_ref[:, 0:D_ATT] = g_sb.astype(BF16)
    gact_ref[:, D_ATT:2 * D_ATT] = g_mb.astype(BF16)
    gact_ref[:, 2 * D_ATT:] = g_ml.astype(BF16)
    pre = proj(O_IML, O_END) + bg_ref[...]
    is_f = lax.broadcasted_iota(I32, pre.shape, 1) >= ML_HEADS
    logf = jnp.minimum(pre, 0.0) - jnp.log1p(jnp.exp(-jnp.abs(pre)))
    outs.pop(0)[...] = jnp.where(is_f, logf, pre)
    if prompt:
        kmean_ref = outs.pop(0)
        for i in range(kn.shape[0] // KV_TILE):
            kmean_ref[i] = jnp.mean(kn[i * KV_TILE:(i + 1) * KV_TILE], axis=0, keepdims=True)
    assert not outs


def input_projection(x, norm_w, scale, shift, w_bf, w_lo, b_gates, qnw, knw, *, tm, rows_per_mod, prompt):
    m, d = x.shape
    n_tiles = m // tm
    rmod = scale.shape[1]
    row = lambda i: (i, 0)
    const = lambda i: (0, 0)
    mod = lambda i: (i // rows_per_mod, 0, 0)

    def rows(width, dtype):
        return jax.ShapeDtypeStruct((m, width), dtype), pl.BlockSpec((tm, width), row)

    def pages(dtype):
        return (jax.ShapeDtypeStruct((m // PAGE, D_ATT, PAGE), dtype),
                pl.BlockSpec((tm // PAGE, D_ATT, PAGE), lambda i: (i, 0, 0)))

    if prompt:
        nb = tm // KV_TILE
        outs = [pages(BF16), pages(F32), rows(D_ATT, BF16), pages(F32), pages(BF16),
                pages(F32), pages(BF16), pages(F32), rows(D_ATT, BF16), pages(F32), pages(BF16)]
        tail = [(jax.ShapeDtypeStruct((n_tiles, nb, 1, D_ATT), F32),
                 pl.BlockSpec((None, nb, 1, D_ATT), lambda i: (i, 0, 0, 0)))]
    else:
        outs = [rows(D_ATT, BF16), rows(D_ATT, F32), rows(D_ATT, F32),
                rows(D_ATT, F32), rows(D_ATT, BF16), rows(D_ATT, F32), rows(D_ATT, F32)]
        tail = []
    outs += [rows(D_ML, BF16), rows(D_ML, BF16), rows(D_ML, BF16),
             rows(2 * D_ATT + D_ML, BF16), rows(2 * ML_HEADS, F32)] + tail
    return pl.pallas_call(
        functools.partial(_inproj_kernel, prompt=prompt),
        out_shape=tuple(o[0] for o in outs),
        grid=(n_tiles,),
        in_specs=[pl.BlockSpec((tm, d), row),
                  pl.BlockSpec((1, d), const),
                  pl.BlockSpec((None, rmod, d), mod),
                  pl.BlockSpec((None, rmod, d), mod),
                  pl.BlockSpec(w_bf.shape, const),
                  pl.BlockSpec(w_lo.shape, const),
                  pl.BlockSpec((1, 2 * ML_HEADS), const),
                  pl.BlockSpec((1, D_ATT), const),
                  pl.BlockSpec((1, D_ATT), const)],
        out_specs=tuple(o[1] for o in outs),
        compiler_params=_params(("parallel",)),
        name="input_projection",
    )(x, norm_w, scale, shift, w_bf, w_lo, b_gates, qnw, knw)


def _outproj_kernel(ysb_ref, ymb_ref, yml_ref, x_ref, gate_ref, w_ref, o_ref):
    y = (_dot(ysb_ref[...], w_ref[0:D_ATT, :]) + _dot(ymb_ref[...], w_ref[D_ATT:2 * D_ATT, :])
         + _dot(yml_ref[...], w_ref[2 * D_ATT:, :]))
    o_ref[...] = x_ref[...] + gate_ref[...] * y


def output_projection(ysb, ymb, yml, x, gate, w_bf, *, tm, rows_per_mod):
    m, d = x.shape
    rmod = gate.shape[1]
    row = lambda i: (i, 0)
    return pl.pallas_call(
        _outproj_kernel,
        out_shape=jax.ShapeDtypeStruct((m, d), F32),
        grid=(m // tm,),
        in_specs=[pl.BlockSpec((tm, D_ATT), row), pl.BlockSpec((tm, D_ATT), row), pl.BlockSpec((tm, D_ML), row),
                  pl.BlockSpec((tm, d), row),
                  pl.BlockSpec((None, rmod, d), lambda i: (i // rows_per_mod, 0, 0)),
                  pl.BlockSpec(w_bf.shape, lambda i: (0, 0))],
        out_specs=pl.BlockSpec((tm, d), row),
        compiler_params=_params(("parallel",)),
        name="output_projection",
    )(ysb, ymb, yml, x, gate, w_bf)


def _suffix_matrix(n, transposed=False):
    r = lax.broadcasted_iota(I32, (n, n), 0)
    c = lax.broadcasted_iota(I32, (n, n), 1)
    return jnp.where((c > r) if transposed else (r > c), 1.0, 0.0).astype(BF16)


def _heads_on_lanes(qt):
    head = _head_of_lane((D_ATT, 1), 0)
    return jnp.concatenate([jnp.where(head == h, qt, jnp.zeros_like(qt)) for h in range(N_HEADS)], axis=1)


def _heads_from_lanes(acc_t, q):
    head = _head_of_lane((D_ATT, 1), 0)
    out = jnp.zeros((D_ATT, q), F32)
    for h in range(N_HEADS):
        out = out + jnp.where(head == h, acc_t[:, h * q:(h + 1) * q], 0.0)
    return out.T


def _kv_tile(k_ref, vt_ref, n):
    start = pl.multiple_of(n * KV_TILE, KV_TILE)
    vt = jnp.concatenate([vt_ref[2 * n], vt_ref[2 * n + 1]], axis=1)
    return k_ref[pl.ds(start, KV_TILE), :], vt


def _sb_tile_t(q4t, k, vt, upper, carry, mask):
    z = _dot(k, q4t)
    sp = _softplus(z)
    if mask is not None:
        sp_sum = jnp.where(mask, sp, 0.0)
    else:
        sp_sum = sp
    later = _dot(upper, sp_sum.astype(BF16))
    w = jnp.exp(((z - sp) - later) - carry)
    if mask is not None:
        w = jnp.where(mask, w, 0.0)
    return _dot(vt, w.astype(BF16)), carry + jnp.sum(sp_sum, axis=0, keepdims=True)


def _sb_prompt_kernel(qt_ref, k_ref, vt_ref, g_ref, o_ref, acc_ref, carry_ref):
    i = pl.program_id(1)
    cols = N_HEADS * Q_BLOCK
    q4t = _heads_on_lanes(qt_ref[...])
    upper = _suffix_matrix(KV_TILE, transposed=True)
    last = (i * Q_BLOCK) // KV_TILE
    kpos = last * KV_TILE + lax.broadcasted_iota(I32, (KV_TILE, cols), 0)
    qpos = i * Q_BLOCK + (lax.broadcasted_iota(I32, (KV_TILE, cols), 1) & (Q_BLOCK - 1))
    k, vt = _kv_tile(k_ref, vt_ref, last)
    pv, carry = _sb_tile_t(q4t, k, vt, upper, jnp.zeros((1, cols), F32), kpos < qpos)
    acc_ref[...] = pv
    carry_ref[...] = carry

    @pl.when(last % 2 == 1)
    def _():
        k, vt = _kv_tile(k_ref, vt_ref, last - 1)
        pv, carry = _sb_tile_t(q4t, k, vt, upper, carry_ref[...], None)
        acc_ref[...] += pv
        carry_ref[...] = carry

    first = last - 1 - last % 2

    def body(s, carry):
        ka, vta = _kv_tile(k_ref, vt_ref, first - 2 * s)
        kb, vtb = _kv_tile(k_ref, vt_ref, first - 2 * s - 1)
        pva, carry = _sb_tile_t(q4t, ka, vta, upper, carry, None)
        pvb, carry = _sb_tile_t(q4t, kb, vtb, upper, carry, None)
        acc_ref[...] += pva + pvb
        return carry

    lax.fori_loop(0, last // 2, body, carry_ref[...])
    o_ref[...] = (_heads_from_lanes(acc_ref[...], Q_BLOCK) * g_ref[...].astype(F32)).astype(BF16)


def sb_prompt(qt, k, vt, gact):
    b, t, _ = k.shape
    n_pages = t // PAGE
    blk = pl.BlockSpec((None, Q_BLOCK, D_ATT), lambda bi, i: (bi, i, 0))
    return pl.pallas_call(
        _sb_prompt_kernel,
        out_shape=jax.ShapeDtypeStruct((b, t, D_ATT), BF16),
        grid=(b, t // Q_BLOCK),
        in_specs=[pl.BlockSpec((None, None, D_ATT, PAGE), lambda bi, i: (bi, i, 0, 0)),
                  pl.BlockSpec((None, t, D_ATT), lambda bi, i: (bi, 0, 0)),
                  pl.BlockSpec((None, n_pages, D_ATT, PAGE), lambda bi, i: (bi, 0, 0, 0)),
                  blk],
        out_specs=blk,
        scratch_shapes=[pltpu.VMEM((D_ATT, N_HEADS * Q_BLOCK), F32), pltpu.VMEM((1, N_HEADS * Q_BLOCK), F32)],
        compiler_params=_params(("parallel", "arbitrary")),
        name="sb_prompt",
    )(qt, k, vt, gact)


def _moba_select_t(gate, valid, idx):
    n = gate.shape[0]
    gate = jnp.where(valid, gate, NEG_INF)
    rank = jnp.zeros(gate.shape, I32)
    for m in range(n):
        gm = gate[m:m + 1, :]
        beats = (gm > gate) | ((gm == gate) & (idx > m))
        rank = rank + beats.astype(I32)
    return valid & (rank < MOBA_TOPK)


def _moba_prompt_kernel(qft_ref, qht_ref, k_ref, vt_ref, km_ref, bias_ref, g_ref, o_ref, acc_ref, m_ref, l_ref, sel_ref):
    i = pl.program_id(1)
    nb = km_ref.shape[0]
    own = (i * Q_BLOCK) // KV_TILE
    odd = (i * Q_BLOCK) % KV_TILE != 0
    q4t = _heads_on_lanes(qht_ref[...])
    gate = jnp.dot(km_ref[...], _heads_on_lanes(qft_ref[...]), precision=HI, preferred_element_type=F32)
    idx = lax.broadcasted_iota(I32, (nb, 1), 0)
    sel = _moba_select_t(gate, idx < own, idx)
    sel_ref[...] = jnp.where(sel, 0.0, NEG_INF)

    def attend(blocks, first=False):
        tiles = [_kv_tile(k_ref, vt_ref, n) for n, _ in blocks]
        scores = []
        for (k, _), (_, terms) in zip(tiles, blocks):
            s = _dot(k, q4t)
            for t in terms:
                s = s + t
            scores.append(s)
        m_new = functools.reduce(jnp.maximum, [jnp.max(s, axis=0, keepdims=True) for s in scores])
        if not first:
            m_old = m_ref[...]
            m_new = jnp.maximum(m_old, m_new)
            alpha = jnp.exp(m_old - m_new)
        ps = [jnp.exp(s - m_new) for s in scores]
        l_new = functools.reduce(jnp.add, [jnp.sum(p, axis=0, keepdims=True) for p in ps])
        pv = functools.reduce(jnp.add, [_dot(vt, p.astype(BF16)) for (_, vt), p in zip(tiles, ps)])
        m_ref[...] = m_new
        l_ref[...] = l_new if first else alpha * l_ref[...] + l_new
        acc_ref[...] = pv if first else alpha * acc_ref[...] + pv

    attend([(own, [bias_ref[jnp.where(odd, 1, 0)]])], first=True)
    near_pat = jnp.where(odd, P_FAR, 2)
    far_row = bias_ref[P_FAR, 0:1, :]
    for n in range(0, nb - 2, 2):
        @pl.when(n + 1 < own)
        def _():
            attend([(n, [far_row + sel_ref[n:n + 1, :]]),
                    (n + 1, [bias_ref[jnp.where(n + 1 == own - 1, near_pat, P_FAR)], sel_ref[n + 1:n + 2, :]])])

    @pl.when(own % 2 == 1)
    def _():
        attend([(own - 1, [bias_ref[near_pat], sel_ref[pl.ds(own - 1, 1), :]])])

    y = _heads_from_lanes(acc_ref[...] / l_ref[...], Q_BLOCK)
    o_ref[...] = (y * g_ref[...].astype(F32)).astype(BF16)


def moba_prompt(qft, qht, k, vt, kmean, bias, gact):
    b, t, _ = k.shape
    nb = t // KV_TILE
    cols = N_HEADS * Q_BLOCK
    blk = pl.BlockSpec((None, Q_BLOCK, D_ATT), lambda bi, i: (bi, i, 0))
    qpage = pl.BlockSpec((None, None, D_ATT, PAGE), lambda bi, i: (bi, i, 0, 0))
    return pl.pallas_call(
        _moba_prompt_kernel,
        out_shape=jax.ShapeDtypeStruct((b, t, D_ATT), BF16),
        grid=(b, t // Q_BLOCK),
        in_specs=[qpage, qpage,
                  pl.BlockSpec((None, t, D_ATT), lambda bi, i: (bi, 0, 0)),
                  pl.BlockSpec((None, t // PAGE, D_ATT, PAGE), lambda bi, i: (bi, 0, 0, 0)),
                  pl.BlockSpec((None, nb, D_ATT), lambda bi, i: (bi, 0, 0)),
                  pl.BlockSpec(bias.shape, lambda bi, i: (0, 0, 0)),
                  pl.BlockSpec((None, Q_BLOCK, D_ATT), lambda bi, i: (bi, i, 1))],
        out_specs=blk,
        scratch_shapes=[pltpu.VMEM((D_ATT, cols), F32), pltpu.VMEM((1, cols), F32), pltpu.VMEM((1, cols), F32),
                        pltpu.VMEM((nb, cols), F32)],
        compiler_params=_params(("parallel", "arbitrary")),
        name="moba_prompt",
    )(qft, qht, k, vt, kmean, bias, gact)


def _gated_head_norm(h, w, g):
    hn = h * lax.rsqrt(jnp.mean(h * h, axis=-1, keepdims=True) + EPS) * w
    return (hn * g.astype(F32)).astype(BF16)


def _mlstm_prompt_kernel(q_ref, k_ref, v_ref, gc_ref, gr_ref, g_ref, nw_ref, h_ref, c_ref, n_ref, m_ref,
                         st_ref, ms_ref):
    ci = pl.program_id(1)
    L = ML_CHUNK

    @pl.when(ci == 0)
    def _():
        st_ref[...] = jnp.zeros_like(st_ref)
        ms_ref[...] = jnp.zeros_like(ms_ref)

    t = lax.broadcasted_iota(I32, (L, L), 0)
    s = lax.broadcasted_iota(I32, (L, L), 1)
    causal = s <= t
    lower = jnp.where(causal, 1.0, 0.0).astype(F32)
    upper = jnp.where(t <= s, 1.0, 0.0).astype(F32)
    gc = gc_ref[...]
    gr = gr_ref[...]
    bh_c = jnp.dot(lower, gc, precision=HI, preferred_element_type=F32)
    bh_r = jnp.dot(gr, upper, precision=HI, preferred_element_type=F32)
    ones_col = jnp.where(lax.broadcasted_iota(I32, (L, ML_D), 1) == 0, 1.0, 0.0).astype(BF16)

    for h in range(ML_HEADS):
        sl = slice(h * ML_D, (h + 1) * ML_D)
        q, k, v = q_ref[:, sl], k_ref[:, sl], v_ref[:, sl]
        vext = jnp.concatenate([v, ones_col], axis=1)
        m_prev = ms_ref[h][0:1, 0:1]
        b_c = bh_c[:, ML_HEADS + h:ML_HEADS + h + 1]
        a_c = gc[:, h:h + 1] - b_c
        a_r = gr[h:h + 1, :] - bh_r[ML_HEADS + h:ML_HEADS + h + 1, :]
        dm = jnp.where(causal, b_c + a_r, NEG_INF)
        g = b_c + m_prev
        m_row = jnp.maximum(g, jnp.max(dm, axis=-1, keepdims=True))
        w_intra = jnp.exp(dm - m_row)
        w_inter = jnp.exp(g - m_row)
        sc = _dot_nt(q, k) * w_intra
        st = st_ref[h]
        numden = _dot(sc.astype(BF16), vext) + w_inter * _dot(q, st.astype(BF16))
        den = numden[:, ML_D:ML_D + 1]
        hout = numden[:, :ML_D] / jnp.maximum(jnp.abs(den), jnp.exp(-m_row))
        h_ref[:, sl] = _gated_head_norm(hout, nw_ref[:, sl], g_ref[:, sl])

        b_last = b_c[L - 1:L, :]
        m_new = jnp.maximum(b_last + m_prev, jnp.max(b_last + a_r, axis=-1, keepdims=True))
        ws = jnp.exp(b_last + a_c - m_new)
        wc = jnp.exp(b_last + m_prev - m_new)
        kt = k.astype(F32).T.astype(BF16)
        st_new = wc * st + _dot(kt, (ws * vext.astype(F32)).astype(BF16))
        st_ref[h] = st_new
        ms_ref[h] = jnp.broadcast_to(m_new, ms_ref.shape[1:])

    @pl.when(ci == pl.num_programs(1) - 1)
    def _():
        for h in range(ML_HEADS):
            stt = st_ref[h].T
            c_ref[h] = stt[:ML_D]
            n_ref[h:h + 1, :] = stt[ML_D:ML_D + 1]
            m_ref[:, h:h + 1] = ms_ref[h][0:1, 0:1]


def mlstm_prompt(q, k, v, gates, gates_t, gact, norm_w):
    b, t, _ = q.shape
    blk = pl.BlockSpec((None, ML_CHUNK, D_ML), lambda bi, c: (bi, c, 0))
    return pl.pallas_call(
        _mlstm_prompt_kernel,
        out_shape=(jax.ShapeDtypeStruct((b, t, D_ML), BF16),
                   jax.ShapeDtypeStruct((b, ML_HEADS, ML_D, ML_D), F32),
                   jax.ShapeDtypeStruct((b, ML_HEADS, ML_D), F32),
                   jax.ShapeDtypeStruct((b, 1, ML_HEADS), F32)),
        grid=(b, t // ML_CHUNK),
        in_specs=[blk, blk, blk,
                  pl.BlockSpec((None, ML_CHUNK, 2 * ML_HEADS), lambda bi, c: (bi, c, 0)),
                  pl.BlockSpec((None, 2 * ML_HEADS, ML_CHUNK), lambda bi, c: (bi, 0, c)),
                  pl.BlockSpec((None, ML_CHUNK, D_ML), lambda bi, c: (bi, c, 1)),
                  pl.BlockSpec((1, D_ML), lambda bi, c: (0, 0))],
        out_specs=(blk,
                   pl.BlockSpec((None, ML_HEADS, ML_D, ML_D), lambda bi, c: (bi, 0, 0, 0)),
                   pl.BlockSpec((None, ML_HEADS, ML_D), lambda bi, c: (bi, 0, 0)),
                   pl.BlockSpec((None, 1, ML_HEADS), lambda bi, c: (bi, 0, 0))),
        scratch_shapes=[pltpu.VMEM((ML_HEADS, ML_D, 2 * ML_D), F32), pltpu.VMEM((ML_HEADS, 8, 128), F32)],
        compiler_params=_params(("parallel", "arbitrary")),
        name="mlstm_prompt",
    )(q, k, v, gates, gates_t, gact, norm_w)


def _page_rows(ref):
    return ref[...].reshape(D_ATT, PAGE)


def _sb_decode_kernel(pt_ref, q_ref, g_ref, *rest, pages):
    k_refs, v_refs = rest[:pages], rest[pages:2 * pages]
    o_ref, acc_ref, carry_ref = rest[2 * pages:]
    j = pl.program_id(1)
    chunks = pages // 2

    @pl.when(j == 0)
    def _():
        acc_ref[...] = jnp.zeros_like(acc_ref)
        carry_ref[...] = jnp.zeros_like(carry_ref)

    rowi = lax.broadcasted_iota(I32, (8, D_ATT), 0)
    own_head = _head_of_lane((8, D_ATT), 1) == rowi
    qm = jnp.where(own_head, jnp.broadcast_to(q_ref[...].astype(F32), (8, D_ATT)), 0.0).astype(BF16)
    suffix = _suffix_matrix(KV_TILE)

    def tile(refs, c):
        return jnp.concatenate([_page_rows(refs[2 * c + 1]), _page_rows(refs[2 * c])], axis=1).astype(BF16)

    z = jnp.concatenate([_dot(qm, tile(k_refs, c)) for c in range(chunks)], axis=0)
    sp = _softplus(z)
    lf = -sp
    hi = lf.astype(BF16).astype(F32)
    both = _dot(jnp.concatenate([hi, lf - hi], axis=0).astype(BF16), suffix)
    later = both[:8 * chunks] + both[8 * chunks:]
    tot = jnp.sum(lf, axis=-1, keepdims=True)
    carries = [carry_ref[...]]
    for c in range(chunks):
        carries.append(carries[-1] + tot[8 * c:8 * (c + 1)])
    w = jnp.exp((z - sp) + later + jnp.concatenate(carries[:chunks], axis=0))
    acc = acc_ref[...]
    for c in range(chunks):
        acc = acc + _dot_nt(w[8 * c:8 * (c + 1)].astype(BF16), tile(v_refs, c))
    acc_ref[...] = acc
    carry_ref[...] = carries[chunks]

    @pl.when(j == pl.num_programs(1) - 1)
    def _():
        y = jnp.sum(jnp.where(own_head, acc, 0.0), axis=0, keepdims=True)
        o_ref[...] = (y * g_ref[...].astype(F32)).astype(BF16)


def _cache_page_spec(index_map):
    return pl.BlockSpec((None, None, N_HEADS, HEAD_DIM, PAGE), index_map)


def sb_decode(page_table, q, gact, cache_k, cache_v, layer, *, pages):
    b, n_pages = page_table.shape
    steps = n_pages // pages

    def page_spec(r):
        return _cache_page_spec(lambda bi, j, pt: (layer, pt[bi, n_pages - 1 - (j * pages + r)], 0, 0, 0))

    row = pl.BlockSpec((None, 1, D_ATT), lambda bi, j, pt: (bi, 0, 0))
    grid_spec = pltpu.PrefetchScalarGridSpec(
        num_scalar_prefetch=1,
        grid=(b, steps),
        in_specs=[row, row] + [page_spec(r) for r in range(pages)] * 2,
        out_specs=row,
        scratch_shapes=[pltpu.VMEM((8, D_ATT), F32), pltpu.VMEM((8, 1), F32)],
    )
    return pl.pallas_call(
        functools.partial(_sb_decode_kernel, pages=pages),
        out_shape=jax.ShapeDtypeStruct((b, 1, D_ATT), BF16),
        grid_spec=grid_spec,
        compiler_params=_params(("parallel", "arbitrary")),
        name="sb_decode",
    )(page_table, q, gact, *([cache_k] * pages), *([cache_v] * pages))


def _moba_gate_kernel(pt_ref, q_ref, *rest, pages):
    k_refs = rest[:pages]
    idx_ref, gs_ref = rest[pages:]
    j = pl.program_id(1)
    per_step = pages // 2
    qc = q_ref[...]
    for r in range(per_step):
        prod = (_page_rows(k_refs[2 * r]) + _page_rows(k_refs[2 * r + 1])) * qc
        for h in range(N_HEADS):
            gs_ref[h, pl.ds(j * per_step + r, 1), :] = jnp.sum(prod[h * HEAD_DIM:(h + 1) * HEAD_DIM], axis=0, keepdims=True)

    @pl.when(j == pl.num_programs(1) - 1)
    def _():
        nb = gs_ref.shape[1]
        lane = lax.broadcasted_iota(I32, (nb, 128), 1)
        gate = jnp.zeros((nb, 128), F32)
        for h in range(N_HEADS):
            gate = gate + jnp.where(lane == h, jnp.sum(gs_ref[h], axis=-1, keepdims=True) * (1.0 / KV_TILE), 0.0)
        blk = lax.broadcasted_iota(I32, (nb, 128), 0)
        rank = jnp.zeros((nb, 128), I32)
        for m in range(nb):
            gm = gate[m:m + 1, :]
            rank = rank + ((gm > gate) | ((gm == gate) & (blk > m))).astype(I32)
        blk_f = blk.astype(F32)
        rows = [jnp.sum(jnp.where(rank == t, blk_f, 0.0), axis=0, keepdims=True) for t in range(MOBA_TOPK)]
        idx_ref[...] = jnp.concatenate(rows + [jnp.zeros((8 - MOBA_TOPK, 128), F32)], axis=0).astype(I32)


def moba_decode_select(page_table, q_col, cache_k, layer, *, pages):
    b, n_pages = page_table.shape
    steps = n_pages // pages
    nb = n_pages * PAGE // KV_TILE

    def page_spec(r):
        return _cache_page_spec(lambda bi, j, pt: (layer, pt[bi, j * pages + r], 0, 0, 0))

    grid_spec = pltpu.PrefetchScalarGridSpec(
        num_scalar_prefetch=1,
        grid=(b, steps),
        in_specs=[pl.BlockSpec((None, D_ATT, 1), lambda bi, j, pt: (bi, 0, 0))] + [page_spec(r) for r in range(pages)],
        out_specs=pl.BlockSpec((None, 8, 128), lambda bi, j, pt: (bi, 0, 0)),
        scratch_shapes=[pltpu.VMEM((N_HEADS, nb, PAGE), F32)],
    )
    return pl.pallas_call(
        functools.partial(_moba_gate_kernel, pages=pages),
        out_shape=jax.ShapeDtypeStruct((b, 8, 128), I32),
        grid_spec=grid_spec,
        compiler_params=_params(("parallel", "arbitrary")),
        name="moba_decode_select",
    )(page_table, q_col, *([cache_k] * pages))


def _moba_decode_kernel(pt_ref, sel_ref, q_ref, kn_ref, vn_ref, bias_ref, g_ref, *rest, last_block):
    n_sel = 2 * MOBA_TOPK
    k_refs, v_refs = rest[:n_sel], rest[n_sel:2 * n_sel]
    o_ref, = rest[2 * n_sel:]
    bi, h = pl.program_id(0), pl.program_id(1)
    qf = jnp.broadcast_to(q_ref[...].astype(F32), (8, HEAD_DIM))
    q = qf.astype(BF16)
    s_self = jnp.sum(qf * kn_ref[...].astype(BF16).astype(F32), axis=-1, keepdims=True) + bias_ref[2][:, 0:1]
    scores = []
    for t in range(MOBA_TOPK):
        kt = jnp.concatenate([k_refs[2 * t][...], k_refs[2 * t + 1][...]], axis=1).astype(BF16)
        near = sel_ref[bi, t * N_HEADS + h] == last_block
        scores.append(_dot(q, kt) + bias_ref[jnp.where(near, 1, 0)])
    m = s_self
    for s in scores:
        m = jnp.maximum(m, jnp.max(s, axis=-1, keepdims=True))
    p_self = jnp.exp(s_self - m)
    l = p_self
    acc = p_self * vn_ref[...].astype(BF16).astype(F32)
    for t, s in enumerate(scores):
        p = jnp.exp(s - m)
        l = l + jnp.sum(p, axis=-1, keepdims=True)
        vt = jnp.concatenate([v_refs[2 * t][...], v_refs[2 * t + 1][...]], axis=1).astype(BF16)
        acc = acc + _dot_nt(p.astype(BF16), vt)
    o_ref[...] = ((acc / l)[0:1] * g_ref[...].astype(F32)).astype(BF16)


def moba_decode(page_table, sel, qh, k_new, v_new, dec_bias, gact_mb, cache_k, cache_v, layer):
    b, n_pages = page_table.shape
    last_block = n_pages * PAGE // KV_TILE - 1

    def page_spec(t, half):
        return pl.BlockSpec((None, None, None, HEAD_DIM, PAGE),
                            lambda bi, h, pt, sl: (layer, pt[bi, 2 * sl[bi, t * N_HEADS + h] + half], h, 0, 0))

    vec = pl.BlockSpec((None, None, 1, HEAD_DIM), lambda bi, h, pt, sl: (bi, h, 0, 0))
    pages = [page_spec(t, half) for t in range(MOBA_TOPK) for half in range(2)]
    grid_spec = pltpu.PrefetchScalarGridSpec(
        num_scalar_prefetch=2,
        grid=(b, N_HEADS),
        in_specs=[vec, vec, vec,
                  pl.BlockSpec((None, 3, 8, KV_TILE), lambda bi, h, pt, sl: (h, 0, 0, 0)),
                  vec] + pages + pages,
        out_specs=vec,
    )
    return pl.pallas_call(
        functools.partial(_moba_decode_kernel, last_block=last_block),
        out_shape=jax.ShapeDtypeStruct((b, N_HEADS, 1, HEAD_DIM), BF16),
        grid_spec=grid_spec,
        compiler_params=_params(("parallel", "parallel")),
        name="moba_decode",
    )(page_table, sel, qh, k_new, v_new, dec_bias, gact_mb, *([cache_k] * len(pages)), *([cache_v] * len(pages)))


def _mlstm_decode_kernel(q_ref, k_ref, v_ref, gt_ref, g_ref, nw_ref, c_ref, n_ref, m_ref,
                         h_ref, co_ref, no_ref, mo_ref):
    r = lax.broadcasted_iota(I32, (ML_D, ML_D), 0)
    c = lax.broadcasted_iota(I32, (ML_D, ML_D), 1)
    eye = r == c
    for h in range(ML_HEADS):
        sl = slice(h * ML_D, (h + 1) * ML_D)
        q = q_ref[:, sl].astype(F32)
        k = k_ref[:, sl].astype(F32)
        v = v_ref[:, sl].astype(F32)
        i_pre = gt_ref[:, h:h + 1]
        logf = gt_ref[:, ML_HEADS + h:ML_HEADS + h + 1]
        m_prev = m_ref[:, h:h + 1]
        cm = c_ref[h]
        nv = n_ref[h:h + 1, :]
        g = logf + m_prev
        m_new = jnp.maximum(g, i_pre)
        w_in = jnp.exp(i_pre - m_new)
        w_st = jnp.exp(g - m_new)
        qk = jnp.sum(q * k, axis=-1, keepdims=True)
        cq = _dot_nt(jnp.broadcast_to(q, (8, ML_D)), cm, precision=HI)[0:1]
        nq = jnp.sum(nv * q, axis=-1, keepdims=True)
        num = (qk * w_in) * v + w_st * cq
        den = qk * w_in + w_st * nq
        hout = num / jnp.maximum(jnp.abs(den), jnp.exp(-m_new))
        h_ref[:, sl] = _gated_head_norm(hout, nw_ref[:, sl], g_ref[:, sl])
        v_diag = jnp.where(eye, jnp.broadcast_to(v, (ML_D, ML_D)), 0.0)
        outer = jnp.dot(v_diag, jnp.broadcast_to(k, (ML_D, ML_D)), precision=HI, preferred_element_type=F32)
        co_ref[h] = w_st * cm + w_in * outer
        no_ref[h:h + 1, :] = w_st * nv + w_in * k
        mo_ref[:, h:h + 1] = m_new


def mlstm_decode(q, k, v, gates, gact, norm_w, c0, n0, m0):
    b = q.shape[0]
    vec = pl.BlockSpec((None, 1, D_ML), lambda bi: (bi, 0, 0))
    cs = pl.BlockSpec((None, ML_HEADS, ML_D, ML_D), lambda bi: (bi, 0, 0, 0))
    ns = pl.BlockSpec((None, ML_HEADS, ML_D), lambda bi: (bi, 0, 0))
    msp = pl.BlockSpec((None, 1, ML_HEADS), lambda bi: (bi, 0, 0))
    return pl.pallas_call(
        _mlstm_decode_kernel,
        out_shape=(jax.ShapeDtypeStruct((b, 1, D_ML), BF16),
                   jax.ShapeDtypeStruct(c0.shape, F32),
                   jax.ShapeDtypeStruct(n0.shape, F32),
                   jax.ShapeDtypeStruct(m0.shape, F32)),
        grid=(b,),
        in_specs=[vec, vec, vec,
                  pl.BlockSpec((None, 1, 2 * ML_HEADS), lambda bi: (bi, 0, 0)),
                  pl.BlockSpec((None, 1, D_ML), lambda bi: (bi, 0, 1)),
                  pl.BlockSpec((1, D_ML), lambda bi: (0, 0)),
                  cs, ns, msp],
        out_specs=(vec, cs, ns, msp),
        compiler_params=_params(("parallel",)),
        name="mlstm_decode",
    )(q, k, v, gates, gact, norm_w, c0, n0, m0)


SB_DECODE_PAGES = 8
MOBA_SELECT_PAGES = 16
PROMPT_TM = 512


def _layer_weights(w_in_l, w_out_l):
    w_bf = w_in_l.astype(BF16)
    w_lo = (w_in_l[:, O_QMB:O_VMB] - w_bf[:, O_QMB:O_VMB].astype(F32)).astype(BF16)
    return w_bf, w_lo, w_out_l.astype(BF16)


def kernel(x_prompt, x_sample, cache_sb_k, cache_sb_v, cache_moba_k, cache_moba_v, state_mlstm_C, state_mlstm_n,
           state_mlstm_m, page_table, c_prompt, c_sample, norm_w, w_ada, b_ada, w_in, b_gates, q_norm_w, k_norm_w,
           rel_bias, ml_norm_w, w_out):
    bp, tp, d = x_prompt.shape
    bs = x_sample.shape[0]
    depth = w_in.shape[0]
    mp = bp * tp
    n_pages = page_table.shape[1]

    mod = ada_modulation(jnp.concatenate([c_prompt, c_sample], axis=0), w_ada, b_ada)
    bias_p, bias_d = bias_tiles(rel_bias)
    lanes_last = lambda a: jnp.transpose(a, (0, 1, 3, 4, 2))
    sbk, sbv, mbk, mbv = (lanes_last(a) for a in (cache_sb_k, cache_sb_v, cache_moba_k, cache_moba_v))

    xp = x_prompt.reshape(mp, d)
    xs = x_sample.reshape(bs, d)
    p_out = [[] for _ in range(7)]
    s_out = [[] for _ in range(7)]
    for l in range(depth):
        w_bf, w_lo, wo_bf = _layer_weights(w_in[l], w_out[l])
        nw = norm_w[l].reshape(1, d)
        bg = b_gates[l].reshape(1, 2 * ML_HEADS)
        qnw = jnp.tile(q_norm_w[l], N_HEADS).reshape(1, D_ATT)
        knw = jnp.tile(k_norm_w[l], N_HEADS).reshape(1, D_ATT)
        mlw = ml_norm_w[l].reshape(1, D_ML)
        shift, scale, gate = jnp.split(mod[l], 3, axis=-1)

        pm = lambda a: a[:bp].reshape(bp, 1, d)
        (qsb_t, ksb_t, ksb, vsb_t, vsb_th, qmb_t, qmb_th, kmb_t, kmb, vmb_t, vmb_th,
         qml, kml, vml, gact, gates, kmean) = \
            input_projection(xp, nw, pm(scale), pm(shift), w_bf, w_lo, bg, qnw, knw,
                             tm=PROMPT_TM, rows_per_mod=tp // PROMPT_TM, prompt=True)
        seq = lambda a: a.reshape(bp, tp, a.shape[-1])
        pages = lambda a: a.reshape(bp, tp // PAGE, D_ATT, PAGE)
        gact3 = seq(gact)
        ysb = sb_prompt(pages(qsb_t), seq(ksb), pages(vsb_th), gact3)
        ymb = moba_prompt(pages(qmb_t), pages(qmb_th), seq(kmb), pages(vmb_th),
                          kmean.reshape(bp, tp // KV_TILE, D_ATT), bias_p, gact3)
        gates3 = seq(gates)
        yml, c_p, n_p, m_p = mlstm_prompt(seq(qml), seq(kml), seq(vml), gates3, gates3.swapaxes(1, 2), gact3, mlw)
        xp = output_projection(ysb.reshape(mp, D_ATT), ymb.reshape(mp, D_ATT), yml.reshape(mp, D_ML), xp, pm(gate),
                               wo_bf, tm=PROMPT_TM, rows_per_mod=tp // PROMPT_TM)
        paged = lambda a: jnp.transpose(a.reshape(bp, tp // PAGE, N_HEADS, HEAD_DIM, PAGE), (0, 1, 4, 2, 3))
        for lst, a in zip(p_out, (paged(ksb_t), paged(vsb_t), paged(kmb_t), paged(vmb_t), c_p, n_p,
                                  m_p.reshape(bp, ML_HEADS))):
            lst.append(a)

        sm = lambda a: a[bp:].reshape(1, bs, d)
        (qsb, ksb, vsb, qmb, qmbh, kmb, vmb, qml, kml, vml, gact, gates) = \
            input_projection(xs, nw, sm(scale), sm(shift), w_bf, w_lo, bg, qnw, knw,
                             tm=bs, rows_per_mod=1, prompt=False)
        tok = lambda a: a.reshape(bs, 1, a.shape[-1])
        heads = lambda a: a.reshape(bs, N_HEADS, 1, HEAD_DIM)
        gact3 = tok(gact)
        ysb = sb_decode(page_table, tok(qsb), gact3[:, :, :D_ATT], sbk, sbv, l, pages=min(SB_DECODE_PAGES, n_pages))
        sel = moba_decode_select(page_table, qmb.reshape(bs, D_ATT, 1), mbk, l, pages=min(MOBA_SELECT_PAGES, n_pages))
        sel = sel[:, :MOBA_TOPK, :N_HEADS].reshape(bs, MOBA_TOPK * N_HEADS)
        ymb = moba_decode(page_table, sel, heads(qmbh), heads(kmb), heads(vmb), bias_d,
                          heads(gact[:, D_ATT:2 * D_ATT]), mbk, mbv, l)
        yml, c_s, n_s, m_s = mlstm_decode(tok(qml), tok(kml), tok(vml), tok(gates), gact3, mlw,
                                          state_mlstm_C[l], state_mlstm_n[l], state_mlstm_m[l].reshape(bs, 1, ML_HEADS))
        xs = output_projection(ysb.reshape(bs, D_ATT), ymb.reshape(bs, D_ATT), yml.reshape(bs, D_ML), xs, sm(gate),
                               wo_bf, tm=bs, rows_per_mod=1)
        new = lambda a: a.reshape(bs, 1, N_HEADS, HEAD_DIM)
        for lst, a in zip(s_out, (new(ksb), new(vsb), new(kmb), new(vmb), c_s, n_s, m_s.reshape(bs, ML_HEADS))):
            lst.append(a)

    P = [jnp.stack(a) for a in p_out]
    S = [jnp.stack(a) for a in s_out]
    return (xp.reshape(bp, tp, d), xs.reshape(bs, 1, d),
            P[0], P[1], P[2], P[3], P[4], P[5], P[6], S[0], S[1], S[2], S[3], S[4], S[5], S[6])
```

```python
import functools
import math

import numpy as np
import jax
import jax.numpy as jnp
from jax import lax
from jax.experimental import pallas as pl
from jax.experimental.pallas import tpu as pltpu

F32 = jnp.float32
BF16 = jnp.bfloat16
I32 = jnp.int32
HI = lax.Precision.HIGHEST

EPS = 1e-6
HEAD_DIM = 64
N_HEADS = 4
D_ATT = N_HEADS * HEAD_DIM
ML_HEADS = 4
ML_D = 128
D_ML = ML_HEADS * ML_D
Q_BLOCK = 128
KV_TILE = 256
MOBA_TOPK = 3
N_BUCKETS = 32
MAX_DISTANCE = 128
ML_CHUNK = 128
PAGE = 128
HEAD_SHIFT = 6
NEG_INF = float("-inf")

VMEM_LIMIT = 56 * 1024 * 1024

_SIZES = [D_ATT] * 8 + [D_ML] * 5 + [ML_HEADS] * 2
_OFF = np.concatenate([[0], np.cumsum(_SIZES)]).tolist()
(O_QSB, O_KSB, O_VSB, O_GSB, O_QMB, O_KMB, O_VMB, O_GMB,
 O_QML, O_KML, O_VML, O_OML, O_GML, O_IML, O_FML, O_END) = _OFF


def _params(sem, vmem=VMEM_LIMIT):
    return pltpu.CompilerParams(dimension_semantics=sem, vmem_limit_bytes=vmem)


def _dot(a, b):
    return jnp.dot(a, b, preferred_element_type=F32)


def _dot_nt(a, b, precision=None):
    return lax.dot_general(a, b, (((1,), (1,)), ((), ())), precision=precision,
                           preferred_element_type=F32)


def _softplus(z):
    return jnp.maximum(z, 0.0) + jnp.log(1.0 + jnp.exp(-jnp.abs(z)))


def _head_of_lane(shape, axis):
    return lax.shift_right_logical(lax.broadcasted_iota(I32, shape, axis), HEAD_SHIFT)


def _ada_kernel(c_ref, w_ref, b_ref, o_ref):
    c = c_ref[...]
    a = c * jax.nn.sigmoid(c)
    o_ref[...] = jnp.dot(a, w_ref[...], precision=HI, preferred_element_type=F32) + b_ref[...]


def ada_modulation(c_all, w_ada, b_ada):
    depth, d, d3 = w_ada.shape
    rows = c_all.shape[0]
    nt = d3 // d
    return pl.pallas_call(
        _ada_kernel,
        out_shape=jax.ShapeDtypeStruct((depth, rows, d3), F32),
        grid=(depth, nt),
        in_specs=[pl.BlockSpec((rows, d), lambda l, j: (0, 0)),
                  pl.BlockSpec((None, d, d), lambda l, j: (l, 0, j)),
                  pl.BlockSpec((None, 1, d), lambda l, j: (l, 0, j))],
        out_specs=pl.BlockSpec((None, rows, d), lambda l, j: (l, 0, j)),
        compiler_params=_params(("parallel", "parallel")),
        name="ada_modulation",
    )(c_all, w_ada, b_ada.reshape(depth, 1, d3))


def _bucket_thresholds():
    d = np.arange(0, 4 * MAX_DISTANCE, dtype=np.int32)
    max_exact = N_BUCKETS // 2
    df = np.maximum(d, 1).astype(np.float32)
    large = max_exact + (np.log(df / np.float32(max_exact)) / np.float32(math.log(MAX_DISTANCE / max_exact))
                         * np.float32(N_BUCKETS - max_exact)).astype(np.int32)
    large = np.minimum(large, N_BUCKETS - 1)
    bucket = np.where(d < max_exact, d, large)
    thr = []
    for k in range(1, N_BUCKETS):
        idx = np.nonzero(bucket >= k)[0]
        thr.append(int(idx[0]))
    assert all(np.all((bucket >= k) == (d >= t)) for k, t in zip(range(1, N_BUCKETS), thr))
    return thr


_BUCKET_THR = _bucket_thresholds()

_PATTERN_OFFSETS = (0, Q_BLOCK, KV_TILE)
P_FAR = len(_PATTERN_OFFSETS)


def _bias_from_dist(dist, rb_ref, h):
    b = jnp.full(dist.shape, rb_ref[0, h], F32)
    for k, t in zip(range(1, N_BUCKETS), _BUCKET_THR):
        b = jnp.where(dist >= t, rb_ref[k, h], b)
    return jnp.where(dist >= 0, b, NEG_INF)


def _bias_kernel(rb_ref, tile_ref, dec_ref):
    j = lax.broadcasted_iota(I32, (KV_TILE, Q_BLOCK), 0)
    i = lax.broadcasted_iota(I32, (KV_TILE, Q_BLOCK), 1)
    for p, off in enumerate(_PATTERN_OFFSETS):
        for h in range(N_HEADS):
            tile_ref[p, :, h * Q_BLOCK:(h + 1) * Q_BLOCK] = _bias_from_dist(off + i - j, rb_ref, h)
    for h in range(N_HEADS):
        tile_ref[P_FAR, :, h * Q_BLOCK:(h + 1) * Q_BLOCK] = jnp.full((KV_TILE, Q_BLOCK), rb_ref[N_BUCKETS - 1, h], F32)
    jr = lax.broadcasted_iota(I32, (8, KV_TILE), 1)
    for h in range(N_HEADS):
        dec_ref[h, 0] = jnp.full((8, KV_TILE), rb_ref[N_BUCKETS - 1, h], F32)
        dec_ref[h, 1] = _bias_from_dist(KV_TILE - jr, rb_ref, h)
        dec_ref[h, 2] = jnp.full((8, KV_TILE), rb_ref[0, h], F32)


def bias_tiles(rel_bias):
    return pl.pallas_call(
        _bias_kernel,
        out_shape=(jax.ShapeDtypeStruct((P_FAR + 1, KV_TILE, N_HEADS * Q_BLOCK), F32),
                   jax.ShapeDtypeStruct((N_HEADS, 3, 8, KV_TILE), F32)),
        in_specs=[pl.BlockSpec(memory_space=pltpu.SMEM)],
        name="bias_tiles",
    )(rel_bias)


def _inproj_kernel(x_ref, nw_ref, sc_ref, sh_ref, w_ref, wlo_ref, bg_ref, qnw_ref, knw_ref, *outs, prompt):
    outs = list(outs)

    def paged(p):
        return [p[pg * PAGE:(pg + 1) * PAGE].T for pg in range(p.shape[0] // PAGE)]

    def emit(p, *, rows_bf16=False, rows_f32=False, pages_f32=False, pages_bf16=False):
        pages = paged(p) if (pages_f32 or pages_bf16) else None
        for want, dtype, is_pages in ((pages_f32, F32, True), (rows_f32, F32, False),
                                      (pages_bf16, BF16, True), (rows_bf16, BF16, False)):
            if not want:
                continue
            ref = outs.pop(0)
            if is_pages:
                for pg, t in enumerate(pages):
                    ref[pg] = t.astype(dtype)
            else:
                ref[...] = p.astype(dtype)

    x = x_ref[...]
    ms = jnp.mean(x * x, axis=-1, keepdims=True)
    h = (x * lax.rsqrt(ms + EPS) * nw_ref[...]) * (1.0 + sc_ref[...]) + sh_ref[...]
    hb = h.astype(BF16)
    hlo = (h - hb.astype(F32)).astype(BF16)

    def proj(a, b):
        return _dot(hb, w_ref[:, a:b])

    def proj3(a, b, la, lb):
        wh = w_ref[:, a:b]
        return _dot(hb, wh) + (_dot(hlo, wh) + _dot(hb, wlo_ref[:, la:lb]))

    def silu(g):
        return g * jax.nn.sigmoid(g)

    r = lax.broadcasted_iota(I32, (D_ATT, D_ATT), 0)
    c = lax.broadcasted_iota(I32, (D_ATT, D_ATT), 1)
    same_head = lax.shift_right_logical(r, HEAD_SHIFT) == lax.shift_right_logical(c, HEAD_SHIFT)
    head_mean = jnp.where(same_head, 1.0 / HEAD_DIM, 0.0).astype(BF16)

    def head_norm(p, w):
        sq = p * p
        hi = sq.astype(BF16)
        lo = (sq - hi.astype(F32)).astype(BF16)
        ms = _dot(hi, head_mean) + _dot(lo, head_mean)
        return p * lax.rsqrt(ms + EPS) * w

    score_scale = HEAD_DIM ** -0.5
    emit(proj(O_QSB, O_KSB) * score_scale, pages_bf16=prompt, rows_bf16=not prompt)
    emit(proj(O_KSB, O_VSB), pages_f32=prompt, rows_bf16=prompt, rows_f32=not prompt)
    emit(proj(O_VSB, O_GSB), pages_f32=prompt, pages_bf16=prompt, rows_f32=not prompt)
    g_sb = silu(proj(O_GSB, O_QMB))
    qn = head_norm(proj3(O_QMB, O_KMB, 0, D_ATT), qnw_ref[...])
    if prompt:
        qt = paged(qn)
        ref_f, ref_h = outs.pop(0), outs.pop(0)
        for pg, t in enumerate(qt):
            ref_f[pg] = t
            ref_h[pg] = (t * score_scale).astype(BF16)
    else:
        emit(qn, rows_f32=True)
        emit(qn * score_scale, rows_bf16=True)
    kn = head_norm(proj3(O_KMB, O_VMB, D_ATT, 2 * D_ATT), knw_ref[...])
    emit(kn, pages_f32=prompt, rows_bf16=prompt, rows_f32=not prompt)
    emit(proj(O_VMB, O_GMB), pages_f32=prompt, pages_bf16=prompt, rows_f32=not prompt)
    g_mb = silu(proj(O_GMB, O_QML))
    emit(proj(O_QML, O_KML), rows_bf16=True)
    emit(proj(O_KML, O_VML) * (ML_D ** -0.5), rows_bf16=True)
    emit(proj(O_VML, O_OML), rows_bf16=True)
    o = proj(O_OML, O_GML)
    g_ml = jax.nn.sigmoid(o) * silu(proj(O_GML, O_IML))
    gact_ref = outs.pop(0)
    gact_ref[:, 0:D_ATT] = g_sb.astype(BF16)
    gact_ref[:, D_ATT:2 * D_ATT] = g_mb.astype(BF16)
    gact_ref[:, 2 * D_ATT:] = g_ml.astype(BF16)
    pre = proj(O_IML, O_END) + bg_ref[...]
    is_f = lax.broadcasted_iota(I32, pre.shape, 1) >= ML_HEADS
    logf = jnp.minimum(pre, 0.0) - jnp.log1p(jnp.exp(-jnp.abs(pre)))
    outs.pop(0)[...] = jnp.where(is_f, logf, pre)
    if prompt:
        kmean_ref = outs.pop(0)
        for i in range(kn.shape[0] // KV_TILE):
            kmean_ref[i] = jnp.mean(kn[i * KV_TILE:(i + 1) * KV_TILE], axis=0, keepdims=True)
    assert not outs


def input_projection(x, norm_w, scale, shift, w_bf, w_lo, b_gates, qnw, knw, *, tm, rows_per_mod, prompt):
    m, d = x.shape
    n_tiles = m // tm
    rmod = scale.shape[1]
    row = lambda i: (i, 0)
    const = lambda i: (0, 0)
    mod = lambda i: (i // rows_per_mod, 0, 0)

    def rows(width, dtype):
        return jax.ShapeDtypeStruct((m, width), dtype), pl.BlockSpec((tm, width), row)

    def pages(dtype):
        return (jax.ShapeDtypeStruct((m // PAGE, D_ATT, PAGE), dtype),
                pl.BlockSpec((tm // PAGE, D_ATT, PAGE), lambda i: (i, 0, 0)))

    if prompt:
        nb = tm // KV_TILE
        outs = [pages(BF16), pages(F32), rows(D_ATT, BF16), pages(F32), pages(BF16),
                pages(F32), pages(BF16), pages(F32), rows(D_ATT, BF16), pages(F32), pages(BF16)]
        tail = [(jax.ShapeDtypeStruct((n_tiles, nb, 1, D_ATT), F32),
                 pl.BlockSpec((None, nb, 1, D_ATT), lambda i: (i, 0, 0, 0)))]
    else:
        outs = [rows(D_ATT, BF16), rows(D_ATT, F32), rows(D_ATT, F32),
                rows(D_ATT, F32), rows(D_ATT, BF16), rows(D_ATT, F32), rows(D_ATT, F32)]
        tail = []
    outs += [rows(D_ML, BF16), rows(D_ML, BF16), rows(D_ML, BF16),
             rows(2 * D_ATT + D_ML, BF16), rows(2 * ML_HEADS, F32)] + tail
    return pl.pallas_call(
        functools.partial(_inproj_kernel, prompt=prompt),
        out_shape=tuple(o[0] for o in outs),
        grid=(n_tiles,),
        in_specs=[pl.BlockSpec((tm, d), row),
                  pl.BlockSpec((1, d), const),
                  pl.BlockSpec((None, rmod, d), mod),
                  pl.BlockSpec((None, rmod, d), mod),
                  pl.BlockSpec(w_bf.shape, const),
                  pl.BlockSpec(w_lo.shape, const),
                  pl.BlockSpec((1, 2 * ML_HEADS), const),
                  pl.BlockSpec((1, D_ATT), const),
                  pl.BlockSpec((1, D_ATT), const)],
        out_specs=tuple(o[1] for o in outs),
        compiler_params=_params(("parallel",)),
        name="input_projection",
    )(x, norm_w, scale, shift, w_bf, w_lo, b_gates, qnw, knw)


def _outproj_kernel(ysb_ref, ymb_ref, yml_ref, x_ref, gate_ref, w_ref, o_ref):
    y = (_dot(ysb_ref[...], w_ref[0:D_ATT, :]) + _dot(ymb_ref[...], w_ref[D_ATT:2 * D_ATT, :])
         + _dot(yml_ref[...], w_ref[2 * D_ATT:, :]))
    o_ref[...] = x_ref[...] + gate_ref[...] * y


def output_projection(ysb, ymb, yml, x, gate, w_bf, *, tm, rows_per_mod):
    m, d = x.shape
    rmod = gate.shape[1]
    row = lambda i: (i, 0)
    return pl.pallas_call(
        _outproj_kernel,
        out_shape=jax.ShapeDtypeStruct((m, d), F32),
        grid=(m // tm,),
        in_specs=[pl.BlockSpec((tm, D_ATT), row), pl.BlockSpec((tm, D_ATT), row), pl.BlockSpec((tm, D_ML), row),
                  pl.BlockSpec((tm, d), row),
                  pl.BlockSpec((None, rmod, d), lambda i: (i // rows_per_mod, 0, 0)),
                  pl.BlockSpec(w_bf.shape, lambda i: (0, 0))],
        out_specs=pl.BlockSpec((tm, d), row),
        compiler_params=_params(("parallel",)),
        name="output_projection",
    )(ysb, ymb, yml, x, gate, w_bf)


def _suffix_matrix(n, transposed=False):
    r = lax.broadcasted_iota(I32, (n, n), 0)
    c = lax.broadcasted_iota(I32, (n, n), 1)
    return jnp.where((c > r) if transposed else (r > c), 1.0, 0.0).astype(BF16)


def _heads_on_lanes(qt):
    head = _head_of_lane((D_ATT, 1), 0)
    return jnp.concatenate([jnp.where(head == h, qt, jnp.zeros_like(qt)) for h in range(N_HEADS)], axis=1)


def _heads_from_lanes(acc_t, q):
    head = _head_of_lane((D_ATT, 1), 0)
    out = jnp.zeros((D_ATT, q), F32)
    for h in range(N_HEADS):
        out = out + jnp.where(head == h, acc_t[:, h * q:(h + 1) * q], 0.0)
    return out.T


def _kv_tile(k_ref, vt_ref, n):
    start = pl.multiple_of(n * KV_TILE, KV_TILE)
    vt = jnp.concatenate([vt_ref[2 * n], vt_ref[2 * n + 1]], axis=1)
    return k_ref[pl.ds(start, KV_TILE), :], vt


def _sb_tile_t(q4t, k, vt, upper, carry, mask):
    z = _dot(k, q4t)
    sp = _softplus(z)
    if mask is not None:
        sp_sum = jnp.where(mask, sp, 0.0)
    else:
        sp_sum = sp
    later = _dot(upper, sp_sum.astype(BF16))
    w = jnp.exp(((z - sp) - later) - carry)
    if mask is not None:
        w = jnp.where(mask, w, 0.0)
    return _dot(vt, w.astype(BF16)), carry + jnp.sum(sp_sum, axis=0, keepdims=True)


def _sb_prompt_kernel(qt_ref, k_ref, vt_ref, g_ref, o_ref, acc_ref, carry_ref):
    i = pl.program_id(1)
    cols = N_HEADS * Q_BLOCK
    q4t = _heads_on_lanes(qt_ref[...])
    upper = _suffix_matrix(KV_TILE, transposed=True)
    last = (i * Q_BLOCK) // KV_TILE
    kpos = last * KV_TILE + lax.broadcasted_iota(I32, (KV_TILE, cols), 0)
    qpos = i * Q_BLOCK + (lax.broadcasted_iota(I32, (KV_TILE, cols), 1) & (Q_BLOCK - 1))
    k, vt = _kv_tile(k_ref, vt_ref, last)
    pv, carry = _sb_tile_t(q4t, k, vt, upper, jnp.zeros((1, cols), F32), kpos < qpos)
    acc_ref[...] = pv
    carry_ref[...] = carry

    @pl.when(last % 2 == 1)
    def _():
        k, vt = _kv_tile(k_ref, vt_ref, last - 1)
        pv, carry = _sb_tile_t(q4t, k, vt, upper, carry_ref[...], None)
        acc_ref[...] += pv
        carry_ref[...] = carry

    first = last - 1 - last % 2

    def body(s, carry):
        ka, vta = _kv_tile(k_ref, vt_ref, first - 2 * s)
        kb, vtb = _kv_tile(k_ref, vt_ref, first - 2 * s - 1)
        pva, carry = _sb_tile_t(q4t, ka, vta, upper, carry, None)
        pvb, carry = _sb_tile_t(q4t, kb, vtb, upper, carry, None)
        acc_ref[...] += pva + pvb
        return carry

    lax.fori_loop(0, last // 2, body, carry_ref[...])
    o_ref[...] = (_heads_from_lanes(acc_ref[...], Q_BLOCK) * g_ref[...].astype(F32)).astype(BF16)


def sb_prompt(qt, k, vt, gact):
    b, t, _ = k.shape
    n_pages = t // PAGE
    blk = pl.BlockSpec((None, Q_BLOCK, D_ATT), lambda bi, i: (bi, i, 0))
    return pl.pallas_call(
        _sb_prompt_kernel,
        out_shape=jax.ShapeDtypeStruct((b, t, D_ATT), BF16),
        grid=(b, t // Q_BLOCK),
        in_specs=[pl.BlockSpec((None, None, D_ATT, PAGE), lambda bi, i: (bi, i, 0, 0)),
                  pl.BlockSpec((None, t, D_ATT), lambda bi, i: (bi, 0, 0)),
                  pl.BlockSpec((None, n_pages, D_ATT, PAGE), lambda bi, i: (bi, 0, 0, 0)),
                  blk],
        out_specs=blk,
        scratch_shapes=[pltpu.VMEM((D_ATT, N_HEADS * Q_BLOCK), F32), pltpu.VMEM((1, N_HEADS * Q_BLOCK), F32)],
        compiler_params=_params(("parallel", "arbitrary")),
        name="sb_prompt",
    )(qt, k, vt, gact)


def _moba_select_t(gate, valid, idx):
    n = gate.shape[0]
    gate = jnp.where(valid, gate, NEG_INF)
    rank = jnp.zeros(gate.shape, I32)
    for m in range(n):
        gm = gate[m:m + 1, :]
        beats = (gm > gate) | ((gm == gate) & (idx > m))
        rank = rank + beats.astype(I32)
    return valid & (rank < MOBA_TOPK)


def _moba_prompt_kernel(qft_ref, qht_ref, k_ref, vt_ref, km_ref, bias_ref, g_ref, o_ref, acc_ref, m_ref, l_ref, sel_ref):
    i = pl.program_id(1)
    nb = km_ref.shape[0]
    own = (i * Q_BLOCK) // KV_TILE
    odd = (i * Q_BLOCK) % KV_TILE != 0
    q4t = _heads_on_lanes(qht_ref[...])
    gate = jnp.dot(km_ref[...], _heads_on_lanes(qft_ref[...]), precision=HI, preferred_element_type=F32)
    idx = lax.broadcasted_iota(I32, (nb, 1), 0)
    sel = _moba_select_t(gate, idx < own, idx)
    sel_ref[...] = jnp.where(sel, 0.0, NEG_INF)

    def attend(blocks, first=False):
        tiles = [_kv_tile(k_ref, vt_ref, n) for n, _ in blocks]
        scores = []
        for (k, _), (_, terms) in zip(tiles, blocks):
            s = _dot(k, q4t)
            for t in terms:
                s = s + t
            scores.append(s)
        m_new = functools.reduce(jnp.maximum, [jnp.max(s, axis=0, keepdims=True) for s in scores])
        if not first:
            m_old = m_ref[...]
            m_new = jnp.maximum(m_old, m_new)
            alpha = jnp.exp(m_old - m_new)
        ps = [jnp.exp(s - m_new) for s in scores]
        l_new = functools.reduce(jnp.add, [jnp.sum(p, axis=0, keepdims=True) for p in ps])
        pv = functools.reduce(jnp.add, [_dot(vt, p.astype(BF16)) for (_, vt), p in zip(tiles, ps)])
        m_ref[...] = m_new
        l_ref[...] = l_new if first else alpha * l_ref[...] + l_new
        acc_ref[...] = pv if first else alpha * acc_ref[...] + pv

    attend([(own, [bias_ref[jnp.where(odd, 1, 0)]])], first=True)
    near_pat = jnp.where(odd, P_FAR, 2)
    far_row = bias_ref[P_FAR, 0:1, :]
    for n in range(0, nb - 2, 2):
        @pl.when(n + 1 < own)
        def _():
            attend([(n, [far_row + sel_ref[n:n + 1, :]]),
                    (n + 1, [bias_ref[jnp.where(n + 1 == own - 1, near_pat, P_FAR)], sel_ref[n + 1:n + 2, :]])])

    @pl.when(own % 2 == 1)
    def _():
        attend([(own - 1, [bias_ref[near_pat], sel_ref[pl.ds(own - 1, 1), :]])])

    y = _heads_from_lanes(acc_ref[...] / l_ref[...], Q_BLOCK)
    o_ref[...] = (y * g_ref[...].astype(F32)).astype(BF16)


def moba_prompt(qft, qht, k, vt, kmean, bias, gact):
    b, t, _ = k.shape
    nb = t // KV_TILE
    cols = N_HEADS * Q_BLOCK
    blk = pl.BlockSpec((None, Q_BLOCK, D_ATT), lambda bi, i: (bi, i, 0))
    qpage = pl.BlockSpec((None, None, D_ATT, PAGE), lambda bi, i: (bi, i, 0, 0))
    return pl.pallas_call(
        _moba_prompt_kernel,
        out_shape=jax.ShapeDtypeStruct((b, t, D_ATT), BF16),
        grid=(b, t // Q_BLOCK),
        in_specs=[qpage, qpage,
                  pl.BlockSpec((None, t, D_ATT), lambda bi, i: (bi, 0, 0)),
                  pl.BlockSpec((None, t // PAGE, D_ATT, PAGE), lambda bi, i: (bi, 0, 0, 0)),
                  pl.BlockSpec((None, nb, D_ATT), lambda bi, i: (bi, 0, 0)),
                  pl.BlockSpec(bias.shape, lambda bi, i: (0, 0, 0)),
                  pl.BlockSpec((None, Q_BLOCK, D_ATT), lambda bi, i: (bi, i, 1))],
        out_specs=blk,
        scratch_shapes=[pltpu.VMEM((D_ATT, cols), F32), pltpu.VMEM((1, cols), F32), pltpu.VMEM((1, cols), F32),
                        pltpu.VMEM((nb, cols), F32)],
        compiler_params=_params(("parallel", "arbitrary")),
        name="moba_prompt",
    )(qft, qht, k, vt, kmean, bias, gact)


def _gated_head_norm(h, w, g):
    hn = h * lax.rsqrt(jnp.mean(h * h, axis=-1, keepdims=True) + EPS) * w
    return (hn * g.astype(F32)).astype(BF16)


def _mlstm_prompt_kernel(q_ref, k_ref, v_ref, gc_ref, gr_ref, g_ref, nw_ref, h_ref, c_ref, n_ref, m_ref,
                         st_ref, ms_ref):
    ci = pl.program_id(1)
    L = ML_CHUNK

    @pl.when(ci == 0)
    def _():
        st_ref[...] = jnp.zeros_like(st_ref)
        ms_ref[...] = jnp.zeros_like(ms_ref)

    t = lax.broadcasted_iota(I32, (L, L), 0)
    s = lax.broadcasted_iota(I32, (L, L), 1)
    causal = s <= t
    lower = jnp.where(causal, 1.0, 0.0).astype(F32)
    upper = jnp.where(t <= s, 1.0, 0.0).astype(F32)
    gc = gc_ref[...]
    gr = gr_ref[...]
    bh_c = jnp.dot(lower, gc, precision=HI, preferred_element_type=F32)
    bh_r = jnp.dot(gr, upper, precision=HI, preferred_element_type=F32)
    ones_col = jnp.where(lax.broadcasted_iota(I32, (L, ML_D), 1) == 0, 1.0, 0.0).astype(BF16)

    for h in range(ML_HEADS):
        sl = slice(h * ML_D, (h + 1) * ML_D)
        q, k, v = q_ref[:, sl], k_ref[:, sl], v_ref[:, sl]
        vext = jnp.concatenate([v, ones_col], axis=1)
        m_prev = ms_ref[h][0:1, 0:1]
        b_c = bh_c[:, ML_HEADS + h:ML_HEADS + h + 1]
        a_c = gc[:, h:h + 1] - b_c
        a_r = gr[h:h + 1, :] - bh_r[ML_HEADS + h:ML_HEADS + h + 1, :]
        dm = jnp.where(causal, b_c + a_r, NEG_INF)
        g = b_c + m_prev
        m_row = jnp.maximum(g, jnp.max(dm, axis=-1, keepdims=True))
        w_intra = jnp.exp(dm - m_row)
        w_inter = jnp.exp(g - m_row)
        sc = _dot_nt(q, k) * w_intra
        st = st_ref[h]
        numden = _dot(sc.astype(BF16), vext) + w_inter * _dot(q, st.astype(BF16))
        den = numden[:, ML_D:ML_D + 1]
        hout = numden[:, :ML_D] / jnp.maximum(jnp.abs(den), jnp.exp(-m_row))
        h_ref[:, sl] = _gated_head_norm(hout, nw_ref[:, sl], g_ref[:, sl])

        b_last = b_c[L - 1:L, :]
        m_new = jnp.maximum(b_last + m_prev, jnp.max(b_last + a_r, axis=-1, keepdims=True))
        ws = jnp.exp(b_last + a_c - m_new)
        wc = jnp.exp(b_last + m_prev - m_new)
        kt = k.astype(F32).T.astype(BF16)
        st_new = wc * st + _dot(kt, (ws * vext.astype(F32)).astype(BF16))
        st_ref[h] = st_new
        ms_ref[h] = jnp.broadcast_to(m_new, ms_ref.shape[1:])

    @pl.when(ci == pl.num_programs(1) - 1)
    def _():
        for h in range(ML_HEADS):
            stt = st_ref[h].T
            c_ref[h] = stt[:ML_D]
            n_ref[h:h + 1, :] = stt[ML_D:ML_D + 1]
            m_ref[:, h:h + 1] = ms_ref[h][0:1, 0:1]


def mlstm_prompt(q, k, v, gates, gates_t, gact, norm_w):
    b, t, _ = q.shape
    blk = pl.BlockSpec((None, ML_CHUNK, D_ML), lambda bi, c: (bi, c, 0))
    return pl.pallas_call(
        _mlstm_prompt_kernel,
        out_shape=(jax.ShapeDtypeStruct((b, t, D_ML), BF16),
                   jax.ShapeDtypeStruct((b, ML_HEADS, ML_D, ML_D), F32),
                   jax.ShapeDtypeStruct((b, ML_HEADS, ML_D), F32),
                   jax.ShapeDtypeStruct((b, 1, ML_HEADS), F32)),
        grid=(b, t // ML_CHUNK),
        in_specs=[blk, blk, blk,
                  pl.BlockSpec((None, ML_CHUNK, 2 * ML_HEADS), lambda bi, c: (bi, c, 0)),
                  pl.BlockSpec((None, 2 * ML_HEADS, ML_CHUNK), lambda bi, c: (bi, 0, c)),
                  pl.BlockSpec((None, ML_CHUNK, D_ML), lambda bi, c: (bi, c, 1)),
                  pl.BlockSpec((1, D_ML), lambda bi, c: (0, 0))],
        out_specs=(blk,
                   pl.BlockSpec((None, ML_HEADS, ML_D, ML_D), lambda bi, c: (bi, 0, 0, 0)),
                   pl.BlockSpec((None, ML_HEADS, ML_D), lambda bi, c: (bi, 0, 0)),
                   pl.BlockSpec((None, 1, ML_HEADS), lambda bi, c: (bi, 0, 0))),
        scratch_shapes=[pltpu.VMEM((ML_HEADS, ML_D, 2 * ML_D), F32), pltpu.VMEM((ML_HEADS, 8, 128), F32)],
        compiler_params=_params(("parallel", "arbitrary")),
        name="mlstm_prompt",
    )(q, k, v, gates, gates_t, gact, norm_w)


def _page_stream(pt_ref, bi, hbm_refs, bufs, sems, *, layer, group, depth, reverse, consume, init):
    n_pages = pt_ref.shape[1]
    n_chunks = n_pages // group

    def copies(c, slot):
        out = []
        for r in range(group):
            walk = c * group + r
            page = pt_ref[bi, n_pages - 1 - walk if reverse else walk]
            for hbm, buf, sem in zip(hbm_refs, bufs, sems):
                out.append(pltpu.make_async_copy(hbm.at[layer, page], buf.at[slot, r], sem.at[slot]))
        return out

    for c in range(depth - 1):
        for cp in copies(c, c):
            cp.start()

    def body(c, carry):
        slot = lax.rem(c, depth)
        ahead = c + depth - 1

        @pl.when(ahead < n_chunks)
        def _():
            for cp in copies(ahead, lax.rem(ahead, depth)):
                cp.start()

        for cp in copies(c, slot):
            cp.wait()
        return consume(c, slot, carry)

    return lax.fori_loop(0, n_chunks, body, init)


def _sb_decode_kernel(pt_ref, q_ref, g_ref, k_hbm, v_hbm, o_ref, kbuf, vbuf, ksem, vsem, *, layer, group, depth):
    bi = pl.program_id(0)
    tiles = group // 2
    rowi = lax.broadcasted_iota(I32, (8, D_ATT), 0)
    own_head = _head_of_lane((8, D_ATT), 1) == rowi
    qm = jnp.where(own_head, jnp.broadcast_to(q_ref[...].astype(F32), (8, D_ATT)), 0.0).astype(BF16)
    suffix = _suffix_matrix(KV_TILE)

    def consume(c, slot, state):
        acc, carry = state

        def tile(buf, t):
            lo, hi = buf[slot, 2 * t + 1].reshape(D_ATT, PAGE), buf[slot, 2 * t].reshape(D_ATT, PAGE)
            return jnp.concatenate([lo, hi], axis=1).astype(BF16)

        z = jnp.concatenate([_dot(qm, tile(kbuf, t)) for t in range(tiles)], axis=0)
        sp = _softplus(z)
        lf = -sp
        hi = lf.astype(BF16).astype(F32)
        both = _dot(jnp.concatenate([hi, lf - hi], axis=0).astype(BF16), suffix)
        later = both[:8 * tiles] + both[8 * tiles:]
        tot = jnp.sum(lf, axis=-1, keepdims=True)
        carries = [carry]
        for t in range(tiles):
            carries.append(carries[-1] + tot[8 * t:8 * (t + 1)])
        w = jnp.exp((z - sp) + later + jnp.concatenate(carries[:tiles], axis=0))
        for t in range(tiles):
            acc = acc + _dot_nt(w[8 * t:8 * (t + 1)].astype(BF16), tile(vbuf, t))
        return acc, carries[tiles]

    acc, _ = _page_stream(pt_ref, bi, (k_hbm, v_hbm), (kbuf, vbuf), (ksem, vsem), layer=layer, group=group,
                          depth=depth, reverse=True, consume=consume,
                          init=(jnp.zeros((8, D_ATT), F32), jnp.zeros((8, 1), F32)))
    y = jnp.sum(jnp.where(own_head, acc, 0.0), axis=0, keepdims=True)
    o_ref[...] = (y * g_ref[...].astype(F32)).astype(BF16)


def _page_buffers(depth, group):
    return pltpu.VMEM((depth, group, N_HEADS, HEAD_DIM, PAGE), F32)


def sb_decode(page_table, q, gact, cache_k, cache_v, layer, *, group, depth):
    b = page_table.shape[0]
    row = pl.BlockSpec((None, 1, D_ATT), lambda bi, pt: (bi, 0, 0))
    hbm = pl.BlockSpec(memory_space=pl.ANY)
    grid_spec = pltpu.PrefetchScalarGridSpec(
        num_scalar_prefetch=1,
        grid=(b,),
        in_specs=[row, row, hbm, hbm],
        out_specs=row,
        scratch_shapes=[_page_buffers(depth, group), _page_buffers(depth, group),
                        pltpu.SemaphoreType.DMA((depth,)), pltpu.SemaphoreType.DMA((depth,))],
    )
    return pl.pallas_call(
        functools.partial(_sb_decode_kernel, layer=layer, group=group, depth=depth),
        out_shape=jax.ShapeDtypeStruct((b, 1, D_ATT), BF16),
        grid_spec=grid_spec,
        compiler_params=_params(("arbitrary",)),
        name="sb_decode",
    )(page_table, q, gact, cache_k, cache_v)


def _moba_gate_kernel(pt_ref, q_ref, k_hbm, idx_ref, kbuf, ksem, gs_ref, *, layer, group, depth):
    bi = pl.program_id(0)
    per_chunk = group // 2
    qc = q_ref[...]

    def consume(c, slot, carry):
        for r in range(per_chunk):
            pair = kbuf[slot, 2 * r].reshape(D_ATT, PAGE) + kbuf[slot, 2 * r + 1].reshape(D_ATT, PAGE)
            prod = pair * qc
            for h in range(N_HEADS):
                gs_ref[h, pl.ds(c * per_chunk + r, 1), :] = jnp.sum(prod[h * HEAD_DIM:(h + 1) * HEAD_DIM], axis=0,
                                                                    keepdims=True)
        return carry

    _page_stream(pt_ref, bi, (k_hbm,), (kbuf,), (ksem,), layer=layer, group=group, depth=depth, reverse=False,
                 consume=consume, init=0)
    nb = gs_ref.shape[1]
    lane = lax.broadcasted_iota(I32, (nb, 128), 1)
    gate = jnp.zeros((nb, 128), F32)
    for h in range(N_HEADS):
        gate = gate + jnp.where(lane == h, jnp.sum(gs_ref[h], axis=-1, keepdims=True) * (1.0 / KV_TILE), 0.0)
    blk = lax.broadcasted_iota(I32, (nb, 128), 0)
    rank = jnp.zeros((nb, 128), I32)
    for m in range(nb):
        gm = gate[m:m + 1, :]
        rank = rank + ((gm > gate) | ((gm == gate) & (blk > m))).astype(I32)
    blk_f = blk.astype(F32)
    rows = [jnp.sum(jnp.where(rank == t, blk_f, 0.0), axis=0, keepdims=True) for t in range(MOBA_TOPK)]
    idx_ref[...] = jnp.concatenate(rows + [jnp.zeros((8 - MOBA_TOPK, 128), F32)], axis=0).astype(I32)


def moba_decode_select(page_table, q_col, cache_k, layer, *, group, depth):
    b, n_pages = page_table.shape
    nb = n_pages * PAGE // KV_TILE
    grid_spec = pltpu.PrefetchScalarGridSpec(
        num_scalar_prefetch=1,
        grid=(b,),
        in_specs=[pl.BlockSpec((None, D_ATT, 1), lambda bi, pt: (bi, 0, 0)), pl.BlockSpec(memory_space=pl.ANY)],
        out_specs=pl.BlockSpec((None, 8, 128), lambda bi, pt: (bi, 0, 0)),
        scratch_shapes=[_page_buffers(depth, group), pltpu.SemaphoreType.DMA((depth,)),
                        pltpu.VMEM((N_HEADS, nb, PAGE), F32)],
    )
    return pl.pallas_call(
        functools.partial(_moba_gate_kernel, layer=layer, group=group, depth=depth),
        out_shape=jax.ShapeDtypeStruct((b, 8, 128), I32),
        grid_spec=grid_spec,
        compiler_params=_params(("arbitrary",)),
        name="moba_decode_select",
    )(page_table, q_col, cache_k)


def _moba_decode_kernel(pt_ref, sel_ref, q_ref, kn_ref, vn_ref, bias_ref, g_ref, *rest, last_block):
    n_sel = 2 * MOBA_TOPK
    k_refs, v_refs = rest[:n_sel], rest[n_sel:2 * n_sel]
    o_ref, = rest[2 * n_sel:]
    bi, h = pl.program_id(0), pl.program_id(1)
    qf = jnp.broadcast_to(q_ref[...].astype(F32), (8, HEAD_DIM))
    q = qf.astype(BF16)
    s_self = jnp.sum(qf * kn_ref[...].astype(BF16).astype(F32), axis=-1, keepdims=True) + bias_ref[2][:, 0:1]
    scores = []
    for t in range(MOBA_TOPK):
        kt = jnp.concatenate([k_refs[2 * t][...], k_refs[2 * t + 1][...]], axis=1).astype(BF16)
        near = sel_ref[bi, t * N_HEADS + h] == last_block
        scores.append(_dot(q, kt) + bias_ref[jnp.where(near, 1, 0)])
    m = s_self
    for s in scores:
        m = jnp.maximum(m, jnp.max(s, axis=-1, keepdims=True))
    p_self = jnp.exp(s_self - m)
    l = p_self
    acc = p_self * vn_ref[...].astype(BF16).astype(F32)
    for t, s in enumerate(scores):
        p = jnp.exp(s - m)
        l = l + jnp.sum(p, axis=-1, keepdims=True)
        vt = jnp.concatenate([v_refs[2 * t][...], v_refs[2 * t + 1][...]], axis=1).astype(BF16)
        acc = acc + _dot_nt(p.astype(BF16), vt)
    o_ref[...] = ((acc / l)[0:1] * g_ref[...].astype(F32)).astype(BF16)


def moba_decode(page_table, sel, qh, k_new, v_new, dec_bias, gact_mb, cache_k, cache_v, layer):
    b, n_pages = page_table.shape
    last_block = n_pages * PAGE // KV_TILE - 1

    def page_spec(t, half):
        return pl.BlockSpec((None, None, None, HEAD_DIM, PAGE),
                            lambda bi, h, pt, sl: (layer, pt[bi, 2 * sl[bi, t * N_HEADS + h] + half], h, 0, 0))

    vec = pl.BlockSpec((None, None, 1, HEAD_DIM), lambda bi, h, pt, sl: (bi, h, 0, 0))
    pages = [page_spec(t, half) for t in range(MOBA_TOPK) for half in range(2)]
    grid_spec = pltpu.PrefetchScalarGridSpec(
        num_scalar_prefetch=2,
        grid=(b, N_HEADS),
        in_specs=[vec, vec, vec,
                  pl.BlockSpec((None, 3, 8, KV_TILE), lambda bi, h, pt, sl: (h, 0, 0, 0)),
                  vec] + pages + pages,
        out_specs=vec,
    )
    return pl.pallas_call(
        functools.partial(_moba_decode_kernel, last_block=last_block),
        out_shape=jax.ShapeDtypeStruct((b, N_HEADS, 1, HEAD_DIM), BF16),
        grid_spec=grid_spec,
        compiler_params=_params(("parallel", "parallel")),
        name="moba_decode",
    )(page_table, sel, qh, k_new, v_new, dec_bias, gact_mb, *([cache_k] * len(pages)), *([cache_v] * len(pages)))


def _mlstm_decode_kernel(q_ref, k_ref, v_ref, gt_ref, g_ref, nw_ref, c_ref, n_ref, m_ref,
                         h_ref, co_ref, no_ref, mo_ref):
    r = lax.broadcasted_iota(I32, (ML_D, ML_D), 0)
    c = lax.broadcasted_iota(I32, (ML_D, ML_D), 1)
    eye = r == c
    for h in range(ML_HEADS):
        sl = slice(h * ML_D, (h + 1) * ML_D)
        q = q_ref[:, sl].astype(F32)
        k = k_ref[:, sl].astype(F32)
        v = v_ref[:, sl].astype(F32)
        i_pre = gt_ref[:, h:h + 1]
        logf = gt_ref[:, ML_HEADS + h:ML_HEADS + h + 1]
        m_prev = m_ref[:, h:h + 1]
        cm = c_ref[h]
        nv = n_ref[h:h + 1, :]
        g = logf + m_prev
        m_new = jnp.maximum(g, i_pre)
        w_in = jnp.exp(i_pre - m_new)
        w_st = jnp.exp(g - m_new)
        qk = jnp.sum(q * k, axis=-1, keepdims=True)
        cq = _dot_nt(jnp.broadcast_to(q, (8, ML_D)), cm, precision=HI)[0:1]
        nq = jnp.sum(nv * q, axis=-1, keepdims=True)
        num = (qk * w_in) * v + w_st * cq
        den = qk * w_in + w_st * nq
        hout = num / jnp.maximum(jnp.abs(den), jnp.exp(-m_new))
        h_ref[:, sl] = _gated_head_norm(hout, nw_ref[:, sl], g_ref[:, sl])
        v_diag = jnp.where(eye, jnp.broadcast_to(v, (ML_D, ML_D)), 0.0)
        outer = jnp.dot(v_diag, jnp.broadcast_to(k, (ML_D, ML_D)), precision=HI, preferred_element_type=F32)
        co_ref[h] = w_st * cm + w_in * outer
        no_ref[h:h + 1, :] = w_st * nv + w_in * k
        mo_ref[:, h:h + 1] = m_new


def mlstm_decode(q, k, v, gates, gact, norm_w, c0, n0, m0):
    b = q.shape[0]
    vec = pl.BlockSpec((None, 1, D_ML), lambda bi: (bi, 0, 0))
    cs = pl.BlockSpec((None, ML_HEADS, ML_D, ML_D), lambda bi: (bi, 0, 0, 0))
    ns = pl.BlockSpec((None, ML_HEADS, ML_D), lambda bi: (bi, 0, 0))
    msp = pl.BlockSpec((None, 1, ML_HEADS), lambda bi: (bi, 0, 0))
    return pl.pallas_call(
        _mlstm_decode_kernel,
        out_shape=(jax.ShapeDtypeStruct((b, 1, D_ML), BF16),
                   jax.ShapeDtypeStruct(c0.shape, F32),
                   jax.ShapeDtypeStruct(n0.shape, F32),
                   jax.ShapeDtypeStruct(m0.shape, F32)),
        grid=(b,),
        in_specs=[vec, vec, vec,
                  pl.BlockSpec((None, 1, 2 * ML_HEADS), lambda bi: (bi, 0, 0)),
                  pl.BlockSpec((None, 1, D_ML), lambda bi: (bi, 0, 1)),
                  pl.BlockSpec((1, D_ML), lambda bi: (0, 0)),
                  cs, ns, msp],
        out_specs=(vec, cs, ns, msp),
        compiler_params=_params(("parallel",)),
        name="mlstm_decode",
    )(q, k, v, gates, gact, norm_w, c0, n0, m0)


SB_DECODE_GROUP = 16
MOBA_SELECT_GROUP = 16
STREAM_DEPTH = 3
PROMPT_TM = 512


def _layer_weights(w_in_l, w_out_l):
    w_bf = w_in_l.astype(BF16)
    w_lo = (w_in_l[:, O_QMB:O_VMB] - w_bf[:, O_QMB:O_VMB].astype(F32)).astype(BF16)
    return w_bf, w_lo, w_out_l.astype(BF16)


def kernel(x_prompt, x_sample, cache_sb_k, cache_sb_v, cache_moba_k, cache_moba_v, state_mlstm_C, state_mlstm_n,
           state_mlstm_m, page_table, c_prompt, c_sample, norm_w, w_ada, b_ada, w_in, b_gates, q_norm_w, k_norm_w,
           rel_bias, ml_norm_w, w_out):
    bp, tp, d = x_prompt.shape
    bs = x_sample.shape[0]
    depth = w_in.shape[0]
    mp = bp * tp
    n_pages = page_table.shape[1]

    mod = ada_modulation(jnp.concatenate([c_prompt, c_sample], axis=0), w_ada, b_ada)
    bias_p, bias_d = bias_tiles(rel_bias)
    lanes_last = lambda a: jnp.transpose(a, (0, 1, 3, 4, 2))
    sbk, sbv, mbk, mbv = (lanes_last(a) for a in (cache_sb_k, cache_sb_v, cache_moba_k, cache_moba_v))

    xp = x_prompt.reshape(mp, d)
    xs = x_sample.reshape(bs, d)
    p_out = [[] for _ in range(7)]
    s_out = [[] for _ in range(7)]
    for l in range(depth):
        w_bf, w_lo, wo_bf = _layer_weights(w_in[l], w_out[l])
        nw = norm_w[l].reshape(1, d)
        bg = b_gates[l].reshape(1, 2 * ML_HEADS)
        qnw = jnp.tile(q_norm_w[l], N_HEADS).reshape(1, D_ATT)
        knw = jnp.tile(k_norm_w[l], N_HEADS).reshape(1, D_ATT)
        mlw = ml_norm_w[l].reshape(1, D_ML)
        shift, scale, gate = jnp.split(mod[l], 3, axis=-1)

        pm = lambda a: a[:bp].reshape(bp, 1, d)
        (qsb_t, ksb_t, ksb, vsb_t, vsb_th, qmb_t, qmb_th, kmb_t, kmb, vmb_t, vmb_th,
         qml, kml, vml, gact, gates, kmean) = \
            input_projection(xp, nw, pm(scale), pm(shift), w_bf, w_lo, bg, qnw, knw,
                             tm=PROMPT_TM, rows_per_mod=tp // PROMPT_TM, prompt=True)
        seq = lambda a: a.reshape(bp, tp, a.shape[-1])
        pages = lambda a: a.reshape(bp, tp // PAGE, D_ATT, PAGE)
        gact3 = seq(gact)
        ysb = sb_prompt(pages(qsb_t), seq(ksb), pages(vsb_th), gact3)
        ymb = moba_prompt(pages(qmb_t), pages(qmb_th), seq(kmb), pages(vmb_th),
                          kmean.reshape(bp, tp // KV_TILE, D_ATT), bias_p, gact3)
        gates3 = seq(gates)
        yml, c_p, n_p, m_p = mlstm_prompt(seq(qml), seq(kml), seq(vml), gates3, gates3.swapaxes(1, 2), gact3, mlw)
        xp = output_projection(ysb.reshape(mp, D_ATT), ymb.reshape(mp, D_ATT), yml.reshape(mp, D_ML), xp, pm(gate),
                               wo_bf, tm=PROMPT_TM, rows_per_mod=tp // PROMPT_TM)
        paged = lambda a: jnp.transpose(a.reshape(bp, tp // PAGE, N_HEADS, HEAD_DIM, PAGE), (0, 1, 4, 2, 3))
        for lst, a in zip(p_out, (paged(ksb_t), paged(vsb_t), paged(kmb_t), paged(vmb_t), c_p, n_p,
                                  m_p.reshape(bp, ML_HEADS))):
            lst.append(a)

        sm = lambda a: a[bp:].reshape(1, bs, d)
        (qsb, ksb, vsb, qmb, qmbh, kmb, vmb, qml, kml, vml, gact, gates) = \
            input_projection(xs, nw, sm(scale), sm(shift), w_bf, w_lo, bg, qnw, knw,
                             tm=bs, rows_per_mod=1, prompt=False)
        tok = lambda a: a.reshape(bs, 1, a.shape[-1])
        heads = lambda a: a.reshape(bs, N_HEADS, 1, HEAD_DIM)
        gact3 = tok(gact)
        ysb = sb_decode(page_table, tok(qsb), gact3[:, :, :D_ATT], sbk, sbv, l, group=min(SB_DECODE_GROUP, n_pages),
                        depth=STREAM_DEPTH)
        sel = moba_decode_select(page_table, qmb.reshape(bs, D_ATT, 1), mbk, l,
                                 group=min(MOBA_SELECT_GROUP, n_pages), depth=STREAM_DEPTH)
        sel = sel[:, :MOBA_TOPK, :N_HEADS].reshape(bs, MOBA_TOPK * N_HEADS)
        ymb = moba_decode(page_table, sel, heads(qmbh), heads(kmb), heads(vmb), bias_d,
                          heads(gact[:, D_ATT:2 * D_ATT]), mbk, mbv, l)
        yml, c_s, n_s, m_s = mlstm_decode(tok(qml), tok(kml), tok(vml), tok(gates), gact3, mlw,
                                          state_mlstm_C[l], state_mlstm_n[l], state_mlstm_m[l].reshape(bs, 1, ML_HEADS))
        xs = output_projection(ysb.reshape(bs, D_ATT), ymb.reshape(bs, D_ATT), yml.reshape(bs, D_ML), xs, sm(gate),
                               wo_bf, tm=bs, rows_per_mod=1)
        new = lambda a: a.reshape(bs, 1, N_HEADS, HEAD_DIM)
        for lst, a in zip(s_out, (new(ksb), new(vsb), new(kmb), new(vmb), c_s, n_s, m_s.reshape(bs, ML_HEADS))):
            lst.append(a)

    P = [jnp.stack(a) for a in p_out]
    S = [jnp.stack(a) for a in s_out]
    return (xp.reshape(bp, tp, d), xs.reshape(bs, 1, d),
            P[0], P[1], P[2], P[3], P[4], P[5], P[6], S[0], S[1], S[2], S[3], S[4], S[5], S[6])
```

```python
import functools
import math

import numpy as np
import jax
import jax.numpy as jnp
from jax import lax
from jax.experimental import pallas as pl
from jax.experimental.pallas import tpu as pltpu

F32 = jnp.float32
BF16 = jnp.bfloat16
I32 = jnp.int32
HI = lax.Precision.HIGHEST

EPS = 1e-6
HEAD_DIM = 64
N_HEADS = 4
D_ATT = N_HEADS * HEAD_DIM
ML_HEADS = 4
ML_D = 128
D_ML = ML_HEADS * ML_D
Q_BLOCK = 128
KV_TILE = 256
MOBA_TOPK = 3
N_BUCKETS = 32
MAX_DISTANCE = 128
ML_CHUNK = 128
PAGE = 128
HEAD_SHIFT = 6
NEG_INF = float("-inf")

VMEM_LIMIT = 56 * 1024 * 1024

_SIZES = [D_ATT] * 8 + [D_ML] * 5 + [ML_HEADS] * 2
_OFF = np.concatenate([[0], np.cumsum(_SIZES)]).tolist()
(O_QSB, O_KSB, O_VSB, O_GSB, O_QMB, O_KMB, O_VMB, O_GMB,
 O_QML, O_KML, O_VML, O_OML, O_GML, O_IML, O_FML, O_END) = _OFF


def _params(sem, vmem=VMEM_LIMIT):
    return pltpu.CompilerParams(dimension_semantics=sem, vmem_limit_bytes=vmem)


def _dot(a, b):
    return jnp.dot(a, b, preferred_element_type=F32)


def _dot_nt(a, b, precision=None):
    return lax.dot_general(a, b, (((1,), (1,)), ((), ())), precision=precision,
                           preferred_element_type=F32)


SOFTPLUS_CLAMP = 64.0


def _softplus(z):
    return jnp.maximum(jnp.log(1.0 + jnp.exp(jnp.minimum(z, SOFTPLUS_CLAMP))), z)


def _head_of_lane(shape, axis):
    return lax.shift_right_logical(lax.broadcasted_iota(I32, shape, axis), HEAD_SHIFT)


def _ada_kernel(c_ref, w_ref, b_ref, o_ref):
    c = c_ref[...]
    a = c * jax.nn.sigmoid(c)
    o_ref[...] = jnp.dot(a, w_ref[...], precision=HI, preferred_element_type=F32) + b_ref[...]


def ada_modulation(c_all, w_ada, b_ada):
    depth, d, d3 = w_ada.shape
    rows = c_all.shape[0]
    nt = d3 // d
    return pl.pallas_call(
        _ada_kernel,
        out_shape=jax.ShapeDtypeStruct((depth, rows, d3), F32),
        grid=(depth, nt),
        in_specs=[pl.BlockSpec((rows, d), lambda l, j: (0, 0)),
                  pl.BlockSpec((None, d, d), lambda l, j: (l, 0, j)),
                  pl.BlockSpec((None, 1, d), lambda l, j: (l, 0, j))],
        out_specs=pl.BlockSpec((None, rows, d), lambda l, j: (l, 0, j)),
        compiler_params=_params(("parallel", "parallel")),
        name="ada_modulation",
    )(c_all, w_ada, b_ada.reshape(depth, 1, d3))


def _bucket_thresholds():
    d = np.arange(0, 4 * MAX_DISTANCE, dtype=np.int32)
    max_exact = N_BUCKETS // 2
    df = np.maximum(d, 1).astype(np.float32)
    large = max_exact + (np.log(df / np.float32(max_exact)) / np.float32(math.log(MAX_DISTANCE / max_exact))
                         * np.float32(N_BUCKETS - max_exact)).astype(np.int32)
    large = np.minimum(large, N_BUCKETS - 1)
    bucket = np.where(d < max_exact, d, large)
    thr = []
    for k in range(1, N_BUCKETS):
        idx = np.nonzero(bucket >= k)[0]
        thr.append(int(idx[0]))
    assert all(np.all((bucket >= k) == (d >= t)) for k, t in zip(range(1, N_BUCKETS), thr))
    return thr


_BUCKET_THR = _bucket_thresholds()

_PATTERN_OFFSETS = (0, Q_BLOCK, KV_TILE)
P_FAR = len(_PATTERN_OFFSETS)


def _bias_from_dist(dist, rb_ref, h):
    b = jnp.full(dist.shape, rb_ref[0, h], F32)
    for k, t in zip(range(1, N_BUCKETS), _BUCKET_THR):
        b = jnp.where(dist >= t, rb_ref[k, h], b)
    return jnp.where(dist >= 0, b, NEG_INF)


def _bias_kernel(rb_ref, tile_ref, dec_ref):
    j = lax.broadcasted_iota(I32, (KV_TILE, Q_BLOCK), 0)
    i = lax.broadcasted_iota(I32, (KV_TILE, Q_BLOCK), 1)
    for p, off in enumerate(_PATTERN_OFFSETS):
        for h in range(N_HEADS):
            tile_ref[p, :, h * Q_BLOCK:(h + 1) * Q_BLOCK] = _bias_from_dist(off + i - j, rb_ref, h)
    for h in range(N_HEADS):
        tile_ref[P_FAR, :, h * Q_BLOCK:(h + 1) * Q_BLOCK] = jnp.full((KV_TILE, Q_BLOCK), rb_ref[N_BUCKETS - 1, h], F32)
    jr = lax.broadcasted_iota(I32, (8, KV_TILE), 1)
    for h in range(N_HEADS):
        dec_ref[h, 0] = jnp.full((8, KV_TILE), rb_ref[N_BUCKETS - 1, h], F32)
        dec_ref[h, 1] = _bias_from_dist(KV_TILE - jr, rb_ref, h)
        dec_ref[h, 2] = jnp.full((8, KV_TILE), rb_ref[0, h], F32)


def bias_tiles(rel_bias):
    return pl.pallas_call(
        _bias_kernel,
        out_shape=(jax.ShapeDtypeStruct((P_FAR + 1, KV_TILE, N_HEADS * Q_BLOCK), F32),
                   jax.ShapeDtypeStruct((N_HEADS, 3, 8, KV_TILE), F32)),
        in_specs=[pl.BlockSpec(memory_space=pltpu.SMEM)],
        name="bias_tiles",
    )(rel_bias)


def _inproj_kernel(x_ref, nw_ref, sc_ref, sh_ref, w_ref, wlo_ref, bg_ref, qnw_ref, knw_ref, *outs, prompt):
    outs = list(outs)

    def paged(p):
        return [p[pg * PAGE:(pg + 1) * PAGE].T for pg in range(p.shape[0] // PAGE)]

    def emit(p, *, rows_bf16=False, rows_f32=False, pages_f32=False, pages_bf16=False):
        pages = paged(p) if (pages_f32 or pages_bf16) else None
        for want, dtype, is_pages in ((pages_f32, F32, True), (rows_f32, F32, False),
                                      (pages_bf16, BF16, True), (rows_bf16, BF16, False)):
            if not want:
                continue
            ref = outs.pop(0)
            if is_pages:
                for pg, t in enumerate(pages):
                    ref[pg] = t.astype(dtype)
            else:
                ref[...] = p.astype(dtype)

    x = x_ref[...]
    ms = jnp.mean(x * x, axis=-1, keepdims=True)
    h = (x * lax.rsqrt(ms + EPS) * nw_ref[...]) * (1.0 + sc_ref[...]) + sh_ref[...]
    hb = h.astype(BF16)
    hlo = (h - hb.astype(F32)).astype(BF16)

    def proj(a, b):
        return _dot(hb, w_ref[:, a:b])

    def proj3(a, b, la, lb):
        wh = w_ref[:, a:b]
        return _dot(hb, wh) + (_dot(hlo, wh) + _dot(hb, wlo_ref[:, la:lb]))

    def silu(g):
        return g * jax.nn.sigmoid(g)

    r = lax.broadcasted_iota(I32, (D_ATT, D_ATT), 0)
    c = lax.broadcasted_iota(I32, (D_ATT, D_ATT), 1)
    same_head = lax.shift_right_logical(r, HEAD_SHIFT) == lax.shift_right_logical(c, HEAD_SHIFT)
    head_mean = jnp.where(same_head, 1.0 / HEAD_DIM, 0.0).astype(BF16)

    def head_norm(p, w):
        sq = p * p
        hi = sq.astype(BF16)
        lo = (sq - hi.astype(F32)).astype(BF16)
        ms = _dot(hi, head_mean) + _dot(lo, head_mean)
        return p * lax.rsqrt(ms + EPS) * w

    score_scale = HEAD_DIM ** -0.5
    emit(proj(O_QSB, O_KSB) * score_scale, pages_bf16=prompt, rows_bf16=not prompt)
    emit(proj(O_KSB, O_VSB), pages_f32=prompt, rows_bf16=prompt, rows_f32=not prompt)
    emit(proj(O_VSB, O_GSB), pages_f32=prompt, pages_bf16=prompt, rows_f32=not prompt)
    g_sb = silu(proj(O_GSB, O_QMB))
    qn = head_norm(proj3(O_QMB, O_KMB, 0, D_ATT), qnw_ref[...])
    if prompt:
        qt = paged(qn)
        ref_f, ref_h = outs.pop(0), outs.pop(0)
        for pg, t in enumerate(qt):
            ref_f[pg] = t
            ref_h[pg] = (t * score_scale).astype(BF16)
    else:
        emit(qn, rows_f32=True)
        emit(qn * score_scale, rows_bf16=True)
    kn = head_norm(proj3(O_KMB, O_VMB, D_ATT, 2 * D_ATT), knw_ref[...])
    emit(kn, pages_f32=prompt, rows_bf16=prompt, rows_f32=not prompt)
    emit(proj(O_VMB, O_GMB), pages_f32=prompt, pages_bf16=prompt, rows_f32=not prompt)
    g_mb = silu(proj(O_GMB, O_QML))
    emit(proj(O_QML, O_KML), rows_bf16=True)
    emit(proj(O_KML, O_VML) * (ML_D ** -0.5), rows_bf16=True)
    emit(proj(O_VML, O_OML), rows_bf16=True)
    o = proj(O_OML, O_GML)
    g_ml = jax.nn.sigmoid(o) * silu(proj(O_GML, O_IML))
    gact_ref = outs.pop(0)
    gact_ref[:, 0:D_ATT] = g_sb.astype(BF16)
    gact_ref[:, D_ATT:2 * D_ATT] = g_mb.astype(BF16)
    gact_ref[:, 2 * D_ATT:] = g_ml.astype(BF16)
    pre = proj(O_IML, O_END) + bg_ref[...]
    is_f = lax.broadcasted_iota(I32, pre.shape, 1) >= ML_HEADS
    logf = jnp.minimum(pre, 0.0) - jnp.log1p(jnp.exp(-jnp.abs(pre)))
    outs.pop(0)[...] = jnp.where(is_f, logf, pre)
    if prompt:
        kmean_ref = outs.pop(0)
        for i in range(kn.shape[0] // KV_TILE):
            kmean_ref[i] = jnp.mean(kn[i * KV_TILE:(i + 1) * KV_TILE], axis=0, keepdims=True)
    assert not outs


def input_projection(x, norm_w, scale, shift, w_bf, w_lo, b_gates, qnw, knw, *, tm, rows_per_mod, prompt):
    m, d = x.shape
    n_tiles = m // tm
    rmod = scale.shape[1]
    row = lambda i: (i, 0)
    const = lambda i: (0, 0)
    mod = lambda i: (i // rows_per_mod, 0, 0)

    def rows(width, dtype):
        return jax.ShapeDtypeStruct((m, width), dtype), pl.BlockSpec((tm, width), row)

    def pages(dtype):
        return (jax.ShapeDtypeStruct((m // PAGE, D_ATT, PAGE), dtype),
                pl.BlockSpec((tm // PAGE, D_ATT, PAGE), lambda i: (i, 0, 0)))

    if prompt:
        nb = tm // KV_TILE
        outs = [pages(BF16), pages(F32), rows(D_ATT, BF16), pages(F32), pages(BF16),
                pages(F32), pages(BF16), pages(F32), rows(D_ATT, BF16), pages(F32), pages(BF16)]
        tail = [(jax.ShapeDtypeStruct((n_tiles, nb, 1, D_ATT), F32),
                 pl.BlockSpec((None, nb, 1, D_ATT), lambda i: (i, 0, 0, 0)))]
    else:
        outs = [rows(D_ATT, BF16), rows(D_ATT, F32), rows(D_ATT, F32),
                rows(D_ATT, F32), rows(D_ATT, BF16), rows(D_ATT, F32), rows(D_ATT, F32)]
        tail = []
    outs += [rows(D_ML, BF16), rows(D_ML, BF16), rows(D_ML, BF16),
             rows(2 * D_ATT + D_ML, BF16), rows(2 * ML_HEADS, F32)] + tail
    return pl.pallas_call(
        functools.partial(_inproj_kernel, prompt=prompt),
        out_shape=tuple(o[0] for o in outs),
        grid=(n_tiles,),
        in_specs=[pl.BlockSpec((tm, d), row),
                  pl.BlockSpec((1, d), const),
                  pl.BlockSpec((None, rmod, d), mod),
                  pl.BlockSpec((None, rmod, d), mod),
                  pl.BlockSpec(w_bf.shape, const),
                  pl.BlockSpec(w_lo.shape, const),
                  pl.BlockSpec((1, 2 * ML_HEADS), const),
                  pl.BlockSpec((1, D_ATT), const),
                  pl.BlockSpec((1, D_ATT), const)],
        out_specs=tuple(o[1] for o in outs),
        compiler_params=_params(("parallel",)),
        name="input_projection",
    )(x, norm_w, scale, shift, w_bf, w_lo, b_gates, qnw, knw)


def _outproj_kernel(ysb_ref, ymb_ref, yml_ref, x_ref, gate_ref, w_ref, o_ref):
    y = (_dot(ysb_ref[...], w_ref[0:D_ATT, :]) + _dot(ymb_ref[...], w_ref[D_ATT:2 * D_ATT, :])
         + _dot(yml_ref[...], w_ref[2 * D_ATT:, :]))
    o_ref[...] = x_ref[...] + gate_ref[...] * y


def output_projection(ysb, ymb, yml, x, gate, w_bf, *, tm, rows_per_mod):
    m, d = x.shape
    rmod = gate.shape[1]
    row = lambda i: (i, 0)
    return pl.pallas_call(
        _outproj_kernel,
        out_shape=jax.ShapeDtypeStruct((m, d), F32),
        grid=(m // tm,),
        in_specs=[pl.BlockSpec((tm, D_ATT), row), pl.BlockSpec((tm, D_ATT), row), pl.BlockSpec((tm, D_ML), row),
                  pl.BlockSpec((tm, d), row),
                  pl.BlockSpec((None, rmod, d), lambda i: (i // rows_per_mod, 0, 0)),
                  pl.BlockSpec(w_bf.shape, lambda i: (0, 0))],
        out_specs=pl.BlockSpec((tm, d), row),
        compiler_params=_params(("parallel",)),
        name="output_projection",
    )(ysb, ymb, yml, x, gate, w_bf)


def _suffix_matrix(n, transposed=False):
    r = lax.broadcasted_iota(I32, (n, n), 0)
    c = lax.broadcasted_iota(I32, (n, n), 1)
    return jnp.where((c > r) if transposed else (r > c), 1.0, 0.0).astype(BF16)


def _heads_on_lanes(qt):
    head = _head_of_lane((D_ATT, 1), 0)
    return jnp.concatenate([jnp.where(head == h, qt, jnp.zeros_like(qt)) for h in range(N_HEADS)], axis=1)


def _heads_from_lanes(acc_t, q):
    head = _head_of_lane((D_ATT, 1), 0)
    out = jnp.zeros((D_ATT, q), F32)
    for h in range(N_HEADS):
        out = out + jnp.where(head == h, acc_t[:, h * q:(h + 1) * q], 0.0)
    return out.T


def _kv_tile(k_ref, vt_ref, n):
    start = pl.multiple_of(n * KV_TILE, KV_TILE)
    vt = jnp.concatenate([vt_ref[2 * n], vt_ref[2 * n + 1]], axis=1)
    return k_ref[pl.ds(start, KV_TILE), :], vt


def _sb_tile_t(q4t, k, vt, upper, carry, mask):
    z = _dot(k, q4t)
    sp = _softplus(z)
    if mask is not None:
        sp_sum = jnp.where(mask, sp, 0.0)
    else:
        sp_sum = sp
    later = _dot(upper, sp_sum.astype(BF16))
    w = jnp.exp(((z - sp) - later) - carry)
    if mask is not None:
        w = jnp.where(mask, w, 0.0)
    return _dot(vt, w.astype(BF16)), carry + jnp.sum(sp_sum, axis=0, keepdims=True)


def _sb_prompt_kernel(qt_ref, k_ref, vt_ref, g_ref, o_ref, acc_ref, carry_ref):
    i = pl.program_id(1)
    cols = N_HEADS * Q_BLOCK
    q4t = _heads_on_lanes(qt_ref[...])
    upper = _suffix_matrix(KV_TILE, transposed=True)
    last = (i * Q_BLOCK) // KV_TILE
    kpos = last * KV_TILE + lax.broadcasted_iota(I32, (KV_TILE, cols), 0)
    qpos = i * Q_BLOCK + (lax.broadcasted_iota(I32, (KV_TILE, cols), 1) & (Q_BLOCK - 1))
    k, vt = _kv_tile(k_ref, vt_ref, last)
    pv, carry = _sb_tile_t(q4t, k, vt, upper, jnp.zeros((1, cols), F32), kpos < qpos)
    acc_ref[...] = pv
    carry_ref[...] = carry

    @pl.when(last % 2 == 1)
    def _():
        k, vt = _kv_tile(k_ref, vt_ref, last - 1)
        pv, carry = _sb_tile_t(q4t, k, vt, upper, carry_ref[...], None)
        acc_ref[...] += pv
        carry_ref[...] = carry

    first = last - 1 - last % 2

    def body(s, carry):
        ka, vta = _kv_tile(k_ref, vt_ref, first - 2 * s)
        kb, vtb = _kv_tile(k_ref, vt_ref, first - 2 * s - 1)
        za, zb = _dot(ka, q4t), _dot(kb, q4t)
        spa, spb = _softplus(za), _softplus(zb)
        la, lb = _dot(upper, spa.astype(BF16)), _dot(upper, spb.astype(BF16))
        ca = carry + jnp.sum(spa, axis=0, keepdims=True)
        wa = jnp.exp(((za - spa) - la) - carry)
        wb = jnp.exp(((zb - spb) - lb) - ca)
        acc_ref[...] += _dot(vta, wa.astype(BF16)) + _dot(vtb, wb.astype(BF16))
        return ca + jnp.sum(spb, axis=0, keepdims=True)

    lax.fori_loop(0, last // 2, body, carry_ref[...])
    o_ref[...] = (_heads_from_lanes(acc_ref[...], Q_BLOCK) * g_ref[...].astype(F32)).astype(BF16)


def sb_prompt(qt, k, vt, gact):
    b, t, _ = k.shape
    n_pages = t // PAGE
    blk = pl.BlockSpec((None, Q_BLOCK, D_ATT), lambda bi, i: (bi, i, 0))
    return pl.pallas_call(
        _sb_prompt_kernel,
        out_shape=jax.ShapeDtypeStruct((b, t, D_ATT), BF16),
        grid=(b, t // Q_BLOCK),
        in_specs=[pl.BlockSpec((None, None, D_ATT, PAGE), lambda bi, i: (bi, i, 0, 0)),
                  pl.BlockSpec((None, t, D_ATT), lambda bi, i: (bi, 0, 0)),
                  pl.BlockSpec((None, n_pages, D_ATT, PAGE), lambda bi, i: (bi, 0, 0, 0)),
                  blk],
        out_specs=blk,
        scratch_shapes=[pltpu.VMEM((D_ATT, N_HEADS * Q_BLOCK), F32), pltpu.VMEM((1, N_HEADS * Q_BLOCK), F32)],
        compiler_params=_params(("parallel", "arbitrary")),
        name="sb_prompt",
    )(qt, k, vt, gact)


def _moba_select_t(gate, valid, idx):
    n = gate.shape[0]
    gate = jnp.where(valid, gate, NEG_INF)
    rank = jnp.zeros(gate.shape, I32)
    for m in range(n):
        gm = gate[m:m + 1, :]
        beats = (gm > gate) | ((gm == gate) & (idx > m))
        rank = rank + beats.astype(I32)
    return valid & (rank < MOBA_TOPK)


def _moba_prompt_kernel(qft_ref, qht_ref, k_ref, vt_ref, km_ref, bias_ref, g_ref, o_ref, acc_ref, m_ref, l_ref, sel_ref):
    i = pl.program_id(1)
    nb = km_ref.shape[0]
    own = (i * Q_BLOCK) // KV_TILE
    odd = (i * Q_BLOCK) % KV_TILE != 0
    q4t = _heads_on_lanes(qht_ref[...])
    gate = jnp.dot(km_ref[...], _heads_on_lanes(qft_ref[...]), precision=HI, preferred_element_type=F32)
    idx = lax.broadcasted_iota(I32, (nb, 1), 0)
    sel = _moba_select_t(gate, idx < own, idx)
    sel_ref[...] = jnp.where(sel, 0.0, NEG_INF)

    def attend(blocks, first=False):
        tiles = [_kv_tile(k_ref, vt_ref, n) for n, _ in blocks]
        scores = []
        for (k, _), (_, terms) in zip(tiles, blocks):
            s = _dot(k, q4t)
            for t in terms:
                s = s + t
            scores.append(s)
        m_new = functools.reduce(jnp.maximum, [jnp.max(s, axis=0, keepdims=True) for s in scores])
        if not first:
            m_old = m_ref[...]
            m_new = jnp.maximum(m_old, m_new)
            alpha = jnp.exp(m_old - m_new)
        ps = [jnp.exp(s - m_new) for s in scores]
        l_new = functools.reduce(jnp.add, [jnp.sum(p, axis=0, keepdims=True) for p in ps])
        pv = functools.reduce(jnp.add, [_dot(vt, p.astype(BF16)) for (_, vt), p in zip(tiles, ps)])
        m_ref[...] = m_new
        l_ref[...] = l_new if first else alpha * l_ref[...] + l_new
        acc_ref[...] = pv if first else alpha * acc_ref[...] + pv

    attend([(own, [bias_ref[jnp.where(odd, 1, 0)]])], first=True)
    near_pat = jnp.where(odd, P_FAR, 2)
    far_row = bias_ref[P_FAR, 0:1, :]
    for n in range(0, nb - 2, 2):
        @pl.when(n + 1 < own)
        def _():
            attend([(n, [far_row + sel_ref[n:n + 1, :]]),
                    (n + 1, [bias_ref[jnp.where(n + 1 == own - 1, near_pat, P_FAR)], sel_ref[n + 1:n + 2, :]])])

    @pl.when(own % 2 == 1)
    def _():
        attend([(own - 1, [bias_ref[near_pat], sel_ref[pl.ds(own - 1, 1), :]])])

    y = _heads_from_lanes(acc_ref[...] / l_ref[...], Q_BLOCK)
    o_ref[...] = (y * g_ref[...].astype(F32)).astype(BF16)


def moba_prompt(qft, qht, k, vt, kmean, bias, gact):
    b, t, _ = k.shape
    nb = t // KV_TILE
    cols = N_HEADS * Q_BLOCK
    blk = pl.BlockSpec((None, Q_BLOCK, D_ATT), lambda bi, i: (bi, i, 0))
    qpage = pl.BlockSpec((None, None, D_ATT, PAGE), lambda bi, i: (bi, i, 0, 0))
    return pl.pallas_call(
        _moba_prompt_kernel,
        out_shape=jax.ShapeDtypeStruct((b, t, D_ATT), BF16),
        grid=(b, t // Q_BLOCK),
        in_specs=[qpage, qpage,
                  pl.BlockSpec((None, t, D_ATT), lambda bi, i: (bi, 0, 0)),
                  pl.BlockSpec((None, t // PAGE, D_ATT, PAGE), lambda bi, i: (bi, 0, 0, 0)),
                  pl.BlockSpec((None, nb, D_ATT), lambda bi, i: (bi, 0, 0)),
                  pl.BlockSpec(bias.shape, lambda bi, i: (0, 0, 0)),
                  pl.BlockSpec((None, Q_BLOCK, D_ATT), lambda bi, i: (bi, i, 1))],
        out_specs=blk,
        scratch_shapes=[pltpu.VMEM((D_ATT, cols), F32), pltpu.VMEM((1, cols), F32), pltpu.VMEM((1, cols), F32),
                        pltpu.VMEM((nb, cols), F32)],
        compiler_params=_params(("parallel", "arbitrary")),
        name="moba_prompt",
    )(qft, qht, k, vt, kmean, bias, gact)


def _gated_head_norm(h, w, g):
    hn = h * lax.rsqrt(jnp.mean(h * h, axis=-1, keepdims=True) + EPS) * w
    return (hn * g.astype(F32)).astype(BF16)


def _mlstm_prompt_kernel(q_ref, k_ref, v_ref, gc_ref, gr_ref, g_ref, nw_ref, h_ref, c_ref, n_ref, m_ref,
                         st_ref, ms_ref):
    ci = pl.program_id(1)
    L = ML_CHUNK

    @pl.when(ci == 0)
    def _():
        st_ref[...] = jnp.zeros_like(st_ref)
        ms_ref[...] = jnp.zeros_like(ms_ref)

    t = lax.broadcasted_iota(I32, (L, L), 0)
    s = lax.broadcasted_iota(I32, (L, L), 1)
    causal = s <= t
    lower = jnp.where(causal, 1.0, 0.0).astype(F32)
    upper = jnp.where(t <= s, 1.0, 0.0).astype(F32)
    gc = gc_ref[...]
    gr = gr_ref[...]
    bh_c = jnp.dot(lower, gc, precision=HI, preferred_element_type=F32)
    bh_r = jnp.dot(gr, upper, precision=HI, preferred_element_type=F32)
    ones_col = jnp.where(lax.broadcasted_iota(I32, (L, ML_D), 1) == 0, 1.0, 0.0).astype(BF16)

    for h in range(ML_HEADS):
        sl = slice(h * ML_D, (h + 1) * ML_D)
        q, k, v = q_ref[:, sl], k_ref[:, sl], v_ref[:, sl]
        vext = jnp.concatenate([v, ones_col], axis=1)
        m_prev = ms_ref[h][0:1, 0:1]
        b_c = bh_c[:, ML_HEADS + h:ML_HEADS + h + 1]
        a_c = gc[:, h:h + 1] - b_c
        a_r = gr[h:h + 1, :] - bh_r[ML_HEADS + h:ML_HEADS + h + 1, :]
        dm = jnp.where(causal, b_c + a_r, NEG_INF)
        g = b_c + m_prev
        m_row = jnp.maximum(g, jnp.max(dm, axis=-1, keepdims=True))
        w_intra = jnp.exp(dm - m_row)
        w_inter = jnp.exp(g - m_row)
        sc = _dot_nt(q, k) * w_intra
        st = st_ref[h]
        numden = _dot(sc.astype(BF16), vext) + w_inter * _dot(q, st.astype(BF16))
        den = numden[:, ML_D:ML_D + 1]
        hout = numden[:, :ML_D] / jnp.maximum(jnp.abs(den), jnp.exp(-m_row))
        h_ref[:, sl] = _gated_head_norm(hout, nw_ref[:, sl], g_ref[:, sl])

        b_last = b_c[L - 1:L, :]
        m_new = jnp.maximum(b_last + m_prev, jnp.max(b_last + a_r, axis=-1, keepdims=True))
        ws = jnp.exp(b_last + a_c - m_new)
        wc = jnp.exp(b_last + m_prev - m_new)
        kt = k.astype(F32).T.astype(BF16)
        st_new = wc * st + _dot(kt, (ws * vext.astype(F32)).astype(BF16))
        st_ref[h] = st_new
        ms_ref[h] = jnp.broadcast_to(m_new, ms_ref.shape[1:])

    @pl.when(ci == pl.num_programs(1) - 1)
    def _():
        for h in range(ML_HEADS):
            stt = st_ref[h].T
            c_ref[h] = stt[:ML_D]
            n_ref[h:h + 1, :] = stt[ML_D:ML_D + 1]
            m_ref[:, h:h + 1] = ms_ref[h][0:1, 0:1]


def mlstm_prompt(q, k, v, gates, gates_t, gact, norm_w):
    b, t, _ = q.shape
    blk = pl.BlockSpec((None, ML_CHUNK, D_ML), lambda bi, c: (bi, c, 0))
    return pl.pallas_call(
        _mlstm_prompt_kernel,
        out_shape=(jax.ShapeDtypeStruct((b, t, D_ML), BF16),
                   jax.ShapeDtypeStruct((b, ML_HEADS, ML_D, ML_D), F32),
                   jax.ShapeDtypeStruct((b, ML_HEADS, ML_D), F32),
                   jax.ShapeDtypeStruct((b, 1, ML_HEADS), F32)),
        grid=(b, t // ML_CHUNK),
        in_specs=[blk, blk, blk,
                  pl.BlockSpec((None, ML_CHUNK, 2 * ML_HEADS), lambda bi, c: (bi, c, 0)),
                  pl.BlockSpec((None, 2 * ML_HEADS, ML_CHUNK), lambda bi, c: (bi, 0, c)),
                  pl.BlockSpec((None, ML_CHUNK, D_ML), lambda bi, c: (bi, c, 1)),
                  pl.BlockSpec((1, D_ML), lambda bi, c: (0, 0))],
        out_specs=(blk,
                   pl.BlockSpec((None, ML_HEADS, ML_D, ML_D), lambda bi, c: (bi, 0, 0, 0)),
                   pl.BlockSpec((None, ML_HEADS, ML_D), lambda bi, c: (bi, 0, 0)),
                   pl.BlockSpec((None, 1, ML_HEADS), lambda bi, c: (bi, 0, 0))),
        scratch_shapes=[pltpu.VMEM((ML_HEADS, ML_D, 2 * ML_D), F32), pltpu.VMEM((ML_HEADS, 8, 128), F32)],
        compiler_params=_params(("parallel", "arbitrary")),
        name="mlstm_prompt",
    )(q, k, v, gates, gates_t, gact, norm_w)


def _page_stream(pt_ref, hbm_refs, bufs, sems, *, layer, group, depth, reverse, consume, init):
    n_seq, n_pages = pt_ref.shape
    per_seq = n_pages // group
    total = n_seq * per_seq

    def copies(g, slot):
        seq, c = g // per_seq, g % per_seq
        out = []
        for r in range(group):
            walk = c * group + r
            page = pt_ref[seq, n_pages - 1 - walk if reverse else walk]
            for hbm, buf, sem in zip(hbm_refs, bufs, sems):
                out.append(pltpu.make_async_copy(hbm.at[layer, page], buf.at[slot, r], sem.at[slot]))
        return out

    for g in range(depth - 1):
        for cp in copies(g, g):
            cp.start()

    def body(g, carry):
        slot = lax.rem(g, depth)
        ahead = g + depth - 1

        @pl.when(ahead < total)
        def _():
            for cp in copies(ahead, lax.rem(ahead, depth)):
                cp.start()

        for cp in copies(g, slot):
            cp.wait()
        return consume(g // per_seq, g % per_seq, slot, carry)

    return lax.fori_loop(0, total, body, init)


def _sb_decode_kernel(pt_ref, q_ref, g_ref, k_hbm, v_hbm, o_ref, kbuf, vbuf, ksem, vsem, *, layer, group, depth):
    tiles = group // 2
    last_chunk = pt_ref.shape[1] // group - 1
    rowi = lax.broadcasted_iota(I32, (8, D_ATT), 0)
    own_head = _head_of_lane((8, D_ATT), 1) == rowi
    suffix = _suffix_matrix(KV_TILE)

    def consume(seq, c, slot, state):
        fresh = c == 0
        acc = jnp.where(fresh, 0.0, state[0])
        carry = jnp.where(fresh, 0.0, state[1])
        qm = jnp.where(own_head, jnp.broadcast_to(q_ref[seq].astype(F32), (8, D_ATT)), 0.0).astype(BF16)

        def tile(buf, t):
            lo, hi = buf[slot, 2 * t + 1].reshape(D_ATT, PAGE), buf[slot, 2 * t].reshape(D_ATT, PAGE)
            return jnp.concatenate([lo, hi], axis=1).astype(BF16)

        z = jnp.concatenate([_dot(qm, tile(kbuf, t)) for t in range(tiles)], axis=0)
        sp = _softplus(z)
        lf = -sp
        hi = lf.astype(BF16).astype(F32)
        both = _dot(jnp.concatenate([hi, lf - hi], axis=0).astype(BF16), suffix)
        later = both[:8 * tiles] + both[8 * tiles:]
        tot = jnp.sum(lf, axis=-1, keepdims=True)
        carries = [carry]
        for t in range(tiles):
            carries.append(carries[-1] + tot[8 * t:8 * (t + 1)])
        w = jnp.exp((z - sp) + later + jnp.concatenate(carries[:tiles], axis=0))
        for t in range(tiles):
            acc = acc + _dot_nt(w[8 * t:8 * (t + 1)].astype(BF16), tile(vbuf, t))

        @pl.when(c == last_chunk)
        def _():
            y = jnp.sum(jnp.where(own_head, acc, 0.0), axis=0, keepdims=True)
            o_ref[seq] = (y * g_ref[seq].astype(F32)).astype(BF16)

        return acc, carries[tiles]

    _page_stream(pt_ref, (k_hbm, v_hbm), (kbuf, vbuf), (ksem, vsem), layer=layer, group=group, depth=depth,
                 reverse=True, consume=consume, init=(jnp.zeros((8, D_ATT), F32), jnp.zeros((8, 1), F32)))


def _page_buffers(depth, group):
    return pltpu.VMEM((depth, group, N_HEADS, HEAD_DIM, PAGE), F32)


def sb_decode(page_table, q, gact, cache_k, cache_v, layer, *, group, depth):
    b = page_table.shape[0]
    rows = pl.BlockSpec((b, 1, D_ATT), lambda i, pt: (0, 0, 0))
    hbm = pl.BlockSpec(memory_space=pl.ANY)
    grid_spec = pltpu.PrefetchScalarGridSpec(
        num_scalar_prefetch=1,
        grid=(1,),
        in_specs=[rows, rows, hbm, hbm],
        out_specs=rows,
        scratch_shapes=[_page_buffers(depth, group), _page_buffers(depth, group),
                        pltpu.SemaphoreType.DMA((depth,)), pltpu.SemaphoreType.DMA((depth,))],
    )
    return pl.pallas_call(
        functools.partial(_sb_decode_kernel, layer=layer, group=group, depth=depth),
        out_shape=jax.ShapeDtypeStruct((b, 1, D_ATT), BF16),
        grid_spec=grid_spec,
        compiler_params=_params(("arbitrary",)),
        name="sb_decode",
    )(page_table, q, gact, cache_k, cache_v)


def _moba_gate_kernel(pt_ref, q_ref, k_hbm, idx_ref, kbuf, ksem, gs_ref, *, layer, group, depth):
    per_chunk = group // 2
    last_chunk = pt_ref.shape[1] // group - 1

    def consume(seq, c, slot, carry):
        qc = q_ref[seq]
        for r in range(per_chunk):
            pair = kbuf[slot, 2 * r].reshape(D_ATT, PAGE) + kbuf[slot, 2 * r + 1].reshape(D_ATT, PAGE)
            prod = pair * qc
            for h in range(N_HEADS):
                gs_ref[h, pl.ds(c * per_chunk + r, 1), :] = jnp.sum(prod[h * HEAD_DIM:(h + 1) * HEAD_DIM], axis=0,
                                                                    keepdims=True)

        @pl.when(c == last_chunk)
        def _():
            nb = gs_ref.shape[1]
            lane = lax.broadcasted_iota(I32, (nb, 128), 1)
            gate = jnp.zeros((nb, 128), F32)
            for h in range(N_HEADS):
                gate = gate + jnp.where(lane == h, jnp.sum(gs_ref[h], axis=-1, keepdims=True) * (1.0 / KV_TILE), 0.0)
            blk = lax.broadcasted_iota(I32, (nb, 128), 0)
            rank = jnp.zeros((nb, 128), I32)
            for m in range(nb):
                gm = gate[m:m + 1, :]
                rank = rank + ((gm > gate) | ((gm == gate) & (blk > m))).astype(I32)
            blk_f = blk.astype(F32)
            rows = [jnp.sum(jnp.where(rank == t, blk_f, 0.0), axis=0, keepdims=True) for t in range(MOBA_TOPK)]
            idx_ref[seq] = jnp.concatenate(rows + [jnp.zeros((8 - MOBA_TOPK, 128), F32)], axis=0).astype(I32)

        return carry

    _page_stream(pt_ref, (k_hbm,), (kbuf,), (ksem,), layer=layer, group=group, depth=depth, reverse=False,
                 consume=consume, init=0)


def moba_decode_select(page_table, q_col, cache_k, layer, *, group, depth):
    b, n_pages = page_table.shape
    nb = n_pages * PAGE // KV_TILE
    grid_spec = pltpu.PrefetchScalarGridSpec(
        num_scalar_prefetch=1,
        grid=(1,),
        in_specs=[pl.BlockSpec((b, D_ATT, 1), lambda i, pt: (0, 0, 0)), pl.BlockSpec(memory_space=pl.ANY)],
        out_specs=pl.BlockSpec((b, 8, 128), lambda i, pt: (0, 0, 0)),
        scratch_shapes=[_page_buffers(depth, group), pltpu.SemaphoreType.DMA((depth,)),
                        pltpu.VMEM((N_HEADS, nb, PAGE), F32)],
    )
    return pl.pallas_call(
        functools.partial(_moba_gate_kernel, layer=layer, group=group, depth=depth),
        out_shape=jax.ShapeDtypeStruct((b, 8, 128), I32),
        grid_spec=grid_spec,
        compiler_params=_params(("arbitrary",)),
        name="moba_decode_select",
    )(page_table, q_col, cache_k)


def _moba_decode_kernel(pt_ref, sel_ref, q_ref, kn_ref, vn_ref, bias_ref, g_ref, k_hbm, v_hbm, o_ref,
                        kbuf, vbuf, ksem, vsem, *, layer, last_block):
    n_seq = pt_ref.shape[0]

    def piece(t, h, half):
        return (t * N_HEADS + h) * 2 + half

    def copies(seq, slot):
        out = []
        for t in range(MOBA_TOPK):
            for h in range(N_HEADS):
                blk = sel_ref[seq, t * N_HEADS + h]
                for half in range(2):
                    page = pt_ref[seq, 2 * blk + half]
                    i = piece(t, h, half)
                    out.append(pltpu.make_async_copy(k_hbm.at[layer, page, h], kbuf.at[slot, i], ksem.at[slot]))
                    out.append(pltpu.make_async_copy(v_hbm.at[layer, page, h], vbuf.at[slot, i], vsem.at[slot]))
        return out

    for cp in copies(0, 0):
        cp.start()

    def body(seq, carry):
        slot = lax.rem(seq, 2)

        @pl.when(seq + 1 < n_seq)
        def _():
            for cp in copies(seq + 1, 1 - slot):
                cp.start()

        for cp in copies(seq, slot):
            cp.wait()
        for h in range(N_HEADS):
            qf = jnp.broadcast_to(q_ref[seq, h].astype(F32), (8, HEAD_DIM))
            q = qf.astype(BF16)
            s_self = (jnp.sum(qf * kn_ref[seq, h].astype(BF16).astype(F32), axis=-1, keepdims=True)
                      + bias_ref[h, 2][:, 0:1])
            scores = []
            for t in range(MOBA_TOPK):
                kt = jnp.concatenate([kbuf[slot, piece(t, h, 0)], kbuf[slot, piece(t, h, 1)]], axis=1).astype(BF16)
                near = sel_ref[seq, t * N_HEADS + h] == last_block
                scores.append(_dot(q, kt) + bias_ref[h, jnp.where(near, 1, 0)])
            m = s_self
            for s in scores:
                m = jnp.maximum(m, jnp.max(s, axis=-1, keepdims=True))
            p_self = jnp.exp(s_self - m)
            l = p_self
            acc = p_self * vn_ref[seq, h].astype(BF16).astype(F32)
            for t, s in enumerate(scores):
                p = jnp.exp(s - m)
                l = l + jnp.sum(p, axis=-1, keepdims=True)
                vt = jnp.concatenate([vbuf[slot, piece(t, h, 0)], vbuf[slot, piece(t, h, 1)]], axis=1).astype(BF16)
                acc = acc + _dot_nt(p.astype(BF16), vt)
            o_ref[seq, h] = ((acc / l)[0:1] * g_ref[seq, h].astype(F32)).astype(BF16)
        return carry

    lax.fori_loop(0, n_seq, body, 0)


def moba_decode(page_table, sel, qh, k_new, v_new, dec_bias, gact_mb, cache_k, cache_v, layer):
    b, n_pages = page_table.shape
    last_block = n_pages * PAGE // KV_TILE - 1

    pieces = 2 * MOBA_TOPK * N_HEADS
    vec = pl.BlockSpec((b, N_HEADS, 1, HEAD_DIM), lambda i, pt, sl: (0, 0, 0, 0))
    hbm = pl.BlockSpec(memory_space=pl.ANY)
    grid_spec = pltpu.PrefetchScalarGridSpec(
        num_scalar_prefetch=2,
        grid=(1,),
        in_specs=[vec, vec, vec, pl.BlockSpec(dec_bias.shape, lambda i, pt, sl: (0, 0, 0, 0)), vec, hbm, hbm],
        out_specs=vec,
        scratch_shapes=[pltpu.VMEM((2, pieces, HEAD_DIM, PAGE), F32), pltpu.VMEM((2, pieces, HEAD_DIM, PAGE), F32),
                        pltpu.SemaphoreType.DMA((2,)), pltpu.SemaphoreType.DMA((2,))],
    )
    return pl.pallas_call(
        functools.partial(_moba_decode_kernel, layer=layer, last_block=last_block),
        out_shape=jax.ShapeDtypeStruct((b, N_HEADS, 1, HEAD_DIM), BF16),
        grid_spec=grid_spec,
        compiler_params=_params(("arbitrary",)),
        name="moba_decode",
    )(page_table, sel, qh, k_new, v_new, dec_bias, gact_mb, cache_k, cache_v)


def _mlstm_decode_kernel(q_ref, k_ref, v_ref, gt_ref, g_ref, nw_ref, c_ref, n_ref, m_ref,
                         h_ref, co_ref, no_ref, mo_ref):
    r = lax.broadcasted_iota(I32, (ML_D, ML_D), 0)
    c = lax.broadcasted_iota(I32, (ML_D, ML_D), 1)
    eye = r == c
    for h in range(ML_HEADS):
        sl = slice(h * ML_D, (h + 1) * ML_D)
        q = q_ref[:, sl].astype(F32)
        k = k_ref[:, sl].astype(F32)
        v = v_ref[:, sl].astype(F32)
        i_pre = gt_ref[:, h:h + 1]
        logf = gt_ref[:, ML_HEADS + h:ML_HEADS + h + 1]
        m_prev = m_ref[:, h:h + 1]
        cm = c_ref[h]
        nv = n_ref[h:h + 1, :]
        g = logf + m_prev
        m_new = jnp.maximum(g, i_pre)
        w_in = jnp.exp(i_pre - m_new)
        w_st = jnp.exp(g - m_new)
        qk = jnp.sum(q * k, axis=-1, keepdims=True)
        cq = _dot_nt(jnp.broadcast_to(q, (8, ML_D)), cm, precision=HI)[0:1]
        nq = jnp.sum(nv * q, axis=-1, keepdims=True)
        num = (qk * w_in) * v + w_st * cq
        den = qk * w_in + w_st * nq
        hout = num / jnp.maximum(jnp.abs(den), jnp.exp(-m_new))
        h_ref[:, sl] = _gated_head_norm(hout, nw_ref[:, sl], g_ref[:, sl])
        v_diag = jnp.where(eye, jnp.broadcast_to(v, (ML_D, ML_D)), 0.0)
        outer = jnp.dot(v_diag, jnp.broadcast_to(k, (ML_D, ML_D)), precision=HI, preferred_element_type=F32)
        co_ref[h] = w_st * cm + w_in * outer
        no_ref[h:h + 1, :] = w_st * nv + w_in * k
        mo_ref[:, h:h + 1] = m_new


def mlstm_decode(q, k, v, gates, gact, norm_w, c0, n0, m0):
    b = q.shape[0]
    vec = pl.BlockSpec((None, 1, D_ML), lambda bi: (bi, 0, 0))
    cs = pl.BlockSpec((None, ML_HEADS, ML_D, ML_D), lambda bi: (bi, 0, 0, 0))
    ns = pl.BlockSpec((None, ML_HEADS, ML_D), lambda bi: (bi, 0, 0))
    msp = pl.BlockSpec((None, 1, ML_HEADS), lambda bi: (bi, 0, 0))
    return pl.pallas_call(
        _mlstm_decode_kernel,
        out_shape=(jax.ShapeDtypeStruct((b, 1, D_ML), BF16),
                   jax.ShapeDtypeStruct(c0.shape, F32),
                   jax.ShapeDtypeStruct(n0.shape, F32),
                   jax.ShapeDtypeStruct(m0.shape, F32)),
        grid=(b,),
        in_specs=[vec, vec, vec,
                  pl.BlockSpec((None, 1, 2 * ML_HEADS), lambda bi: (bi, 0, 0)),
                  pl.BlockSpec((None, 1, D_ML), lambda bi: (bi, 0, 1)),
                  pl.BlockSpec((1, D_ML), lambda bi: (0, 0)),
                  cs, ns, msp],
        out_specs=(vec, cs, ns, msp),
        compiler_params=_params(("parallel",)),
        name="mlstm_decode",
    )(q, k, v, gates, gact, norm_w, c0, n0, m0)


SB_DECODE_GROUP = 16
MOBA_SELECT_GROUP = 16
STREAM_DEPTH = 3
PROMPT_TM = 512


def _layer_weights(w_in_l, w_out_l):
    w_bf = w_in_l.astype(BF16)
    w_lo = (w_in_l[:, O_QMB:O_VMB] - w_bf[:, O_QMB:O_VMB].astype(F32)).astype(BF16)
    return w_bf, w_lo, w_out_l.astype(BF16)


def kernel(x_prompt, x_sample, cache_sb_k, cache_sb_v, cache_moba_k, cache_moba_v, state_mlstm_C, state_mlstm_n,
           state_mlstm_m, page_table, c_prompt, c_sample, norm_w, w_ada, b_ada, w_in, b_gates, q_norm_w, k_norm_w,
           rel_bias, ml_norm_w, w_out):
    bp, tp, d = x_prompt.shape
    bs = x_sample.shape[0]
    depth = w_in.shape[0]
    mp = bp * tp
    n_pages = page_table.shape[1]

    mod = ada_modulation(jnp.concatenate([c_prompt, c_sample], axis=0), w_ada, b_ada)
    bias_p, bias_d = bias_tiles(rel_bias)
    lanes_last = lambda a: jnp.transpose(a, (0, 1, 3, 4, 2))
    sbk, sbv, mbk, mbv = (lanes_last(a) for a in (cache_sb_k, cache_sb_v, cache_moba_k, cache_moba_v))

    xp = x_prompt.reshape(mp, d)
    xs = x_sample.reshape(bs, d)
    p_out = [[] for _ in range(7)]
    s_out = [[] for _ in range(7)]
    for l in range(depth):
        w_bf, w_lo, wo_bf = _layer_weights(w_in[l], w_out[l])
        nw = norm_w[l].reshape(1, d)
        bg = b_gates[l].reshape(1, 2 * ML_HEADS)
        qnw = jnp.tile(q_norm_w[l], N_HEADS).reshape(1, D_ATT)
        knw = jnp.tile(k_norm_w[l], N_HEADS).reshape(1, D_ATT)
        mlw = ml_norm_w[l].reshape(1, D_ML)
        shift, scale, gate = jnp.split(mod[l], 3, axis=-1)

        pm = lambda a: a[:bp].reshape(bp, 1, d)
        (qsb_t, ksb_t, ksb, vsb_t, vsb_th, qmb_t, qmb_th, kmb_t, kmb, vmb_t, vmb_th,
         qml, kml, vml, gact, gates, kmean) = \
            input_projection(xp, nw, pm(scale), pm(shift), w_bf, w_lo, bg, qnw, knw,
                             tm=PROMPT_TM, rows_per_mod=tp // PROMPT_TM, prompt=True)
        seq = lambda a: a.reshape(bp, tp, a.shape[-1])
        pages = lambda a: a.reshape(bp, tp // PAGE, D_ATT, PAGE)
        gact3 = seq(gact)
        ysb = sb_prompt(pages(qsb_t), seq(ksb), pages(vsb_th), gact3)
        ymb = moba_prompt(pages(qmb_t), pages(qmb_th), seq(kmb), pages(vmb_th),
                          kmean.reshape(bp, tp // KV_TILE, D_ATT), bias_p, gact3)
        gates3 = seq(gates)
        yml, c_p, n_p, m_p = mlstm_prompt(seq(qml), seq(kml), seq(vml), gates3, gates3.swapaxes(1, 2), gact3, mlw)
        xp = output_projection(ysb.reshape(mp, D_ATT), ymb.reshape(mp, D_ATT), yml.reshape(mp, D_ML), xp, pm(gate),
                               wo_bf, tm=PROMPT_TM, rows_per_mod=tp // PROMPT_TM)
        paged = lambda a: jnp.transpose(a.reshape(bp, tp // PAGE, N_HEADS, HEAD_DIM, PAGE), (0, 1, 4, 2, 3))
        for lst, a in zip(p_out, (paged(ksb_t), paged(vsb_t), paged(kmb_t), paged(vmb_t), c_p, n_p,
                                  m_p.reshape(bp, ML_HEADS))):
            lst.append(a)

        sm = lambda a: a[bp:].reshape(1, bs, d)
        (qsb, ksb, vsb, qmb, qmbh, kmb, vmb, qml, kml, vml, gact, gates) = \
            input_projection(xs, nw, sm(scale), sm(shift), w_bf, w_lo, bg, qnw, knw,
                             tm=bs, rows_per_mod=1, prompt=False)
        tok = lambda a: a.reshape(bs, 1, a.shape[-1])
        heads = lambda a: a.reshape(bs, N_HEADS, 1, HEAD_DIM)
        gact3 = tok(gact)
        ysb = sb_decode(page_table, tok(qsb), gact3[:, :, :D_ATT], sbk, sbv, l, group=min(SB_DECODE_GROUP, n_pages),
                        depth=STREAM_DEPTH)
        sel = moba_decode_select(page_table, qmb.reshape(bs, D_ATT, 1), mbk, l,
                                 group=min(MOBA_SELECT_GROUP, n_pages), depth=STREAM_DEPTH)
        sel = sel[:, :MOBA_TOPK, :N_HEADS].reshape(bs, MOBA_TOPK * N_HEADS)
        ymb = moba_decode(page_table, sel, heads(qmbh), heads(kmb), heads(vmb), bias_d,
                          heads(gact[:, D_ATT:2 * D_ATT]), mbk, mbv, l)
        yml, c_s, n_s, m_s = mlstm_decode(tok(qml), tok(kml), tok(vml), tok(gates), gact3, mlw,
                                          state_mlstm_C[l], state_mlstm_n[l], state_mlstm_m[l].reshape(bs, 1, ML_HEADS))
        xs = output_projection(ysb.reshape(bs, D_ATT), ymb.reshape(bs, D_ATT), yml.reshape(bs, D_ML), xs, sm(gate),
                               wo_bf, tm=bs, rows_per_mod=1)
        new = lambda a: a.reshape(bs, 1, N_HEADS, HEAD_DIM)
        for lst, a in zip(s_out, (new(ksb), new(vsb), new(kmb), new(vmb), c_s, n_s, m_s.reshape(bs, ML_HEADS))):
            lst.append(a)

    P = [jnp.stack(a) for a in p_out]
    S = [jnp.stack(a) for a in s_out]
    return (xp.reshape(bp, tp, d), xs.reshape(bs, 1, d),
            P[0], P[1], P[2], P[3], P[4], P[5], P[6], S[0], S[1], S[2], S[3], S[4], S[5], S[6])
```

```python
import functools
import math

import numpy as np
import jax
import jax.numpy as jnp
from jax import lax
from jax.experimental import pallas as pl
from jax.experimental.pallas import tpu as pltpu

F32 = jnp.float32
BF16 = jnp.bfloat16
I32 = jnp.int32
HI = lax.Precision.HIGHEST

EPS = 1e-6
HEAD_DIM = 64
N_HEADS = 4
D_ATT = N_HEADS * HEAD_DIM
ML_HEADS = 4
ML_D = 128
D_ML = ML_HEADS * ML_D
Q_BLOCK = 128
KV_TILE = 256
MOBA_TOPK = 3
N_BUCKETS = 32
MAX_DISTANCE = 128
ML_CHUNK = 128
PAGE = 128
HEAD_SHIFT = 6
NEG_INF = float("-inf")

VMEM_LIMIT = 56 * 1024 * 1024

_SIZES = [D_ATT] * 8 + [D_ML] * 5 + [ML_HEADS] * 2
_OFF = np.concatenate([[0], np.cumsum(_SIZES)]).tolist()
(O_QSB, O_KSB, O_VSB, O_GSB, O_QMB, O_KMB, O_VMB, O_GMB,
 O_QML, O_KML, O_VML, O_OML, O_GML, O_IML, O_FML, O_END) = _OFF


def _params(sem, vmem=VMEM_LIMIT):
    return pltpu.CompilerParams(dimension_semantics=sem, vmem_limit_bytes=vmem)


def _dot(a, b):
    return jnp.dot(a, b, preferred_element_type=F32)


def _dot_nt(a, b, precision=None):
    return lax.dot_general(a, b, (((1,), (1,)), ((), ())), precision=precision,
                           preferred_element_type=F32)


SOFTPLUS_CLAMP = 64.0


def _softplus(z):
    return jnp.maximum(jnp.log(1.0 + jnp.exp(jnp.minimum(z, SOFTPLUS_CLAMP))), z)


def _head_of_lane(shape, axis):
    return lax.shift_right_logical(lax.broadcasted_iota(I32, shape, axis), HEAD_SHIFT)


def _ada_kernel(c_ref, w_ref, b_ref, o_ref):
    c = c_ref[...]
    a = c * jax.nn.sigmoid(c)
    o_ref[...] = jnp.dot(a, w_ref[...], precision=HI, preferred_element_type=F32) + b_ref[...]


def ada_modulation(c_all, w_ada, b_ada):
    depth, d, d3 = w_ada.shape
    rows = c_all.shape[0]
    nt = d3 // d
    return pl.pallas_call(
        _ada_kernel,
        out_shape=jax.ShapeDtypeStruct((depth, rows, d3), F32),
        grid=(depth, nt),
        in_specs=[pl.BlockSpec((rows, d), lambda l, j: (0, 0)),
                  pl.BlockSpec((None, d, d), lambda l, j: (l, 0, j)),
                  pl.BlockSpec((None, 1, d), lambda l, j: (l, 0, j))],
        out_specs=pl.BlockSpec((None, rows, d), lambda l, j: (l, 0, j)),
        compiler_params=_params(("parallel", "parallel")),
        name="ada_modulation",
    )(c_all, w_ada, b_ada.reshape(depth, 1, d3))


def _bucket_thresholds():
    d = np.arange(0, 4 * MAX_DISTANCE, dtype=np.int32)
    max_exact = N_BUCKETS // 2
    df = np.maximum(d, 1).astype(np.float32)
    large = max_exact + (np.log(df / np.float32(max_exact)) / np.float32(math.log(MAX_DISTANCE / max_exact))
                         * np.float32(N_BUCKETS - max_exact)).astype(np.int32)
    large = np.minimum(large, N_BUCKETS - 1)
    bucket = np.where(d < max_exact, d, large)
    thr = []
    for k in range(1, N_BUCKETS):
        idx = np.nonzero(bucket >= k)[0]
        thr.append(int(idx[0]))
    assert all(np.all((bucket >= k) == (d >= t)) for k, t in zip(range(1, N_BUCKETS), thr))
    return thr


_BUCKET_THR = _bucket_thresholds()

_PATTERN_OFFSETS = (0, Q_BLOCK, KV_TILE)
P_FAR = len(_PATTERN_OFFSETS)


def _bias_from_dist(dist, rb_ref, h):
    b = jnp.full(dist.shape, rb_ref[0, h], F32)
    for k, t in zip(range(1, N_BUCKETS), _BUCKET_THR):
        b = jnp.where(dist >= t, rb_ref[k, h], b)
    return jnp.where(dist >= 0, b, NEG_INF)


def _bias_kernel(rb_ref, tile_ref, dec_ref):
    j = lax.broadcasted_iota(I32, (KV_TILE, Q_BLOCK), 0)
    i = lax.broadcasted_iota(I32, (KV_TILE, Q_BLOCK), 1)
    for p, off in enumerate(_PATTERN_OFFSETS):
        for h in range(N_HEADS):
            tile_ref[p, :, h * Q_BLOCK:(h + 1) * Q_BLOCK] = _bias_from_dist(off + i - j, rb_ref, h)
    for h in range(N_HEADS):
        tile_ref[P_FAR, :, h * Q_BLOCK:(h + 1) * Q_BLOCK] = jnp.full((KV_TILE, Q_BLOCK), rb_ref[N_BUCKETS - 1, h], F32)
    jr = lax.broadcasted_iota(I32, (8, KV_TILE), 1)
    for h in range(N_HEADS):
        dec_ref[h, 0] = jnp.full((8, KV_TILE), rb_ref[N_BUCKETS - 1, h], F32)
        dec_ref[h, 1] = _bias_from_dist(KV_TILE - jr, rb_ref, h)
        dec_ref[h, 2] = jnp.full((8, KV_TILE), rb_ref[0, h], F32)


def bias_tiles(rel_bias):
    return pl.pallas_call(
        _bias_kernel,
        out_shape=(jax.ShapeDtypeStruct((P_FAR + 1, KV_TILE, N_HEADS * Q_BLOCK), F32),
                   jax.ShapeDtypeStruct((N_HEADS, 3, 8, KV_TILE), F32)),
        in_specs=[pl.BlockSpec(memory_space=pltpu.SMEM)],
        name="bias_tiles",
    )(rel_bias)


def _inproj_kernel(x_ref, nw_ref, sc_ref, sh_ref, w_ref, wlo_ref, bg_ref, qnw_ref, knw_ref, *outs, prompt, layer,
                   n_carried):
    outs = list(outs[n_carried:])

    def paged(p):
        return [p[pg * PAGE:(pg + 1) * PAGE].T for pg in range(p.shape[0] // PAGE)]

    def emit(p, *, rows_bf16=False, rows_f32=False, pages_f32=False, pages_bf16=False):
        pages = paged(p) if (pages_f32 or pages_bf16) else None
        for want, dtype, is_pages in ((pages_f32, F32, True), (rows_f32, F32, False),
                                      (pages_bf16, BF16, True), (rows_bf16, BF16, False)):
            if not want:
                continue
            ref = outs.pop(0)
            if is_pages and len(ref.shape) == 4:
                for slab in range(ref.shape[0]):
                    for pg, t in enumerate(pages):
                        ref[slab, pg] = t.astype(dtype) if slab == layer else jnp.zeros_like(t, dtype)
            elif is_pages:
                for pg, t in enumerate(pages):
                    ref[pg] = t.astype(dtype)
            else:
                ref[...] = p.astype(dtype)

    x = x_ref[...]
    ms = jnp.mean(x * x, axis=-1, keepdims=True)
    h = (x * lax.rsqrt(ms + EPS) * nw_ref[...]) * (1.0 + sc_ref[...]) + sh_ref[...]
    hb = h.astype(BF16)
    hlo = (h - hb.astype(F32)).astype(BF16)

    def proj(a, b):
        return _dot(hb, w_ref[:, a:b])

    def proj3(a, b, la, lb):
        wh = w_ref[:, a:b]
        return _dot(hb, wh) + (_dot(hlo, wh) + _dot(hb, wlo_ref[:, la:lb]))

    def silu(g):
        return g * jax.nn.sigmoid(g)

    r = lax.broadcasted_iota(I32, (D_ATT, D_ATT), 0)
    c = lax.broadcasted_iota(I32, (D_ATT, D_ATT), 1)
    same_head = lax.shift_right_logical(r, HEAD_SHIFT) == lax.shift_right_logical(c, HEAD_SHIFT)
    head_mean = jnp.where(same_head, 1.0 / HEAD_DIM, 0.0).astype(BF16)

    def head_norm(p, w):
        sq = p * p
        hi = sq.astype(BF16)
        lo = (sq - hi.astype(F32)).astype(BF16)
        ms = _dot(hi, head_mean) + _dot(lo, head_mean)
        return p * lax.rsqrt(ms + EPS) * w

    score_scale = HEAD_DIM ** -0.5
    emit(proj(O_QSB, O_KSB) * score_scale, pages_bf16=prompt, rows_bf16=not prompt)
    emit(proj(O_KSB, O_VSB), pages_f32=prompt, rows_bf16=prompt, rows_f32=not prompt)
    emit(proj(O_VSB, O_GSB), pages_f32=prompt, pages_bf16=prompt, rows_f32=not prompt)
    g_sb = silu(proj(O_GSB, O_QMB))
    qn = head_norm(proj3(O_QMB, O_KMB, 0, D_ATT), qnw_ref[...])
    if prompt:
        qt = paged(qn)
        ref_f, ref_h = outs.pop(0), outs.pop(0)
        for pg, t in enumerate(qt):
            ref_f[pg] = t
            ref_h[pg] = (t * score_scale).astype(BF16)
    else:
        emit(qn, rows_f32=True)
        emit(qn * score_scale, rows_bf16=True)
    kn = head_norm(proj3(O_KMB, O_VMB, D_ATT, 2 * D_ATT), knw_ref[...])
    emit(kn, pages_f32=prompt, rows_bf16=prompt, rows_f32=not prompt)
    emit(proj(O_VMB, O_GMB), pages_f32=prompt, pages_bf16=prompt, rows_f32=not prompt)
    g_mb = silu(proj(O_GMB, O_QML))
    emit(proj(O_QML, O_KML), pages_bf16=prompt, rows_bf16=not prompt)
    emit(proj(O_KML, O_VML) * (ML_D ** -0.5), rows_bf16=True)
    emit(proj(O_VML, O_OML), pages_bf16=prompt, rows_bf16=not prompt)
    o = proj(O_OML, O_GML)
    g_ml = jax.nn.sigmoid(o) * silu(proj(O_GML, O_IML))
    gact_ref = outs.pop(0)
    gact_ref[:, 0:D_ATT] = g_sb.astype(BF16)
    gact_ref[:, D_ATT:2 * D_ATT] = g_mb.astype(BF16)
    gact_ref[:, 2 * D_ATT:] = g_ml.astype(BF16)
    pre = proj(O_IML, O_END) + bg_ref[...]
    is_f = lax.broadcasted_iota(I32, pre.shape, 1) >= ML_HEADS
    logf = jnp.minimum(pre, 0.0) - jnp.log1p(jnp.exp(-jnp.abs(pre)))
    outs.pop(0)[...] = jnp.where(is_f, logf, pre)
    if prompt:
        kmean_ref = outs.pop(0)
        for i in range(kn.shape[0] // KV_TILE):
            kmean_ref[i] = jnp.mean(kn[i * KV_TILE:(i + 1) * KV_TILE], axis=0, keepdims=True)
    assert not outs


def input_projection(x, norm_w, scale, shift, w_bf, w_lo, b_gates, qnw, knw, *, tm, rows_per_mod, prompt,
                     layer=0, depth=1, kv_carried=()):
    m, d = x.shape
    n_tiles = m // tm
    rmod = scale.shape[1]
    row = lambda i: (i, 0)
    const = lambda i: (0, 0)
    mod = lambda i: (i // rows_per_mod, 0, 0)

    def rows(width, dtype):
        return jax.ShapeDtypeStruct((m, width), dtype), pl.BlockSpec((tm, width), row)

    def pages(dtype, width=D_ATT):
        return (jax.ShapeDtypeStruct((m // PAGE, width, PAGE), dtype),
                pl.BlockSpec((tm // PAGE, width, PAGE), lambda i: (i, 0, 0)))

    def kv_pages():
        shape = jax.ShapeDtypeStruct((depth, m // PAGE, D_ATT, PAGE), F32)
        if kv_carried:
            return shape, pl.BlockSpec((None, tm // PAGE, D_ATT, PAGE), lambda i: (layer, i, 0, 0))
        return shape, pl.BlockSpec((depth, tm // PAGE, D_ATT, PAGE), lambda i: (0, i, 0, 0))

    kv_out_index = (1, 3, 7, 9)
    if prompt:
        nb = tm // KV_TILE
        outs = [pages(BF16), kv_pages(), rows(D_ATT, BF16), kv_pages(), pages(BF16),
                pages(F32), pages(BF16), kv_pages(), rows(D_ATT, BF16), kv_pages(), pages(BF16)]
        tail = [(jax.ShapeDtypeStruct((n_tiles, nb, 1, D_ATT), F32),
                 pl.BlockSpec((None, nb, 1, D_ATT), lambda i: (i, 0, 0, 0)))]
    else:
        outs = [rows(D_ATT, BF16), rows(D_ATT, F32), rows(D_ATT, F32),
                rows(D_ATT, F32), rows(D_ATT, BF16), rows(D_ATT, F32), rows(D_ATT, F32)]
        tail = []
    ml_qv = pages(BF16, D_ML) if prompt else rows(D_ML, BF16)
    outs += [ml_qv, rows(D_ML, BF16), ml_qv,
             rows(2 * D_ATT + D_ML, BF16), rows(2 * ML_HEADS, F32)] + tail
    n_in = 9
    return pl.pallas_call(
        functools.partial(_inproj_kernel, prompt=prompt, layer=layer, n_carried=len(kv_carried)),
        out_shape=tuple(o[0] for o in outs),
        input_output_aliases={n_in + j: kv_out_index[j] for j in range(len(kv_carried))},
        grid=(n_tiles,),
        in_specs=[pl.BlockSpec((tm, d), row),
                  pl.BlockSpec((1, d), const),
                  pl.BlockSpec((None, rmod, d), mod),
                  pl.BlockSpec((None, rmod, d), mod),
                  pl.BlockSpec(w_bf.shape, const),
                  pl.BlockSpec(w_lo.shape, const),
                  pl.BlockSpec((1, 2 * ML_HEADS), const),
                  pl.BlockSpec((1, D_ATT), const),
                  pl.BlockSpec((1, D_ATT), const)] + [pl.BlockSpec(memory_space=pl.ANY)] * len(kv_carried),
        out_specs=tuple(o[1] for o in outs),
        compiler_params=_params(("parallel",)),
        name="input_projection",
    )(x, norm_w, scale, shift, w_bf, w_lo, b_gates, qnw, knw, *kv_carried)


def _outproj_kernel(ysb_ref, ymb_ref, yml_ref, x_ref, gate_ref, w_ref, o_ref):
    y = (_dot(ysb_ref[...], w_ref[0:D_ATT, :]) + _dot(ymb_ref[...], w_ref[D_ATT:2 * D_ATT, :])
         + _dot(yml_ref[...], w_ref[2 * D_ATT:, :]))
    o_ref[...] = x_ref[...] + gate_ref[...] * y


def output_projection(ysb, ymb, yml, x, gate, w_bf, *, tm, rows_per_mod):
    m, d = x.shape
    rmod = gate.shape[1]
    row = lambda i: (i, 0)
    return pl.pallas_call(
        _outproj_kernel,
        out_shape=jax.ShapeDtypeStruct((m, d), F32),
        grid=(m // tm,),
        in_specs=[pl.BlockSpec((tm, D_ATT), row), pl.BlockSpec((tm, D_ATT), row), pl.BlockSpec((tm, D_ML), row),
                  pl.BlockSpec((tm, d), row),
                  pl.BlockSpec((None, rmod, d), lambda i: (i // rows_per_mod, 0, 0)),
                  pl.BlockSpec(w_bf.shape, lambda i: (0, 0))],
        out_specs=pl.BlockSpec((tm, d), row),
        compiler_params=_params(("parallel",)),
        name="output_projection",
    )(ysb, ymb, yml, x, gate, w_bf)


def _suffix_matrix(n, transposed=False):
    r = lax.broadcasted_iota(I32, (n, n), 0)
    c = lax.broadcasted_iota(I32, (n, n), 1)
    return jnp.where((c > r) if transposed else (r > c), 1.0, 0.0).astype(BF16)


def _heads_on_lanes(qt):
    head = _head_of_lane((D_ATT, 1), 0)
    return jnp.concatenate([jnp.where(head == h, qt, jnp.zeros_like(qt)) for h in range(N_HEADS)], axis=1)


def _heads_from_lanes(acc_t, q):
    head = _head_of_lane((D_ATT, 1), 0)
    out = jnp.zeros((D_ATT, q), F32)
    for h in range(N_HEADS):
        out = out + jnp.where(head == h, acc_t[:, h * q:(h + 1) * q], 0.0)
    return out.T


def _kv_tile(k_ref, vt_ref, n):
    start = pl.multiple_of(n * KV_TILE, KV_TILE)
    vt = jnp.concatenate([vt_ref[2 * n], vt_ref[2 * n + 1]], axis=1)
    return k_ref[pl.ds(start, KV_TILE), :], vt


def _sb_tile_t(q4t, k, vt, upper, carry, mask):
    z = _dot(k, q4t)
    sp = _softplus(z)
    if mask is not None:
        sp_sum = jnp.where(mask, sp, 0.0)
    else:
        sp_sum = sp
    later = _dot(upper, sp_sum.astype(BF16))
    w = jnp.exp(((z - sp) - later) - carry)
    if mask is not None:
        w = jnp.where(mask, w, 0.0)
    return _dot(vt, w.astype(BF16)), carry + jnp.sum(sp_sum, axis=0, keepdims=True)


def _sb_prompt_kernel(qt_ref, k_ref, vt_ref, g_ref, o_ref, acc_ref, carry_ref):
    i = pl.program_id(1)
    cols = N_HEADS * Q_BLOCK
    q4t = _heads_on_lanes(qt_ref[...])
    upper = _suffix_matrix(KV_TILE, transposed=True)
    last = (i * Q_BLOCK) // KV_TILE
    kpos = last * KV_TILE + lax.broadcasted_iota(I32, (KV_TILE, cols), 0)
    qpos = i * Q_BLOCK + (lax.broadcasted_iota(I32, (KV_TILE, cols), 1) & (Q_BLOCK - 1))
    k, vt = _kv_tile(k_ref, vt_ref, last)
    pv, carry = _sb_tile_t(q4t, k, vt, upper, jnp.zeros((1, cols), F32), kpos < qpos)
    acc_ref[...] = pv
    carry_ref[...] = carry

    @pl.when(last % 2 == 1)
    def _():
        k, vt = _kv_tile(k_ref, vt_ref, last - 1)
        pv, carry = _sb_tile_t(q4t, k, vt, upper, carry_ref[...], None)
        acc_ref[...] += pv
        carry_ref[...] = carry

    first = last - 1 - last % 2

    def body(s, carry):
        ka, vta = _kv_tile(k_ref, vt_ref, first - 2 * s)
        kb, vtb = _kv_tile(k_ref, vt_ref, first - 2 * s - 1)
        za, zb = _dot(ka, q4t), _dot(kb, q4t)
        spa, spb = _softplus(za), _softplus(zb)
        la, lb = _dot(upper, spa.astype(BF16)), _dot(upper, spb.astype(BF16))
        ca = carry + jnp.sum(spa, axis=0, keepdims=True)
        wa = jnp.exp(((za - spa) - la) - carry)
        wb = jnp.exp(((zb - spb) - lb) - ca)
        acc_ref[...] += _dot(vta, wa.astype(BF16)) + _dot(vtb, wb.astype(BF16))
        return ca + jnp.sum(spb, axis=0, keepdims=True)

    lax.fori_loop(0, last // 2, body, carry_ref[...])
    o_ref[...] = (_heads_from_lanes(acc_ref[...], Q_BLOCK) * g_ref[...].astype(F32)).astype(BF16)


def sb_prompt(qt, k, vt, gact):
    b, t, _ = k.shape
    n_pages = t // PAGE
    blk = pl.BlockSpec((None, Q_BLOCK, D_ATT), lambda bi, i: (bi, i, 0))
    return pl.pallas_call(
        _sb_prompt_kernel,
        out_shape=jax.ShapeDtypeStruct((b, t, D_ATT), BF16),
        grid=(b, t // Q_BLOCK),
        in_specs=[pl.BlockSpec((None, None, D_ATT, PAGE), lambda bi, i: (bi, i, 0, 0)),
                  pl.BlockSpec((None, t, D_ATT), lambda bi, i: (bi, 0, 0)),
                  pl.BlockSpec((None, n_pages, D_ATT, PAGE), lambda bi, i: (bi, 0, 0, 0)),
                  blk],
        out_specs=blk,
        scratch_shapes=[pltpu.VMEM((D_ATT, N_HEADS * Q_BLOCK), F32), pltpu.VMEM((1, N_HEADS * Q_BLOCK), F32)],
        compiler_params=_params(("parallel", "arbitrary")),
        name="sb_prompt",
    )(qt, k, vt, gact)


def _moba_select_t(gate, valid, idx):
    n = gate.shape[0]
    gate = jnp.where(valid, gate, NEG_INF)
    rank = jnp.zeros(gate.shape, I32)
    for m in range(n):
        gm = gate[m:m + 1, :]
        beats = (gm > gate) | ((gm == gate) & (idx > m))
        rank = rank + beats.astype(I32)
    return valid & (rank < MOBA_TOPK)


def _moba_prompt_kernel(qft_ref, qht_ref, k_ref, vt_ref, km_ref, bias_ref, g_ref, o_ref, acc_ref, m_ref, l_ref, sel_ref):
    i = pl.program_id(1)
    nb = km_ref.shape[0]
    own = (i * Q_BLOCK) // KV_TILE
    odd = (i * Q_BLOCK) % KV_TILE != 0
    q4t = _heads_on_lanes(qht_ref[...])
    gate = jnp.dot(km_ref[...], _heads_on_lanes(qft_ref[...]), precision=HI, preferred_element_type=F32)
    idx = lax.broadcasted_iota(I32, (nb, 1), 0)
    sel = _moba_select_t(gate, idx < own, idx)
    sel_ref[...] = jnp.where(sel, 0.0, NEG_INF)

    def attend(blocks, first=False):
        tiles = [_kv_tile(k_ref, vt_ref, n) for n, _ in blocks]
        scores = []
        for (k, _), (_, terms) in zip(tiles, blocks):
            s = _dot(k, q4t)
            for t in terms:
                s = s + t
            scores.append(s)
        m_new = functools.reduce(jnp.maximum, [jnp.max(s, axis=0, keepdims=True) for s in scores])
        if not first:
            m_old = m_ref[...]
            m_new = jnp.maximum(m_old, m_new)
            alpha = jnp.exp(m_old - m_new)
        ps = [jnp.exp(s - m_new) for s in scores]
        l_new = functools.reduce(jnp.add, [jnp.sum(p, axis=0, keepdims=True) for p in ps])
        pv = functools.reduce(jnp.add, [_dot(vt, p.astype(BF16)) for (_, vt), p in zip(tiles, ps)])
        m_ref[...] = m_new
        l_ref[...] = l_new if first else alpha * l_ref[...] + l_new
        acc_ref[...] = pv if first else alpha * acc_ref[...] + pv

    attend([(own, [bias_ref[jnp.where(odd, 1, 0)]])], first=True)
    near_pat = jnp.where(odd, P_FAR, 2)
    far_row = bias_ref[P_FAR, 0:1, :]
    for n in range(0, nb - 2, 2):
        @pl.when(n + 1 < own)
        def _():
            attend([(n, [far_row + sel_ref[n:n + 1, :]]),
                    (n + 1, [bias_ref[jnp.where(n + 1 == own - 1, near_pat, P_FAR)], sel_ref[n + 1:n + 2, :]])])

    @pl.when(own % 2 == 1)
    def _():
        attend([(own - 1, [bias_ref[near_pat], sel_ref[pl.ds(own - 1, 1), :]])])

    y = _heads_from_lanes(acc_ref[...] / l_ref[...], Q_BLOCK)
    o_ref[...] = (y * g_ref[...].astype(F32)).astype(BF16)


def moba_prompt(qft, qht, k, vt, kmean, bias, gact):
    b, t, _ = k.shape
    nb = t // KV_TILE
    cols = N_HEADS * Q_BLOCK
    blk = pl.BlockSpec((None, Q_BLOCK, D_ATT), lambda bi, i: (bi, i, 0))
    qpage = pl.BlockSpec((None, None, D_ATT, PAGE), lambda bi, i: (bi, i, 0, 0))
    return pl.pallas_call(
        _moba_prompt_kernel,
        out_shape=jax.ShapeDtypeStruct((b, t, D_ATT), BF16),
        grid=(b, t // Q_BLOCK),
        in_specs=[qpage, qpage,
                  pl.BlockSpec((None, t, D_ATT), lambda bi, i: (bi, 0, 0)),
                  pl.BlockSpec((None, t // PAGE, D_ATT, PAGE), lambda bi, i: (bi, 0, 0, 0)),
                  pl.BlockSpec((None, nb, D_ATT), lambda bi, i: (bi, 0, 0)),
                  pl.BlockSpec(bias.shape, lambda bi, i: (0, 0, 0)),
                  pl.BlockSpec((None, Q_BLOCK, D_ATT), lambda bi, i: (bi, i, 1))],
        out_specs=blk,
        scratch_shapes=[pltpu.VMEM((D_ATT, cols), F32), pltpu.VMEM((1, cols), F32), pltpu.VMEM((1, cols), F32),
                        pltpu.VMEM((nb, cols), F32)],
        compiler_params=_params(("parallel", "arbitrary")),
        name="moba_prompt",
    )(qft, qht, k, vt, kmean, bias, gact)


def _gated_head_norm(h, w, g):
    hn = h * lax.rsqrt(jnp.mean(h * h, axis=-1, keepdims=True) + EPS) * w
    return (hn * g.astype(F32)).astype(BF16)


def _mlstm_prompt_kernel(qt_ref, k_ref, vt_ref, gc_ref, gr_ref, g_ref, nw_ref, h_ref, c_ref, n_ref, m_ref,
                         cst_ref, nst_ref, ms_ref):
    ci = pl.program_id(1)
    L = ML_CHUNK

    @pl.when(ci == 0)
    def _():
        cst_ref[...] = jnp.zeros_like(cst_ref)
        nst_ref[...] = jnp.zeros_like(nst_ref)
        ms_ref[...] = jnp.zeros_like(ms_ref)

    r = lax.broadcasted_iota(I32, (L, L), 0)
    c = lax.broadcasted_iota(I32, (L, L), 1)
    lower = jnp.where(c <= r, 1.0, 0.0).astype(F32)
    upper = jnp.where(r <= c, 1.0, 0.0).astype(F32)
    src_before_tgt = r <= c
    gc = gc_ref[...]
    gr = gr_ref[...]
    bh_c = jnp.dot(lower, gc, precision=HI, preferred_element_type=F32)
    bh_r = jnp.dot(gr, upper, precision=HI, preferred_element_type=F32)

    for h in range(ML_HEADS):
        sl = slice(h * ML_D, (h + 1) * ML_D)
        qt, k, vt = qt_ref[sl, :], k_ref[:, sl], vt_ref[sl, :]
        m_prev = ms_ref[h][0:1, 0:1]
        b_r = bh_r[ML_HEADS + h:ML_HEADS + h + 1, :]
        a_r = gr[h:h + 1, :] - b_r
        a_c = gc[:, h:h + 1] - bh_c[:, ML_HEADS + h:ML_HEADS + h + 1]
        dm = jnp.where(src_before_tgt, a_c + b_r, NEG_INF)
        g = b_r + m_prev
        m_row = jnp.maximum(g, jnp.max(dm, axis=0, keepdims=True))
        w_inter = jnp.exp(g - m_row)
        sc = _dot(k, qt) * jnp.exp(dm - m_row)
        cst, nst = cst_ref[h], nst_ref[h]
        num = _dot(vt, sc.astype(BF16)) + w_inter * _dot(cst.astype(BF16), qt)
        den = jnp.sum(sc, axis=0, keepdims=True) + w_inter * _dot(nst.astype(BF16), qt)[0:1]
        hout = (num / jnp.maximum(jnp.abs(den), jnp.exp(-m_row))).T
        h_ref[:, sl] = _gated_head_norm(hout, nw_ref[:, sl], g_ref[:, sl])

        b_last = b_r[:, L - 1:L]
        m_new = jnp.maximum(b_last + m_prev, jnp.max(b_last + a_r, axis=-1, keepdims=True))
        ws = jnp.exp(b_last + a_r - m_new)
        wc = jnp.exp(b_last + m_prev - m_new)
        cst_ref[h] = wc * cst + _dot((vt.astype(F32) * ws).astype(BF16), k)
        nst_ref[h] = wc * nst + _dot(jnp.broadcast_to(ws, (8, L)).astype(BF16), k)
        ms_ref[h] = jnp.broadcast_to(m_new, ms_ref.shape[1:])

    @pl.when(ci == pl.num_programs(1) - 1)
    def _():
        for h in range(ML_HEADS):
            c_ref[h] = cst_ref[h]
            n_ref[h:h + 1, :] = nst_ref[h][0:1]
            m_ref[:, h:h + 1] = ms_ref[h][0:1, 0:1]


def mlstm_prompt(qt, k, vt, gates, gates_t, gact, norm_w):
    b, t, _ = k.shape
    blk = pl.BlockSpec((None, ML_CHUNK, D_ML), lambda bi, c: (bi, c, 0))
    page = pl.BlockSpec((None, None, D_ML, ML_CHUNK), lambda bi, c: (bi, c, 0, 0))
    return pl.pallas_call(
        _mlstm_prompt_kernel,
        out_shape=(jax.ShapeDtypeStruct((b, t, D_ML), BF16),
                   jax.ShapeDtypeStruct((b, ML_HEADS, ML_D, ML_D), F32),
                   jax.ShapeDtypeStruct((b, ML_HEADS, ML_D), F32),
                   jax.ShapeDtypeStruct((b, 1, ML_HEADS), F32)),
        grid=(b, t // ML_CHUNK),
        in_specs=[page, blk, page,
                  pl.BlockSpec((None, ML_CHUNK, 2 * ML_HEADS), lambda bi, c: (bi, c, 0)),
                  pl.BlockSpec((None, 2 * ML_HEADS, ML_CHUNK), lambda bi, c: (bi, 0, c)),
                  pl.BlockSpec((None, ML_CHUNK, D_ML), lambda bi, c: (bi, c, 1)),
                  pl.BlockSpec((1, D_ML), lambda bi, c: (0, 0))],
        out_specs=(blk,
                   pl.BlockSpec((None, ML_HEADS, ML_D, ML_D), lambda bi, c: (bi, 0, 0, 0)),
                   pl.BlockSpec((None, ML_HEADS, ML_D), lambda bi, c: (bi, 0, 0)),
                   pl.BlockSpec((None, 1, ML_HEADS), lambda bi, c: (bi, 0, 0))),
        scratch_shapes=[pltpu.VMEM((ML_HEADS, ML_D, ML_D), F32), pltpu.VMEM((ML_HEADS, 8, ML_D), F32),
                        pltpu.VMEM((ML_HEADS, 8, 128), F32)],
        compiler_params=_params(("parallel", "arbitrary")),
        name="mlstm_prompt",
    )(qt, k, vt, gates, gates_t, gact, norm_w)


def _page_stream(pt_ref, hbm_refs, bufs, sems, *, layer, group, depth, reverse, consume, init):
    n_seq, n_pages = pt_ref.shape
    per_seq = n_pages // group
    total = n_seq * per_seq

    def copies(g, slot):
        seq, c = g // per_seq, g % per_seq
        out = []
        for r in range(group):
            walk = c * group + r
            page = pt_ref[seq, n_pages - 1 - walk if reverse else walk]
            for hbm, buf, sem in zip(hbm_refs, bufs, sems):
                out.append(pltpu.make_async_copy(hbm.at[layer, page], buf.at[slot, r], sem.at[slot]))
        return out

    for g in range(depth - 1):
        for cp in copies(g, g):
            cp.start()

    def body(g, carry):
        slot = lax.rem(g, depth)
        ahead = g + depth - 1

        @pl.when(ahead < total)
        def _():
            for cp in copies(ahead, lax.rem(ahead, depth)):
                cp.start()

        for cp in copies(g, slot):
            cp.wait()
        return consume(g // per_seq, g % per_seq, slot, carry)

    return lax.fori_loop(0, total, body, init)


def _sb_decode_kernel(pt_ref, q_ref, g_ref, k_hbm, v_hbm, o_ref, kbuf, vbuf, ksem, vsem, *, layer, group, depth):
    tiles = group // 2
    last_chunk = pt_ref.shape[1] // group - 1
    rowi = lax.broadcasted_iota(I32, (8, D_ATT), 0)
    own_head = _head_of_lane((8, D_ATT), 1) == rowi
    suffix = _suffix_matrix(KV_TILE)

    def consume(seq, c, slot, state):
        fresh = c == 0
        acc = jnp.where(fresh, 0.0, state[0])
        carry = jnp.where(fresh, 0.0, state[1])
        qm = jnp.where(own_head, jnp.broadcast_to(q_ref[seq].astype(F32), (8, D_ATT)), 0.0).astype(BF16)

        def tile(buf, t):
            lo, hi = buf[slot, 2 * t + 1].reshape(D_ATT, PAGE), buf[slot, 2 * t].reshape(D_ATT, PAGE)
            return jnp.concatenate([lo, hi], axis=1).astype(BF16)

        z = jnp.concatenate([_dot(qm, tile(kbuf, t)) for t in range(tiles)], axis=0)
        sp = _softplus(z)
        lf = -sp
        hi = lf.astype(BF16).astype(F32)
        both = _dot(jnp.concatenate([hi, lf - hi], axis=0).astype(BF16), suffix)
        later = both[:8 * tiles] + both[8 * tiles:]
        tot = jnp.sum(lf, axis=-1, keepdims=True)
        carries = [carry]
        for t in range(tiles):
            carries.append(carries[-1] + tot[8 * t:8 * (t + 1)])
        w = jnp.exp((z - sp) + later + jnp.concatenate(carries[:tiles], axis=0))
        for t in range(tiles):
            acc = acc + _dot_nt(w[8 * t:8 * (t + 1)].astype(BF16), tile(vbuf, t))

        @pl.when(c == last_chunk)
        def _():
            y = jnp.sum(jnp.where(own_head, acc, 0.0), axis=0, keepdims=True)
            o_ref[seq] = (y * g_ref[seq].astype(F32)).astype(BF16)

        return acc, carries[tiles]

    _page_stream(pt_ref, (k_hbm, v_hbm), (kbuf, vbuf), (ksem, vsem), layer=layer, group=group, depth=depth,
                 reverse=True, consume=consume, init=(jnp.zeros((8, D_ATT), F32), jnp.zeros((8, 1), F32)))


def _page_buffers(depth, group):
    return pltpu.VMEM((depth, group, N_HEADS, HEAD_DIM, PAGE), F32)


def sb_decode(page_table, q, gact, cache_k, cache_v, layer, *, group, depth):
    b = page_table.shape[0]
    rows = pl.BlockSpec((b, 1, D_ATT), lambda i, pt: (0, 0, 0))
    hbm = pl.BlockSpec(memory_space=pl.ANY)
    grid_spec = pltpu.PrefetchScalarGridSpec(
        num_scalar_prefetch=1,
        grid=(1,),
        in_specs=[rows, rows, hbm, hbm],
        out_specs=rows,
        scratch_shapes=[_page_buffers(depth, group), _page_buffers(depth, group),
                        pltpu.SemaphoreType.DMA((depth,)), pltpu.SemaphoreType.DMA((depth,))],
    )
    return pl.pallas_call(
        functools.partial(_sb_decode_kernel, layer=layer, group=group, depth=depth),
        out_shape=jax.ShapeDtypeStruct((b, 1, D_ATT), BF16),
        grid_spec=grid_spec,
        compiler_params=_params(("arbitrary",)),
        name="sb_decode",
    )(page_table, q, gact, cache_k, cache_v)


def _moba_gate_kernel(pt_ref, q_ref, k_hbm, idx_ref, kbuf, ksem, gs_ref, *, layer, group, depth):
    per_chunk = group // 2
    last_chunk = pt_ref.shape[1] // group - 1

    def consume(seq, c, slot, carry):
        qc = q_ref[seq]
        for r in range(per_chunk):
            pair = kbuf[slot, 2 * r].reshape(D_ATT, PAGE) + kbuf[slot, 2 * r + 1].reshape(D_ATT, PAGE)
            prod = pair * qc
            for h in range(N_HEADS):
                gs_ref[h, pl.ds(c * per_chunk + r, 1), :] = jnp.sum(prod[h * HEAD_DIM:(h + 1) * HEAD_DIM], axis=0,
                                                                    keepdims=True)

        @pl.when(c == last_chunk)
        def _():
            nb = gs_ref.shape[1]
            lane = lax.broadcasted_iota(I32, (nb, 128), 1)
            gate = jnp.zeros((nb, 128), F32)
            for h in range(N_HEADS):
                gate = gate + jnp.where(lane == h, jnp.sum(gs_ref[h], axis=-1, keepdims=True) * (1.0 / KV_TILE), 0.0)
            blk = lax.broadcasted_iota(I32, (nb, 128), 0)
            rank = jnp.zeros((nb, 128), I32)
            for m in range(nb):
                gm = gate[m:m + 1, :]
                rank = rank + ((gm > gate) | ((gm == gate) & (blk > m))).astype(I32)
            blk_f = blk.astype(F32)
            rows = [jnp.sum(jnp.where(rank == t, blk_f, 0.0), axis=0, keepdims=True) for t in range(MOBA_TOPK)]
            idx_ref[seq] = jnp.concatenate(rows + [jnp.zeros((8 - MOBA_TOPK, 128), F32)], axis=0).astype(I32)

        return carry

    _page_stream(pt_ref, (k_hbm,), (kbuf,), (ksem,), layer=layer, group=group, depth=depth, reverse=False,
                 consume=consume, init=0)


def moba_decode_select(page_table, q_col, cache_k, layer, *, group, depth):
    b, n_pages = page_table.shape
    nb = n_pages * PAGE // KV_TILE
    grid_spec = pltpu.PrefetchScalarGridSpec(
        num_scalar_prefetch=1,
        grid=(1,),
        in_specs=[pl.BlockSpec((b, D_ATT, 1), lambda i, pt: (0, 0, 0)), pl.BlockSpec(memory_space=pl.ANY)],
        out_specs=pl.BlockSpec((b, 8, 128), lambda i, pt: (0, 0, 0)),
        scratch_shapes=[_page_buffers(depth, group), pltpu.SemaphoreType.DMA((depth,)),
                        pltpu.VMEM((N_HEADS, nb, PAGE), F32)],
    )
    return pl.pallas_call(
        functools.partial(_moba_gate_kernel, layer=layer, group=group, depth=depth),
        out_shape=jax.ShapeDtypeStruct((b, 8, 128), I32),
        grid_spec=grid_spec,
        compiler_params=_params(("arbitrary",)),
        name="moba_decode_select",
    )(page_table, q_col, cache_k)


def _moba_decode_kernel(pt_ref, sel_ref, q_ref, kn_ref, vn_ref, bias_ref, g_ref, k_hbm, v_hbm, o_ref,
                        kbuf, vbuf, ksem, vsem, *, layer, last_block):
    n_seq = pt_ref.shape[0]

    def piece(t, h, half):
        return (t * N_HEADS + h) * 2 + half

    def copies(seq, slot):
        out = []
        for t in range(MOBA_TOPK):
            for h in range(N_HEADS):
                blk = sel_ref[seq, t * N_HEADS + h]
                for half in range(2):
                    page = pt_ref[seq, 2 * blk + half]
                    i = piece(t, h, half)
                    out.append(pltpu.make_async_copy(k_hbm.at[layer, page, h], kbuf.at[slot, i], ksem.at[slot]))
                    out.append(pltpu.make_async_copy(v_hbm.at[layer, page, h], vbuf.at[slot, i], vsem.at[slot]))
        return out

    for cp in copies(0, 0):
        cp.start()

    def body(seq, carry):
        slot = lax.rem(seq, 2)

        @pl.when(seq + 1 < n_seq)
        def _():
            for cp in copies(seq + 1, 1 - slot):
                cp.start()

        for cp in copies(seq, slot):
            cp.wait()
        for h in range(N_HEADS):
            qf = jnp.broadcast_to(q_ref[seq, h].astype(F32), (8, HEAD_DIM))
            q = qf.astype(BF16)
            s_self = (jnp.sum(qf * kn_ref[seq, h].astype(BF16).astype(F32), axis=-1, keepdims=True)
                      + bias_ref[h, 2][:, 0:1])
            scores = []
            for t in range(MOBA_TOPK):
                kt = jnp.concatenate([kbuf[slot, piece(t, h, 0)], kbuf[slot, piece(t, h, 1)]], axis=1).astype(BF16)
                near = sel_ref[seq, t * N_HEADS + h] == last_block
                scores.append(_dot(q, kt) + bias_ref[h, jnp.where(near, 1, 0)])
            m = s_self
            for s in scores:
                m = jnp.maximum(m, jnp.max(s, axis=-1, keepdims=True))
            p_self = jnp.exp(s_self - m)
            l = p_self
            acc = p_self * vn_ref[seq, h].astype(BF16).astype(F32)
            for t, s in enumerate(scores):
                p = jnp.exp(s - m)
                l = l + jnp.sum(p, axis=-1, keepdims=True)
                vt = jnp.concatenate([vbuf[slot, piece(t, h, 0)], vbuf[slot, piece(t, h, 1)]], axis=1).astype(BF16)
                acc = acc + _dot_nt(p.astype(BF16), vt)
            o_ref[seq, h] = ((acc / l)[0:1] * g_ref[seq, h].astype(F32)).astype(BF16)
        return carry

    lax.fori_loop(0, n_seq, body, 0)


def moba_decode(page_table, sel, qh, k_new, v_new, dec_bias, gact_mb, cache_k, cache_v, layer):
    b, n_pages = page_table.shape
    last_block = n_pages * PAGE // KV_TILE - 1

    pieces = 2 * MOBA_TOPK * N_HEADS
    vec = pl.BlockSpec((b, N_HEADS, 1, HEAD_DIM), lambda i, pt, sl: (0, 0, 0, 0))
    hbm = pl.BlockSpec(memory_space=pl.ANY)
    grid_spec = pltpu.PrefetchScalarGridSpec(
        num_scalar_prefetch=2,
        grid=(1,),
        in_specs=[vec, vec, vec, pl.BlockSpec(dec_bias.shape, lambda i, pt, sl: (0, 0, 0, 0)), vec, hbm, hbm],
        out_specs=vec,
        scratch_shapes=[pltpu.VMEM((2, pieces, HEAD_DIM, PAGE), F32), pltpu.VMEM((2, pieces, HEAD_DIM, PAGE), F32),
                        pltpu.SemaphoreType.DMA((2,)), pltpu.SemaphoreType.DMA((2,))],
    )
    return pl.pallas_call(
        functools.partial(_moba_decode_kernel, layer=layer, last_block=last_block),
        out_shape=jax.ShapeDtypeStruct((b, N_HEADS, 1, HEAD_DIM), BF16),
        grid_spec=grid_spec,
        compiler_params=_params(("arbitrary",)),
        name="moba_decode",
    )(page_table, sel, qh, k_new, v_new, dec_bias, gact_mb, cache_k, cache_v)


def _mlstm_decode_kernel(q_ref, k_ref, v_ref, gt_ref, g_ref, nw_ref, c_ref, n_ref, m_ref,
                         h_ref, co_ref, no_ref, mo_ref):
    r = lax.broadcasted_iota(I32, (ML_D, ML_D), 0)
    c = lax.broadcasted_iota(I32, (ML_D, ML_D), 1)
    eye = r == c
    for h in range(ML_HEADS):
        sl = slice(h * ML_D, (h + 1) * ML_D)
        q = q_ref[:, sl].astype(F32)
        k = k_ref[:, sl].astype(F32)
        v = v_ref[:, sl].astype(F32)
        i_pre = gt_ref[:, h:h + 1]
        logf = gt_ref[:, ML_HEADS + h:ML_HEADS + h + 1]
        m_prev = m_ref[:, h:h + 1]
        cm = c_ref[h]
        nv = n_ref[h:h + 1, :]
        g = logf + m_prev
        m_new = jnp.maximum(g, i_pre)
        w_in = jnp.exp(i_pre - m_new)
        w_st = jnp.exp(g - m_new)
        qk = jnp.sum(q * k, axis=-1, keepdims=True)
        cq = _dot_nt(jnp.broadcast_to(q, (8, ML_D)), cm, precision=HI)[0:1]
        nq = jnp.sum(nv * q, axis=-1, keepdims=True)
        num = (qk * w_in) * v + w_st * cq
        den = qk * w_in + w_st * nq
        hout = num / jnp.maximum(jnp.abs(den), jnp.exp(-m_new))
        h_ref[:, sl] = _gated_head_norm(hout, nw_ref[:, sl], g_ref[:, sl])
        v_diag = jnp.where(eye, jnp.broadcast_to(v, (ML_D, ML_D)), 0.0)
        outer = jnp.dot(v_diag, jnp.broadcast_to(k, (ML_D, ML_D)), precision=HI, preferred_element_type=F32)
        co_ref[h] = w_st * cm + w_in * outer
        no_ref[h:h + 1, :] = w_st * nv + w_in * k
        mo_ref[:, h:h + 1] = m_new


def mlstm_decode(q, k, v, gates, gact, norm_w, c0, n0, m0):
    b = q.shape[0]
    vec = pl.BlockSpec((None, 1, D_ML), lambda bi: (bi, 0, 0))
    cs = pl.BlockSpec((None, ML_HEADS, ML_D, ML_D), lambda bi: (bi, 0, 0, 0))
    ns = pl.BlockSpec((None, ML_HEADS, ML_D), lambda bi: (bi, 0, 0))
    msp = pl.BlockSpec((None, 1, ML_HEADS), lambda bi: (bi, 0, 0))
    return pl.pallas_call(
        _mlstm_decode_kernel,
        out_shape=(jax.ShapeDtypeStruct((b, 1, D_ML), BF16),
                   jax.ShapeDtypeStruct(c0.shape, F32),
                   jax.ShapeDtypeStruct(n0.shape, F32),
                   jax.ShapeDtypeStruct(m0.shape, F32)),
        grid=(b,),
        in_specs=[vec, vec, vec,
                  pl.BlockSpec((None, 1, 2 * ML_HEADS), lambda bi: (bi, 0, 0)),
                  pl.BlockSpec((None, 1, D_ML), lambda bi: (bi, 0, 1)),
                  pl.BlockSpec((1, D_ML), lambda bi: (0, 0)),
                  cs, ns, msp],
        out_specs=(vec, cs, ns, msp),
        compiler_params=_params(("parallel",)),
        name="mlstm_decode",
    )(q, k, v, gates, gact, norm_w, c0, n0, m0)


SB_DECODE_GROUP = 16
MOBA_SELECT_GROUP = 16
STREAM_DEPTH = 3
PROMPT_TM = 512


def _layer_weights(w_in_l, w_out_l):
    w_bf = w_in_l.astype(BF16)
    w_lo = (w_in_l[:, O_QMB:O_VMB] - w_bf[:, O_QMB:O_VMB].astype(F32)).astype(BF16)
    return w_bf, w_lo, w_out_l.astype(BF16)


def kernel(x_prompt, x_sample, cache_sb_k, cache_sb_v, cache_moba_k, cache_moba_v, state_mlstm_C, state_mlstm_n,
           state_mlstm_m, page_table, c_prompt, c_sample, norm_w, w_ada, b_ada, w_in, b_gates, q_norm_w, k_norm_w,
           rel_bias, ml_norm_w, w_out):
    bp, tp, d = x_prompt.shape
    bs = x_sample.shape[0]
    depth = w_in.shape[0]
    mp = bp * tp
    n_pages = page_table.shape[1]

    mod = ada_modulation(jnp.concatenate([c_prompt, c_sample], axis=0), w_ada, b_ada)
    bias_p, bias_d = bias_tiles(rel_bias)
    lanes_last = lambda a: jnp.transpose(a, (0, 1, 3, 4, 2))
    sbk, sbv, mbk, mbv = (lanes_last(a) for a in (cache_sb_k, cache_sb_v, cache_moba_k, cache_moba_v))

    xp = x_prompt.reshape(mp, d)
    xs = x_sample.reshape(bs, d)
    p_out = [[] for _ in range(3)]
    s_out = [[] for _ in range(7)]
    kv_pages = ()
    for l in range(depth):
        w_bf, w_lo, wo_bf = _layer_weights(w_in[l], w_out[l])
        nw = norm_w[l].reshape(1, d)
        bg = b_gates[l].reshape(1, 2 * ML_HEADS)
        qnw = jnp.tile(q_norm_w[l], N_HEADS).reshape(1, D_ATT)
        knw = jnp.tile(k_norm_w[l], N_HEADS).reshape(1, D_ATT)
        mlw = ml_norm_w[l].reshape(1, D_ML)
        shift, scale, gate = jnp.split(mod[l], 3, axis=-1)

        pm = lambda a: a[:bp].reshape(bp, 1, d)
        (qsb_t, ksb_t, ksb, vsb_t, vsb_th, qmb_t, qmb_th, kmb_t, kmb, vmb_t, vmb_th,
         qml, kml, vml, gact, gates, kmean) = \
            input_projection(xp, nw, pm(scale), pm(shift), w_bf, w_lo, bg, qnw, knw,
                             tm=PROMPT_TM, rows_per_mod=tp // PROMPT_TM, prompt=True,
                             layer=l, depth=depth, kv_carried=kv_pages)
        kv_pages = (ksb_t, vsb_t, kmb_t, vmb_t)
        seq = lambda a: a.reshape(bp, tp, a.shape[-1])
        pages = lambda a: a.reshape(bp, tp // PAGE, D_ATT, PAGE)
        gact3 = seq(gact)
        ysb = sb_prompt(pages(qsb_t), seq(ksb), pages(vsb_th), gact3)
        ymb = moba_prompt(pages(qmb_t), pages(qmb_th), seq(kmb), pages(vmb_th),
                          kmean.reshape(bp, tp // KV_TILE, D_ATT), bias_p, gact3)
        gates3 = seq(gates)
        ml_pages = lambda a: a.reshape(bp, tp // ML_CHUNK, D_ML, ML_CHUNK)
        yml, c_p, n_p, m_p = mlstm_prompt(ml_pages(qml), seq(kml), ml_pages(vml), gates3, gates3.swapaxes(1, 2),
                                          gact3, mlw)
        xp = output_projection(ysb.reshape(mp, D_ATT), ymb.reshape(mp, D_ATT), yml.reshape(mp, D_ML), xp, pm(gate),
                               wo_bf, tm=PROMPT_TM, rows_per_mod=tp // PROMPT_TM)
        for lst, a in zip(p_out, (c_p, n_p, m_p.reshape(bp, ML_HEADS))):
            lst.append(a)

        sm = lambda a: a[bp:].reshape(1, bs, d)
        (qsb, ksb, vsb, qmb, qmbh, kmb, vmb, qml, kml, vml, gact, gates) = \
            input_projection(xs, nw, sm(scale), sm(shift), w_bf, w_lo, bg, qnw, knw,
                             tm=bs, rows_per_mod=1, prompt=False)
        tok = lambda a: a.reshape(bs, 1, a.shape[-1])
        heads = lambda a: a.reshape(bs, N_HEADS, 1, HEAD_DIM)
        gact3 = tok(gact)
        ysb = sb_decode(page_table, tok(qsb), gact3[:, :, :D_ATT], sbk, sbv, l, group=min(SB_DECODE_GROUP, n_pages),
                        depth=STREAM_DEPTH)
        sel = moba_decode_select(page_table, qmb.reshape(bs, D_ATT, 1), mbk, l,
                                 group=min(MOBA_SELECT_GROUP, n_pages), depth=STREAM_DEPTH)
        sel = sel[:, :MOBA_TOPK, :N_HEADS].reshape(bs, MOBA_TOPK * N_HEADS)
        ymb = moba_decode(page_table, sel, heads(qmbh), heads(kmb), heads(vmb), bias_d,
                          heads(gact[:, D_ATT:2 * D_ATT]), mbk, mbv, l)
        yml, c_s, n_s, m_s = mlstm_decode(tok(qml), tok(kml), tok(vml), tok(gates), gact3, mlw,
                                          state_mlstm_C[l], state_mlstm_n[l], state_mlstm_m[l].reshape(bs, 1, ML_HEADS))
        xs = output_projection(ysb.reshape(bs, D_ATT), ymb.reshape(bs, D_ATT), yml.reshape(bs, D_ML), xs, sm(gate),
                               wo_bf, tm=bs, rows_per_mod=1)
        new = lambda a: a.reshape(bs, 1, N_HEADS, HEAD_DIM)
        for lst, a in zip(s_out, (new(ksb), new(vsb), new(kmb), new(vmb), c_s, n_s, m_s.reshape(bs, ML_HEADS))):
            lst.append(a)

    paged = lambda a: jnp.transpose(a.reshape(depth, bp, tp // PAGE, N_HEADS, HEAD_DIM, PAGE), (0, 1, 2, 5, 3, 4))
    P = [paged(a) for a in kv_pages] + [jnp.stack(a) for a in p_out]
    S = [jnp.stack(a) for a in s_out]
    return (xp.reshape(bp, tp, d), xs.reshape(bs, 1, d),
            P[0], P[1], P[2], P[3], P[4], P[5], P[6], S[0], S[1], S[2], S[3], S[4], S[5], S[6])
```

```python
import functools
import math

import numpy as np
import jax
import jax.numpy as jnp
from jax import lax
from jax.experimental import pallas as pl
from jax.experimental.pallas import tpu as pltpu

F32 = jnp.float32
BF16 = jnp.bfloat16
I32 = jnp.int32
HI = lax.Precision.HIGHEST

EPS = 1e-6
HEAD_DIM = 64
N_HEADS = 4
D_ATT = N_HEADS * HEAD_DIM
ML_HEADS = 4
ML_D = 128
D_ML = ML_HEADS * ML_D
Q_BLOCK = 128
KV_TILE = 256
MOBA_TOPK = 3
N_BUCKETS = 32
MAX_DISTANCE = 128
ML_CHUNK = 128
PAGE = 128
HEAD_SHIFT = 6
NEG_INF = float("-inf")

VMEM_LIMIT = 56 * 1024 * 1024

_SIZES = [D_ATT] * 8 + [D_ML] * 5 + [ML_HEADS] * 2
_OFF = np.concatenate([[0], np.cumsum(_SIZES)]).tolist()
(O_QSB, O_KSB, O_VSB, O_GSB, O_QMB, O_KMB, O_VMB, O_GMB,
 O_QML, O_KML, O_VML, O_OML, O_GML, O_IML, O_FML, O_END) = _OFF


def _params(sem, vmem=VMEM_LIMIT):
    return pltpu.CompilerParams(dimension_semantics=sem, vmem_limit_bytes=vmem)


def _dot(a, b):
    return jnp.dot(a, b, preferred_element_type=F32)


def _dot_nt(a, b, precision=None):
    return lax.dot_general(a, b, (((1,), (1,)), ((), ())), precision=precision,
                           preferred_element_type=F32)


SOFTPLUS_CLAMP = 64.0


def _softplus(z):
    return jnp.maximum(jnp.log(1.0 + jnp.exp(jnp.minimum(z, SOFTPLUS_CLAMP))), z)


def _head_of_lane(shape, axis):
    return lax.shift_right_logical(lax.broadcasted_iota(I32, shape, axis), HEAD_SHIFT)


def _ada_kernel(c_ref, w_ref, b_ref, o_ref):
    c = c_ref[...]
    a = c * jax.nn.sigmoid(c)
    o_ref[...] = jnp.dot(a, w_ref[...], precision=HI, preferred_element_type=F32) + b_ref[...]


def ada_modulation(c_all, w_ada, b_ada):
    depth, d, d3 = w_ada.shape
    rows = c_all.shape[0]
    nt = d3 // d
    return pl.pallas_call(
        _ada_kernel,
        out_shape=jax.ShapeDtypeStruct((depth, rows, d3), F32),
        grid=(depth, nt),
        in_specs=[pl.BlockSpec((rows, d), lambda l, j: (0, 0)),
                  pl.BlockSpec((None, d, d), lambda l, j: (l, 0, j)),
                  pl.BlockSpec((None, 1, d), lambda l, j: (l, 0, j))],
        out_specs=pl.BlockSpec((None, rows, d), lambda l, j: (l, 0, j)),
        compiler_params=_params(("parallel", "parallel")),
        name="ada_modulation",
    )(c_all, w_ada, b_ada.reshape(depth, 1, d3))


def _bucket_thresholds():
    d = np.arange(0, 4 * MAX_DISTANCE, dtype=np.int32)
    max_exact = N_BUCKETS // 2
    df = np.maximum(d, 1).astype(np.float32)
    large = max_exact + (np.log(df / np.float32(max_exact)) / np.float32(math.log(MAX_DISTANCE / max_exact))
                         * np.float32(N_BUCKETS - max_exact)).astype(np.int32)
    large = np.minimum(large, N_BUCKETS - 1)
    bucket = np.where(d < max_exact, d, large)
    thr = []
    for k in range(1, N_BUCKETS):
        idx = np.nonzero(bucket >= k)[0]
        thr.append(int(idx[0]))
    assert all(np.all((bucket >= k) == (d >= t)) for k, t in zip(range(1, N_BUCKETS), thr))
    return thr


_BUCKET_THR = _bucket_thresholds()

_PATTERN_OFFSETS = (0, Q_BLOCK, KV_TILE)
P_FAR = len(_PATTERN_OFFSETS)


def _bias_from_dist(dist, rb_ref, h):
    b = jnp.full(dist.shape, rb_ref[0, h], F32)
    for k, t in zip(range(1, N_BUCKETS), _BUCKET_THR):
        b = jnp.where(dist >= t, rb_ref[k, h], b)
    return jnp.where(dist >= 0, b, NEG_INF)


def _bias_kernel(rb_ref, tile_ref, dec_ref):
    j = lax.broadcasted_iota(I32, (KV_TILE, Q_BLOCK), 0)
    i = lax.broadcasted_iota(I32, (KV_TILE, Q_BLOCK), 1)
    for p, off in enumerate(_PATTERN_OFFSETS):
        for h in range(N_HEADS):
            tile_ref[p, :, h * Q_BLOCK:(h + 1) * Q_BLOCK] = _bias_from_dist(off + i - j, rb_ref, h)
    for h in range(N_HEADS):
        tile_ref[P_FAR, :, h * Q_BLOCK:(h + 1) * Q_BLOCK] = jnp.full((KV_TILE, Q_BLOCK), rb_ref[N_BUCKETS - 1, h], F32)
    jr = lax.broadcasted_iota(I32, (8, KV_TILE), 1)
    for h in range(N_HEADS):
        dec_ref[h, 0] = jnp.full((8, KV_TILE), rb_ref[N_BUCKETS - 1, h], F32)
        dec_ref[h, 1] = _bias_from_dist(KV_TILE - jr, rb_ref, h)
        dec_ref[h, 2] = jnp.full((8, KV_TILE), rb_ref[0, h], F32)


def bias_tiles(rel_bias):
    return pl.pallas_call(
        _bias_kernel,
        out_shape=(jax.ShapeDtypeStruct((P_FAR + 1, KV_TILE, N_HEADS * Q_BLOCK), F32),
                   jax.ShapeDtypeStruct((N_HEADS, 3, 8, KV_TILE), F32)),
        in_specs=[pl.BlockSpec(memory_space=pltpu.SMEM)],
        name="bias_tiles",
    )(rel_bias)


def _inproj_kernel(x_ref, nw_ref, sc_ref, sh_ref, w_ref, wlo_ref, bg_ref, qnw_ref, knw_ref, *outs, prompt, layer,
                   n_carried):
    outs = list(outs[n_carried:])

    def paged(p):
        return [p[pg * PAGE:(pg + 1) * PAGE].T for pg in range(p.shape[0] // PAGE)]

    def emit(p, *, rows_bf16=False, rows_f32=False, pages_f32=False, pages_bf16=False):
        pages = paged(p) if (pages_f32 or pages_bf16) else None
        for want, dtype, is_pages in ((pages_f32, F32, True), (rows_f32, F32, False),
                                      (pages_bf16, BF16, True), (rows_bf16, BF16, False)):
            if not want:
                continue
            ref = outs.pop(0)
            if is_pages and len(ref.shape) == 4:
                for slab in range(ref.shape[0]):
                    for pg, t in enumerate(pages):
                        ref[slab, pg] = t.astype(dtype) if slab == layer else jnp.zeros_like(t, dtype)
            elif is_pages:
                for pg, t in enumerate(pages):
                    ref[pg] = t.astype(dtype)
            else:
                ref[...] = p.astype(dtype)

    x = x_ref[...]
    ms = jnp.mean(x * x, axis=-1, keepdims=True)
    h = (x * lax.rsqrt(ms + EPS) * nw_ref[...]) * (1.0 + sc_ref[...]) + sh_ref[...]
    hb = h.astype(BF16)
    hlo = (h - hb.astype(F32)).astype(BF16)

    def proj(a, b):
        return _dot(hb, w_ref[:, a:b])

    def proj3(a, b, la, lb):
        wh = w_ref[:, a:b]
        return _dot(hb, wh) + (_dot(hlo, wh) + _dot(hb, wlo_ref[:, la:lb]))

    def silu(g):
        return g * jax.nn.sigmoid(g)

    r = lax.broadcasted_iota(I32, (D_ATT, D_ATT), 0)
    c = lax.broadcasted_iota(I32, (D_ATT, D_ATT), 1)
    same_head = lax.shift_right_logical(r, HEAD_SHIFT) == lax.shift_right_logical(c, HEAD_SHIFT)
    head_mean = jnp.where(same_head, 1.0 / HEAD_DIM, 0.0).astype(BF16)

    def head_norm(p, w):
        sq = p * p
        hi = sq.astype(BF16)
        lo = (sq - hi.astype(F32)).astype(BF16)
        ms = _dot(hi, head_mean) + _dot(lo, head_mean)
        return p * lax.rsqrt(ms + EPS) * w

    score_scale = HEAD_DIM ** -0.5
    emit(proj(O_QSB, O_KSB) * score_scale, pages_bf16=prompt, rows_bf16=not prompt)
    emit(proj(O_KSB, O_VSB), pages_f32=prompt, rows_bf16=prompt, rows_f32=not prompt)
    emit(proj(O_VSB, O_GSB), pages_f32=prompt, pages_bf16=prompt, rows_f32=not prompt)
    g_sb = silu(proj(O_GSB, O_QMB))
    qn = head_norm(proj3(O_QMB, O_KMB, 0, D_ATT), qnw_ref[...])
    if prompt:
        qt = paged(qn)
        ref_f, ref_h = outs.pop(0), outs.pop(0)
        for pg, t in enumerate(qt):
            ref_f[pg] = t
            ref_h[pg] = (t * score_scale).astype(BF16)
    else:
        emit(qn, rows_f32=True)
        emit(qn * score_scale, rows_bf16=True)
    kn = head_norm(proj3(O_KMB, O_VMB, D_ATT, 2 * D_ATT), knw_ref[...])
    emit(kn, pages_f32=prompt, rows_bf16=prompt, rows_f32=not prompt)
    emit(proj(O_VMB, O_GMB), pages_f32=prompt, pages_bf16=prompt, rows_f32=not prompt)
    g_mb = silu(proj(O_GMB, O_QML))
    emit(proj(O_QML, O_KML), pages_bf16=prompt, rows_bf16=not prompt)
    emit(proj(O_KML, O_VML) * (ML_D ** -0.5), rows_bf16=True)
    emit(proj(O_VML, O_OML), pages_bf16=prompt, rows_bf16=not prompt)
    o = proj(O_OML, O_GML)
    g_ml = jax.nn.sigmoid(o) * silu(proj(O_GML, O_IML))
    gact_ref = outs.pop(0)
    gact_ref[:, 0:D_ATT] = g_sb.astype(BF16)
    gact_ref[:, D_ATT:2 * D_ATT] = g_mb.astype(BF16)
    gact_ref[:, 2 * D_ATT:] = g_ml.astype(BF16)
    pre = proj(O_IML, O_END) + bg_ref[...]
    is_f = lax.broadcasted_iota(I32, pre.shape, 1) >= ML_HEADS
    logf = jnp.minimum(pre, 0.0) - jnp.log1p(jnp.exp(-jnp.abs(pre)))
    outs.pop(0)[...] = jnp.where(is_f, logf, pre)
    if prompt:
        kmean_ref = outs.pop(0)
        for i in range(kn.shape[0] // KV_TILE):
            kmean_ref[i] = jnp.mean(kn[i * KV_TILE:(i + 1) * KV_TILE], axis=0, keepdims=True)
    assert not outs


def input_projection(x, norm_w, scale, shift, w_bf, w_lo, b_gates, qnw, knw, *, tm, rows_per_mod, prompt,
                     layer=0, depth=1, kv_carried=()):
    m, d = x.shape
    n_tiles = m // tm
    rmod = scale.shape[1]
    row = lambda i: (i, 0)
    const = lambda i: (0, 0)
    mod = lambda i: (i // rows_per_mod, 0, 0)

    def rows(width, dtype):
        return jax.ShapeDtypeStruct((m, width), dtype), pl.BlockSpec((tm, width), row)

    def pages(dtype, width=D_ATT):
        return (jax.ShapeDtypeStruct((m // PAGE, width, PAGE), dtype),
                pl.BlockSpec((tm // PAGE, width, PAGE), lambda i: (i, 0, 0)))

    def kv_pages():
        shape = jax.ShapeDtypeStruct((depth, m // PAGE, D_ATT, PAGE), F32)
        if kv_carried:
            return shape, pl.BlockSpec((None, tm // PAGE, D_ATT, PAGE), lambda i: (layer, i, 0, 0))
        return shape, pl.BlockSpec((depth, tm // PAGE, D_ATT, PAGE), lambda i: (0, i, 0, 0))

    kv_out_index = (1, 3, 7, 9)
    if prompt:
        nb = tm // KV_TILE
        outs = [pages(BF16), kv_pages(), rows(D_ATT, BF16), kv_pages(), pages(BF16),
                pages(F32), pages(BF16), kv_pages(), rows(D_ATT, BF16), kv_pages(), pages(BF16)]
        tail = [(jax.ShapeDtypeStruct((n_tiles, nb, 1, D_ATT), F32),
                 pl.BlockSpec((None, nb, 1, D_ATT), lambda i: (i, 0, 0, 0)))]
    else:
        outs = [rows(D_ATT, BF16), rows(D_ATT, F32), rows(D_ATT, F32),
                rows(D_ATT, F32), rows(D_ATT, BF16), rows(D_ATT, F32), rows(D_ATT, F32)]
        tail = []
    ml_qv = pages(BF16, D_ML) if prompt else rows(D_ML, BF16)
    outs += [ml_qv, rows(D_ML, BF16), ml_qv,
             rows(2 * D_ATT + D_ML, BF16), rows(2 * ML_HEADS, F32)] + tail
    n_in = 9
    return pl.pallas_call(
        functools.partial(_inproj_kernel, prompt=prompt, layer=layer, n_carried=len(kv_carried)),
        out_shape=tuple(o[0] for o in outs),
        input_output_aliases={n_in + j: kv_out_index[j] for j in range(len(kv_carried))},
        grid=(n_tiles,),
        in_specs=[pl.BlockSpec((tm, d), row),
                  pl.BlockSpec((1, d), const),
                  pl.BlockSpec((None, rmod, d), mod),
                  pl.BlockSpec((None, rmod, d), mod),
                  pl.BlockSpec((None,) + w_bf.shape[1:], lambda i: (layer, 0, 0)),
                  pl.BlockSpec((None,) + w_lo.shape[1:], lambda i: (layer, 0, 0)),
                  pl.BlockSpec((1, 2 * ML_HEADS), const),
                  pl.BlockSpec((1, D_ATT), const),
                  pl.BlockSpec((1, D_ATT), const)] + [pl.BlockSpec(memory_space=pl.ANY)] * len(kv_carried),
        out_specs=tuple(o[1] for o in outs),
        compiler_params=_params(("parallel",)),
        name="input_projection",
    )(x, norm_w, scale, shift, w_bf, w_lo, b_gates, qnw, knw, *kv_carried)


def _outproj_kernel(ysb_ref, ymb_ref, yml_ref, x_ref, gate_ref, w_ref, o_ref):
    y = (_dot(ysb_ref[...], w_ref[0:D_ATT, :]) + _dot(ymb_ref[...], w_ref[D_ATT:2 * D_ATT, :])
         + _dot(yml_ref[...], w_ref[2 * D_ATT:, :]))
    o_ref[...] = x_ref[...] + gate_ref[...] * y


def output_projection(ysb, ymb, yml, x, gate, w_bf, *, tm, rows_per_mod, layer):
    m, d = x.shape
    rmod = gate.shape[1]
    row = lambda i: (i, 0)
    return pl.pallas_call(
        _outproj_kernel,
        out_shape=jax.ShapeDtypeStruct((m, d), F32),
        grid=(m // tm,),
        in_specs=[pl.BlockSpec((tm, D_ATT), row), pl.BlockSpec((tm, D_ATT), row), pl.BlockSpec((tm, D_ML), row),
                  pl.BlockSpec((tm, d), row),
                  pl.BlockSpec((None, rmod, d), lambda i: (i // rows_per_mod, 0, 0)),
                  pl.BlockSpec((None,) + w_bf.shape[1:], lambda i: (layer, 0, 0))],
        out_specs=pl.BlockSpec((tm, d), row),
        compiler_params=_params(("parallel",)),
        name="output_projection",
    )(ysb, ymb, yml, x, gate, w_bf)


def _suffix_matrix(n, transposed=False):
    r = lax.broadcasted_iota(I32, (n, n), 0)
    c = lax.broadcasted_iota(I32, (n, n), 1)
    return jnp.where((c > r) if transposed else (r > c), 1.0, 0.0).astype(BF16)


def _heads_on_lanes(qt):
    head = _head_of_lane((D_ATT, 1), 0)
    return jnp.concatenate([jnp.where(head == h, qt, jnp.zeros_like(qt)) for h in range(N_HEADS)], axis=1)


def _heads_from_lanes(acc_t, q):
    head = _head_of_lane((D_ATT, 1), 0)
    out = jnp.zeros((D_ATT, q), F32)
    for h in range(N_HEADS):
        out = out + jnp.where(head == h, acc_t[:, h * q:(h + 1) * q], 0.0)
    return out.T


def _kv_tile(k_ref, vt_ref, n):
    start = pl.multiple_of(n * KV_TILE, KV_TILE)
    vt = jnp.concatenate([vt_ref[2 * n], vt_ref[2 * n + 1]], axis=1)
    return k_ref[pl.ds(start, KV_TILE), :], vt


def _sb_tile_t(q4t, k, vt, upper, carry, mask):
    z = _dot(k, q4t)
    sp = _softplus(z)
    if mask is not None:
        sp_sum = jnp.where(mask, sp, 0.0)
    else:
        sp_sum = sp
    later = _dot(upper, sp_sum.astype(BF16))
    w = jnp.exp(((z - sp) - later) - carry)
    if mask is not None:
        w = jnp.where(mask, w, 0.0)
    return _dot(vt, w.astype(BF16)), carry + jnp.sum(sp_sum, axis=0, keepdims=True)


def _sb_pair_t(q4t, tile_a, tile_b, upper, carry, mask_a):
    (ka, vta), (kb, vtb) = tile_a, tile_b
    za, zb = _dot(ka, q4t), _dot(kb, q4t)
    spa, spb = _softplus(za), _softplus(zb)
    sum_a = spa if mask_a is None else jnp.where(mask_a, spa, 0.0)
    la, lb = _dot(upper, sum_a.astype(BF16)), _dot(upper, spb.astype(BF16))
    carry_b = carry + jnp.sum(sum_a, axis=0, keepdims=True)
    wa = jnp.exp(((za - spa) - la) - carry)
    if mask_a is not None:
        wa = jnp.where(mask_a, wa, 0.0)
    wb = jnp.exp(((zb - spb) - lb) - carry_b)
    pv = _dot(vta, wa.astype(BF16)) + _dot(vtb, wb.astype(BF16))
    return pv, carry_b + jnp.sum(spb, axis=0, keepdims=True)


def _sb_prompt_kernel(qt_ref, k_ref, vt_ref, g_ref, o_ref, acc_ref, carry_ref):
    i = pl.program_id(1)
    cols = N_HEADS * Q_BLOCK
    q4t = _heads_on_lanes(qt_ref[...])
    upper = _suffix_matrix(KV_TILE, transposed=True)
    last = (i * Q_BLOCK) // KV_TILE
    kpos = last * KV_TILE + lax.broadcasted_iota(I32, (KV_TILE, cols), 0)
    qpos = i * Q_BLOCK + (lax.broadcasted_iota(I32, (KV_TILE, cols), 1) & (Q_BLOCK - 1))
    mask = kpos < qpos
    zero = jnp.zeros((1, cols), F32)

    @pl.when(last % 2 == 0)
    def _():
        k, vt = _kv_tile(k_ref, vt_ref, last)
        acc_ref[...], carry_ref[...] = _sb_tile_t(q4t, k, vt, upper, zero, mask)

    @pl.when(last % 2 == 1)
    def _():
        acc_ref[...], carry_ref[...] = _sb_pair_t(q4t, _kv_tile(k_ref, vt_ref, last), _kv_tile(k_ref, vt_ref, last - 1),
                                                  upper, zero, mask)

    first = last - 1 - last % 2

    def body(s, carry):
        pv, carry = _sb_pair_t(q4t, _kv_tile(k_ref, vt_ref, first - 2 * s), _kv_tile(k_ref, vt_ref, first - 2 * s - 1),
                               upper, carry, None)
        acc_ref[...] += pv
        return carry

    lax.fori_loop(0, last // 2, body, carry_ref[...])
    o_ref[...] = (_heads_from_lanes(acc_ref[...], Q_BLOCK) * g_ref[...].astype(F32)).astype(BF16)


def sb_prompt(qt, k, vt, gact):
    b, t, _ = k.shape
    n_pages = t // PAGE
    blk = pl.BlockSpec((None, Q_BLOCK, D_ATT), lambda bi, i: (bi, i, 0))
    return pl.pallas_call(
        _sb_prompt_kernel,
        out_shape=jax.ShapeDtypeStruct((b, t, D_ATT), BF16),
        grid=(b, t // Q_BLOCK),
        in_specs=[pl.BlockSpec((None, None, D_ATT, PAGE), lambda bi, i: (bi, i, 0, 0)),
                  pl.BlockSpec((None, t, D_ATT), lambda bi, i: (bi, 0, 0)),
                  pl.BlockSpec((None, n_pages, D_ATT, PAGE), lambda bi, i: (bi, 0, 0, 0)),
                  blk],
        out_specs=blk,
        scratch_shapes=[pltpu.VMEM((D_ATT, N_HEADS * Q_BLOCK), F32), pltpu.VMEM((1, N_HEADS * Q_BLOCK), F32)],
        compiler_params=_params(("parallel", "arbitrary")),
        name="sb_prompt",
    )(qt, k, vt, gact)


def _moba_select_t(gate, valid, idx):
    n = gate.shape[0]
    gate = jnp.where(valid, gate, NEG_INF)
    rank = jnp.zeros(gate.shape, I32)
    for m in range(n):
        gm = gate[m:m + 1, :]
        beats = (gm > gate) | ((gm == gate) & (idx > m))
        rank = rank + beats.astype(I32)
    return valid & (rank < MOBA_TOPK)


def _moba_prompt_kernel(qft_ref, qht_ref, k_ref, vt_ref, km_ref, bias_ref, g_ref, o_ref, acc_ref, m_ref, l_ref, sel_ref):
    i = pl.program_id(1)
    nb = km_ref.shape[0]
    own = (i * Q_BLOCK) // KV_TILE
    odd = (i * Q_BLOCK) % KV_TILE != 0
    q4t = _heads_on_lanes(qht_ref[...])
    gate = jnp.dot(km_ref[...], _heads_on_lanes(qft_ref[...]), precision=HI, preferred_element_type=F32)
    idx = lax.broadcasted_iota(I32, (nb, 1), 0)
    sel = _moba_select_t(gate, idx < own, idx)
    sel_ref[...] = jnp.where(sel, 0.0, NEG_INF)

    def attend(blocks, first=False):
        tiles = [_kv_tile(k_ref, vt_ref, n) for n, _ in blocks]
        scores = []
        for (k, _), (_, terms) in zip(tiles, blocks):
            s = _dot(k, q4t)
            for t in terms:
                s = s + t
            scores.append(s)
        m_new = functools.reduce(jnp.maximum, [jnp.max(s, axis=0, keepdims=True) for s in scores])
        if not first:
            m_old = m_ref[...]
            m_new = jnp.maximum(m_old, m_new)
            alpha = jnp.exp(m_old - m_new)
        ps = [jnp.exp(s - m_new) for s in scores]
        l_new = functools.reduce(jnp.add, [jnp.sum(p, axis=0, keepdims=True) for p in ps])
        pv = functools.reduce(jnp.add, [_dot(vt, p.astype(BF16)) for (_, vt), p in zip(tiles, ps)])
        m_ref[...] = m_new
        l_ref[...] = l_new if first else alpha * l_ref[...] + l_new
        acc_ref[...] = pv if first else alpha * acc_ref[...] + pv

    attend([(own, [bias_ref[jnp.where(odd, 1, 0)]])], first=True)
    near_pat = jnp.where(odd, P_FAR, 2)
    far_row = bias_ref[P_FAR, 0:1, :]
    for n in range(0, nb - 2, 2):
        @pl.when(n + 1 < own)
        def _():
            attend([(n, [far_row + sel_ref[n:n + 1, :]]),
                    (n + 1, [bias_ref[jnp.where(n + 1 == own - 1, near_pat, P_FAR)], sel_ref[n + 1:n + 2, :]])])

    @pl.when(own % 2 == 1)
    def _():
        attend([(own - 1, [bias_ref[near_pat], sel_ref[pl.ds(own - 1, 1), :]])])

    y = _heads_from_lanes(acc_ref[...] / l_ref[...], Q_BLOCK)
    o_ref[...] = (y * g_ref[...].astype(F32)).astype(BF16)


def moba_prompt(qft, qht, k, vt, kmean, bias, gact):
    b, t, _ = k.shape
    nb = t // KV_TILE
    cols = N_HEADS * Q_BLOCK
    blk = pl.BlockSpec((None, Q_BLOCK, D_ATT), lambda bi, i: (bi, i, 0))
    qpage = pl.BlockSpec((None, None, D_ATT, PAGE), lambda bi, i: (bi, i, 0, 0))
    return pl.pallas_call(
        _moba_prompt_kernel,
        out_shape=jax.ShapeDtypeStruct((b, t, D_ATT), BF16),
        grid=(b, t // Q_BLOCK),
        in_specs=[qpage, qpage,
                  pl.BlockSpec((None, t, D_ATT), lambda bi, i: (bi, 0, 0)),
                  pl.BlockSpec((None, t // PAGE, D_ATT, PAGE), lambda bi, i: (bi, 0, 0, 0)),
                  pl.BlockSpec((None, nb, D_ATT), lambda bi, i: (bi, 0, 0)),
                  pl.BlockSpec(bias.shape, lambda bi, i: (0, 0, 0)),
                  pl.BlockSpec((None, Q_BLOCK, D_ATT), lambda bi, i: (bi, i, 1))],
        out_specs=blk,
        scratch_shapes=[pltpu.VMEM((D_ATT, cols), F32), pltpu.VMEM((1, cols), F32), pltpu.VMEM((1, cols), F32),
                        pltpu.VMEM((nb, cols), F32)],
        compiler_params=_params(("parallel", "arbitrary")),
        name="moba_prompt",
    )(qft, qht, k, vt, kmean, bias, gact)


def _gated_head_norm(h, w, g):
    hn = h * lax.rsqrt(jnp.mean(h * h, axis=-1, keepdims=True) + EPS) * w
    return (hn * g.astype(F32)).astype(BF16)


def _mlstm_prompt_kernel(qt_ref, k_ref, vt_ref, gc_ref, gr_ref, g_ref, nw_ref, h_ref, c_ref, n_ref, m_ref,
                         cst_ref, nst_ref, ms_ref):
    ci = pl.program_id(1)
    L = ML_CHUNK

    @pl.when(ci == 0)
    def _():
        cst_ref[...] = jnp.zeros_like(cst_ref)
        nst_ref[...] = jnp.zeros_like(nst_ref)
        ms_ref[...] = jnp.zeros_like(ms_ref)

    r = lax.broadcasted_iota(I32, (L, L), 0)
    c = lax.broadcasted_iota(I32, (L, L), 1)
    lower = jnp.where(c <= r, 1.0, 0.0).astype(F32)
    upper = jnp.where(r <= c, 1.0, 0.0).astype(F32)
    src_before_tgt = r <= c
    gc = gc_ref[...]
    gr = gr_ref[...]
    bh_c = jnp.dot(lower, gc, precision=HI, preferred_element_type=F32)
    bh_r = jnp.dot(gr, upper, precision=HI, preferred_element_type=F32)

    for h in range(ML_HEADS):
        sl = slice(h * ML_D, (h + 1) * ML_D)
        qt, k, vt = qt_ref[sl, :], k_ref[:, sl], vt_ref[sl, :]
        m_prev = ms_ref[h][0:1, 0:1]
        b_r = bh_r[ML_HEADS + h:ML_HEADS + h + 1, :]
        a_r = gr[h:h + 1, :] - b_r
        a_c = gc[:, h:h + 1] - bh_c[:, ML_HEADS + h:ML_HEADS + h + 1]
        dm = jnp.where(src_before_tgt, a_c + b_r, NEG_INF)
        g = b_r + m_prev
        m_row = jnp.maximum(g, jnp.max(dm, axis=0, keepdims=True))
        w_inter = jnp.exp(g - m_row)
        sc = _dot(k, qt) * jnp.exp(dm - m_row)
        cst, nst = cst_ref[h], nst_ref[h]
        num = _dot(vt, sc.astype(BF16)) + w_inter * _dot(cst.astype(BF16), qt)
        den = jnp.sum(sc, axis=0, keepdims=True) + w_inter * _dot(nst.astype(BF16), qt)[0:1]
        hout = (num / jnp.maximum(jnp.abs(den), jnp.exp(-m_row))).T
        h_ref[:, sl] = _gated_head_norm(hout, nw_ref[:, sl], g_ref[:, sl])

        b_last = b_r[:, L - 1:L]
        m_new = jnp.maximum(b_last + m_prev, jnp.max(b_last + a_r, axis=-1, keepdims=True))
        ws = jnp.exp(b_last + a_r - m_new)
        wc = jnp.exp(b_last + m_prev - m_new)
        cst_ref[h] = wc * cst + _dot((vt.astype(F32) * ws).astype(BF16), k)
        nst_ref[h] = wc * nst + _dot(jnp.broadcast_to(ws, (8, L)).astype(BF16), k)
        ms_ref[h] = jnp.broadcast_to(m_new, ms_ref.shape[1:])

    @pl.when(ci == pl.num_programs(1) - 1)
    def _():
        for h in range(ML_HEADS):
            c_ref[h] = cst_ref[h]
            n_ref[h:h + 1, :] = nst_ref[h][0:1]
            m_ref[:, h:h + 1] = ms_ref[h][0:1, 0:1]


def mlstm_prompt(qt, k, vt, gates, gates_t, gact, norm_w):
    b, t, _ = k.shape
    blk = pl.BlockSpec((None, ML_CHUNK, D_ML), lambda bi, c: (bi, c, 0))
    page = pl.BlockSpec((None, None, D_ML, ML_CHUNK), lambda bi, c: (bi, c, 0, 0))
    return pl.pallas_call(
        _mlstm_prompt_kernel,
        out_shape=(jax.ShapeDtypeStruct((b, t, D_ML), BF16),
                   jax.ShapeDtypeStruct((b, ML_HEADS, ML_D, ML_D), F32),
                   jax.ShapeDtypeStruct((b, ML_HEADS, ML_D), F32),
                   jax.ShapeDtypeStruct((b, 1, ML_HEADS), F32)),
        grid=(b, t // ML_CHUNK),
        in_specs=[page, blk, page,
                  pl.BlockSpec((None, ML_CHUNK, 2 * ML_HEADS), lambda bi, c: (bi, c, 0)),
                  pl.BlockSpec((None, 2 * ML_HEADS, ML_CHUNK), lambda bi, c: (bi, 0, c)),
                  pl.BlockSpec((None, ML_CHUNK, D_ML), lambda bi, c: (bi, c, 1)),
                  pl.BlockSpec((1, D_ML), lambda bi, c: (0, 0))],
        out_specs=(blk,
                   pl.BlockSpec((None, ML_HEADS, ML_D, ML_D), lambda bi, c: (bi, 0, 0, 0)),
                   pl.BlockSpec((None, ML_HEADS, ML_D), lambda bi, c: (bi, 0, 0)),
                   pl.BlockSpec((None, 1, ML_HEADS), lambda bi, c: (bi, 0, 0))),
        scratch_shapes=[pltpu.VMEM((ML_HEADS, ML_D, ML_D), F32), pltpu.VMEM((ML_HEADS, 8, ML_D), F32),
                        pltpu.VMEM((ML_HEADS, 8, 128), F32)],
        compiler_params=_params(("parallel", "arbitrary")),
        name="mlstm_prompt",
    )(qt, k, vt, gates, gates_t, gact, norm_w)


def _page_stream(pt_ref, hbm_refs, bufs, sems, *, layer, group, depth, reverse, consume, init):
    n_seq, n_pages = pt_ref.shape
    per_seq = n_pages // group
    total = n_seq * per_seq

    def copies(g, slot):
        seq, c = g // per_seq, g % per_seq
        out = []
        for r in range(group):
            walk = c * group + r
            page = pt_ref[seq, n_pages - 1 - walk if reverse else walk]
            for hbm, buf, sem in zip(hbm_refs, bufs, sems):
                out.append(pltpu.make_async_copy(hbm.at[layer, page], buf.at[slot, r], sem.at[slot]))
        return out

    for g in range(depth - 1):
        for cp in copies(g, g):
            cp.start()

    def body(g, carry):
        slot = lax.rem(g, depth)
        ahead = g + depth - 1

        @pl.when(ahead < total)
        def _():
            for cp in copies(ahead, lax.rem(ahead, depth)):
                cp.start()

        for cp in copies(g, slot):
            cp.wait()
        return consume(g // per_seq, g % per_seq, slot, carry)

    return lax.fori_loop(0, total, body, init)


def _sb_decode_kernel(pt_ref, q_ref, g_ref, k_hbm, v_hbm, o_ref, kbuf, vbuf, ksem, vsem, *, layer, group, depth):
    tiles = group // 2
    last_chunk = pt_ref.shape[1] // group - 1
    rowi = lax.broadcasted_iota(I32, (8, D_ATT), 0)
    own_head = _head_of_lane((8, D_ATT), 1) == rowi
    suffix = _suffix_matrix(KV_TILE)

    def consume(seq, c, slot, state):
        fresh = c == 0
        acc = jnp.where(fresh, 0.0, state[0])
        carry = jnp.where(fresh, 0.0, state[1])
        qm = jnp.where(own_head, jnp.broadcast_to(q_ref[seq].astype(F32), (8, D_ATT)), 0.0).astype(BF16)

        def tile(buf, t):
            lo, hi = buf[slot, 2 * t + 1].reshape(D_ATT, PAGE), buf[slot, 2 * t].reshape(D_ATT, PAGE)
            return jnp.concatenate([lo, hi], axis=1).astype(BF16)

        z = jnp.concatenate([_dot(qm, tile(kbuf, t)) for t in range(tiles)], axis=0)
        sp = _softplus(z)
        lf = -sp
        hi = lf.astype(BF16).astype(F32)
        both = _dot(jnp.concatenate([hi, lf - hi], axis=0).astype(BF16), suffix)
        later = both[:8 * tiles] + both[8 * tiles:]
        tot = jnp.sum(lf, axis=-1, keepdims=True)
        carries = [carry]
        for t in range(tiles):
            carries.append(carries[-1] + tot[8 * t:8 * (t + 1)])
        w = jnp.exp((z - sp) + later + jnp.concatenate(carries[:tiles], axis=0))
        for t in range(tiles):
            acc = acc + _dot_nt(w[8 * t:8 * (t + 1)].astype(BF16), tile(vbuf, t))

        @pl.when(c == last_chunk)
        def _():
            y = jnp.sum(jnp.where(own_head, acc, 0.0), axis=0, keepdims=True)
            o_ref[seq] = (y * g_ref[seq].astype(F32)).astype(BF16)

        return acc, carries[tiles]

    _page_stream(pt_ref, (k_hbm, v_hbm), (kbuf, vbuf), (ksem, vsem), layer=layer, group=group, depth=depth,
                 reverse=True, consume=consume, init=(jnp.zeros((8, D_ATT), F32), jnp.zeros((8, 1), F32)))


def _page_buffers(depth, group):
    return pltpu.VMEM((depth, group, N_HEADS, HEAD_DIM, PAGE), F32)


def sb_decode(page_table, q, gact, cache_k, cache_v, layer, *, group, depth):
    b = page_table.shape[0]
    rows = pl.BlockSpec((b, 1, D_ATT), lambda i, pt: (0, 0, 0))
    hbm = pl.BlockSpec(memory_space=pl.ANY)
    grid_spec = pltpu.PrefetchScalarGridSpec(
        num_scalar_prefetch=1,
        grid=(1,),
        in_specs=[rows, rows, hbm, hbm],
        out_specs=rows,
        scratch_shapes=[_page_buffers(depth, group), _page_buffers(depth, group),
                        pltpu.SemaphoreType.DMA((depth,)), pltpu.SemaphoreType.DMA((depth,))],
    )
    return pl.pallas_call(
        functools.partial(_sb_decode_kernel, layer=layer, group=group, depth=depth),
        out_shape=jax.ShapeDtypeStruct((b, 1, D_ATT), BF16),
        grid_spec=grid_spec,
        compiler_params=_params(("arbitrary",)),
        name="sb_decode",
    )(page_table, q, gact, cache_k, cache_v)


def _moba_gate_kernel(pt_ref, q_ref, k_hbm, idx_ref, kbuf, ksem, gs_ref, *, layer, group, depth):
    per_chunk = group // 2
    last_chunk = pt_ref.shape[1] // group - 1

    def consume(seq, c, slot, carry):
        qc = q_ref[seq]
        for r in range(per_chunk):
            pair = kbuf[slot, 2 * r].reshape(D_ATT, PAGE) + kbuf[slot, 2 * r + 1].reshape(D_ATT, PAGE)
            prod = pair * qc
            for h in range(N_HEADS):
                gs_ref[h, pl.ds(c * per_chunk + r, 1), :] = jnp.sum(prod[h * HEAD_DIM:(h + 1) * HEAD_DIM], axis=0,
                                                                    keepdims=True)

        @pl.when(c == last_chunk)
        def _():
            nb = gs_ref.shape[1]
            lane = lax.broadcasted_iota(I32, (nb, 128), 1)
            gate = jnp.zeros((nb, 128), F32)
            for h in range(N_HEADS):
                gate = gate + jnp.where(lane == h, jnp.sum(gs_ref[h], axis=-1, keepdims=True) * (1.0 / KV_TILE), 0.0)
            blk = lax.broadcasted_iota(I32, (nb, 128), 0)
            rank = jnp.zeros((nb, 128), I32)
            for m in range(nb):
                gm = gate[m:m + 1, :]
                rank = rank + ((gm > gate) | ((gm == gate) & (blk > m))).astype(I32)
            blk_f = blk.astype(F32)
            rows = [jnp.sum(jnp.where(rank == t, blk_f, 0.0), axis=0, keepdims=True) for t in range(MOBA_TOPK)]
            idx_ref[seq] = jnp.concatenate(rows + [jnp.zeros((8 - MOBA_TOPK, 128), F32)], axis=0).astype(I32)

        return carry

    _page_stream(pt_ref, (k_hbm,), (kbuf,), (ksem,), layer=layer, group=group, depth=depth, reverse=False,
                 consume=consume, init=0)


def moba_decode_select(page_table, q_col, cache_k, layer, *, group, depth):
    b, n_pages = page_table.shape
    nb = n_pages * PAGE // KV_TILE
    grid_spec = pltpu.PrefetchScalarGridSpec(
        num_scalar_prefetch=1,
        grid=(1,),
        in_specs=[pl.BlockSpec((b, D_ATT, 1), lambda i, pt: (0, 0, 0)), pl.BlockSpec(memory_space=pl.ANY)],
        out_specs=pl.BlockSpec((b, 8, 128), lambda i, pt: (0, 0, 0)),
        scratch_shapes=[_page_buffers(depth, group), pltpu.SemaphoreType.DMA((depth,)),
                        pltpu.VMEM((N_HEADS, nb, PAGE), F32)],
    )
    return pl.pallas_call(
        functools.partial(_moba_gate_kernel, layer=layer, group=group, depth=depth),
        out_shape=jax.ShapeDtypeStruct((b, 8, 128), I32),
        grid_spec=grid_spec,
        compiler_params=_params(("arbitrary",)),
        name="moba_decode_select",
    )(page_table, q_col, cache_k)


def _moba_decode_kernel(pt_ref, sel_ref, q_ref, kn_ref, vn_ref, bias_ref, g_ref, k_hbm, v_hbm, o_ref,
                        kbuf, vbuf, ksem, vsem, *, layer, last_block):
    n_seq = pt_ref.shape[0]

    def piece(t, h, half):
        return (t * N_HEADS + h) * 2 + half

    def copies(seq, slot):
        out = []
        for t in range(MOBA_TOPK):
            for h in range(N_HEADS):
                blk = sel_ref[seq, t * N_HEADS + h]
                for half in range(2):
                    page = pt_ref[seq, 2 * blk + half]
                    i = piece(t, h, half)
                    out.append(pltpu.make_async_copy(k_hbm.at[layer, page, h], kbuf.at[slot, i], ksem.at[slot]))
                    out.append(pltpu.make_async_copy(v_hbm.at[layer, page, h], vbuf.at[slot, i], vsem.at[slot]))
        return out

    for cp in copies(0, 0):
        cp.start()

    def body(seq, carry):
        slot = lax.rem(seq, 2)

        @pl.when(seq + 1 < n_seq)
        def _():
            for cp in copies(seq + 1, 1 - slot):
                cp.start()

        for cp in copies(seq, slot):
            cp.wait()
        for h in range(N_HEADS):
            qf = jnp.broadcast_to(q_ref[seq, h].astype(F32), (8, HEAD_DIM))
            q = qf.astype(BF16)
            s_self = (jnp.sum(qf * kn_ref[seq, h].astype(BF16).astype(F32), axis=-1, keepdims=True)
                      + bias_ref[h, 2][:, 0:1])
            scores = []
            for t in range(MOBA_TOPK):
                kt = jnp.concatenate([kbuf[slot, piece(t, h, 0)], kbuf[slot, piece(t, h, 1)]], axis=1).astype(BF16)
                near = sel_ref[seq, t * N_HEADS + h] == last_block
                scores.append(_dot(q, kt) + bias_ref[h, jnp.where(near, 1, 0)])
            m = s_self
            for s in scores:
                m = jnp.maximum(m, jnp.max(s, axis=-1, keepdims=True))
            p_self = jnp.exp(s_self - m)
            l = p_self
            acc = p_self * vn_ref[seq, h].astype(BF16).astype(F32)
            for t, s in enumerate(scores):
                p = jnp.exp(s - m)
                l = l + jnp.sum(p, axis=-1, keepdims=True)
                vt = jnp.concatenate([vbuf[slot, piece(t, h, 0)], vbuf[slot, piece(t, h, 1)]], axis=1).astype(BF16)
                acc = acc + _dot_nt(p.astype(BF16), vt)
            o_ref[seq, h] = ((acc / l)[0:1] * g_ref[seq, h].astype(F32)).astype(BF16)
        return carry

    lax.fori_loop(0, n_seq, body, 0)


def moba_decode(page_table, sel, qh, k_new, v_new, dec_bias, gact_mb, cache_k, cache_v, layer):
    b, n_pages = page_table.shape
    last_block = n_pages * PAGE // KV_TILE - 1

    pieces = 2 * MOBA_TOPK * N_HEADS
    vec = pl.BlockSpec((b, N_HEADS, 1, HEAD_DIM), lambda i, pt, sl: (0, 0, 0, 0))
    hbm = pl.BlockSpec(memory_space=pl.ANY)
    grid_spec = pltpu.PrefetchScalarGridSpec(
        num_scalar_prefetch=2,
        grid=(1,),
        in_specs=[vec, vec, vec, pl.BlockSpec(dec_bias.shape, lambda i, pt, sl: (0, 0, 0, 0)), vec, hbm, hbm],
        out_specs=vec,
        scratch_shapes=[pltpu.VMEM((2, pieces, HEAD_DIM, PAGE), F32), pltpu.VMEM((2, pieces, HEAD_DIM, PAGE), F32),
                        pltpu.SemaphoreType.DMA((2,)), pltpu.SemaphoreType.DMA((2,))],
    )
    return pl.pallas_call(
        functools.partial(_moba_decode_kernel, layer=layer, last_block=last_block),
        out_shape=jax.ShapeDtypeStruct((b, N_HEADS, 1, HEAD_DIM), BF16),
        grid_spec=grid_spec,
        compiler_params=_params(("arbitrary",)),
        name="moba_decode",
    )(page_table, sel, qh, k_new, v_new, dec_bias, gact_mb, cache_k, cache_v)


def _mlstm_decode_kernel(q_ref, k_ref, v_ref, gt_ref, g_ref, nw_ref, c_ref, n_ref, m_ref,
                         h_ref, co_ref, no_ref, mo_ref):
    r = lax.broadcasted_iota(I32, (ML_D, ML_D), 0)
    c = lax.broadcasted_iota(I32, (ML_D, ML_D), 1)
    eye = r == c
    for h in range(ML_HEADS):
        sl = slice(h * ML_D, (h + 1) * ML_D)
        q = q_ref[:, sl].astype(F32)
        k = k_ref[:, sl].astype(F32)
        v = v_ref[:, sl].astype(F32)
        i_pre = gt_ref[:, h:h + 1]
        logf = gt_ref[:, ML_HEADS + h:ML_HEADS + h + 1]
        m_prev = m_ref[:, h:h + 1]
        cm = c_ref[h]
        nv = n_ref[h:h + 1, :]
        g = logf + m_prev
        m_new = jnp.maximum(g, i_pre)
        w_in = jnp.exp(i_pre - m_new)
        w_st = jnp.exp(g - m_new)
        qk = jnp.sum(q * k, axis=-1, keepdims=True)
        cq = _dot_nt(jnp.broadcast_to(q, (8, ML_D)), cm, precision=HI)[0:1]
        nq = jnp.sum(nv * q, axis=-1, keepdims=True)
        num = (qk * w_in) * v + w_st * cq
        den = qk * w_in + w_st * nq
        hout = num / jnp.maximum(jnp.abs(den), jnp.exp(-m_new))
        h_ref[:, sl] = _gated_head_norm(hout, nw_ref[:, sl], g_ref[:, sl])
        v_diag = jnp.where(eye, jnp.broadcast_to(v, (ML_D, ML_D)), 0.0)
        outer = jnp.dot(v_diag, jnp.broadcast_to(k, (ML_D, ML_D)), precision=HI, preferred_element_type=F32)
        co_ref[h] = w_st * cm + w_in * outer
        no_ref[h:h + 1, :] = w_st * nv + w_in * k
        mo_ref[:, h:h + 1] = m_new


def mlstm_decode(q, k, v, gates, gact, norm_w, c0, n0, m0, layer):
    b = q.shape[0]
    vec = pl.BlockSpec((None, 1, D_ML), lambda bi: (bi, 0, 0))
    cs = pl.BlockSpec((None, ML_HEADS, ML_D, ML_D), lambda bi: (bi, 0, 0, 0))
    ns = pl.BlockSpec((None, ML_HEADS, ML_D), lambda bi: (bi, 0, 0))
    msp = pl.BlockSpec((None, 1, ML_HEADS), lambda bi: (bi, 0, 0))
    return pl.pallas_call(
        _mlstm_decode_kernel,
        out_shape=(jax.ShapeDtypeStruct((b, 1, D_ML), BF16),
                   jax.ShapeDtypeStruct(c0.shape[1:], F32),
                   jax.ShapeDtypeStruct(n0.shape[1:], F32),
                   jax.ShapeDtypeStruct(m0.shape, F32)),
        grid=(b,),
        in_specs=[vec, vec, vec,
                  pl.BlockSpec((None, 1, 2 * ML_HEADS), lambda bi: (bi, 0, 0)),
                  pl.BlockSpec((None, 1, D_ML), lambda bi: (bi, 0, 1)),
                  pl.BlockSpec((1, D_ML), lambda bi: (0, 0)),
                  pl.BlockSpec((None, None, ML_HEADS, ML_D, ML_D), lambda bi: (layer, bi, 0, 0, 0)),
                  pl.BlockSpec((None, None, ML_HEADS, ML_D), lambda bi: (layer, bi, 0, 0)),
                  msp],
        out_specs=(vec, cs, ns, msp),
        compiler_params=_params(("parallel",)),
        name="mlstm_decode",
    )(q, k, v, gates, gact, norm_w, c0, n0, m0)


SB_DECODE_GROUP = 16
MOBA_SELECT_GROUP = 32
STREAM_DEPTH = 3
PROMPT_TM = 512


def _bf16_weights(w_in, w_out):
    w_bf = w_in.astype(BF16)
    w_lo = (w_in[:, :, O_QMB:O_VMB] - w_bf[:, :, O_QMB:O_VMB].astype(F32)).astype(BF16)
    return w_bf, w_lo, w_out.astype(BF16)


def kernel(x_prompt, x_sample, cache_sb_k, cache_sb_v, cache_moba_k, cache_moba_v, state_mlstm_C, state_mlstm_n,
           state_mlstm_m, page_table, c_prompt, c_sample, norm_w, w_ada, b_ada, w_in, b_gates, q_norm_w, k_norm_w,
           rel_bias, ml_norm_w, w_out):
    bp, tp, d = x_prompt.shape
    bs = x_sample.shape[0]
    depth = w_in.shape[0]
    mp = bp * tp
    n_pages = page_table.shape[1]

    mod = ada_modulation(jnp.concatenate([c_prompt, c_sample], axis=0), w_ada, b_ada)
    bias_p, bias_d = bias_tiles(rel_bias)
    lanes_last = lambda a: jnp.transpose(a, (0, 1, 3, 4, 2))
    sbk, sbv, mbk, mbv = (lanes_last(a) for a in (cache_sb_k, cache_sb_v, cache_moba_k, cache_moba_v))

    xp = x_prompt.reshape(mp, d)
    xs = x_sample.reshape(bs, d)
    p_out = [[] for _ in range(3)]
    s_out = [[] for _ in range(7)]
    kv_pages = ()
    w_bf, w_lo, wo_bf = _bf16_weights(w_in, w_out)
    for l in range(depth):
        nw = norm_w[l].reshape(1, d)
        bg = b_gates[l].reshape(1, 2 * ML_HEADS)
        qnw = jnp.tile(q_norm_w[l], N_HEADS).reshape(1, D_ATT)
        knw = jnp.tile(k_norm_w[l], N_HEADS).reshape(1, D_ATT)
        mlw = ml_norm_w[l].reshape(1, D_ML)
        shift, scale, gate = jnp.split(mod[l], 3, axis=-1)

        pm = lambda a: a[:bp].reshape(bp, 1, d)
        (qsb_t, ksb_t, ksb, vsb_t, vsb_th, qmb_t, qmb_th, kmb_t, kmb, vmb_t, vmb_th,
         qml, kml, vml, gact, gates, kmean) = \
            input_projection(xp, nw, pm(scale), pm(shift), w_bf, w_lo, bg, qnw, knw,
                             tm=PROMPT_TM, rows_per_mod=tp // PROMPT_TM, prompt=True,
                             layer=l, depth=depth, kv_carried=kv_pages)
        kv_pages = (ksb_t, vsb_t, kmb_t, vmb_t)
        seq = lambda a: a.reshape(bp, tp, a.shape[-1])
        pages = lambda a: a.reshape(bp, tp // PAGE, D_ATT, PAGE)
        gact3 = seq(gact)
        ysb = sb_prompt(pages(qsb_t), seq(ksb), pages(vsb_th), gact3)
        ymb = moba_prompt(pages(qmb_t), pages(qmb_th), seq(kmb), pages(vmb_th),
                          kmean.reshape(bp, tp // KV_TILE, D_ATT), bias_p, gact3)
        gates3 = seq(gates)
        ml_pages = lambda a: a.reshape(bp, tp // ML_CHUNK, D_ML, ML_CHUNK)
        yml, c_p, n_p, m_p = mlstm_prompt(ml_pages(qml), seq(kml), ml_pages(vml), gates3, gates3.swapaxes(1, 2),
                                          gact3, mlw)
        xp = output_projection(ysb.reshape(mp, D_ATT), ymb.reshape(mp, D_ATT), yml.reshape(mp, D_ML), xp, pm(gate),
                               wo_bf, tm=PROMPT_TM, rows_per_mod=tp // PROMPT_TM, layer=l)
        for lst, a in zip(p_out, (c_p, n_p, m_p.reshape(bp, ML_HEADS))):
            lst.append(a)

        sm = lambda a: a[bp:].reshape(1, bs, d)
        (qsb, ksb, vsb, qmb, qmbh, kmb, vmb, qml, kml, vml, gact, gates) = \
            input_projection(xs, nw, sm(scale), sm(shift), w_bf, w_lo, bg, qnw, knw,
                             tm=bs, rows_per_mod=1, prompt=False, layer=l)
        tok = lambda a: a.reshape(bs, 1, a.shape[-1])
        heads = lambda a: a.reshape(bs, N_HEADS, 1, HEAD_DIM)
        gact3 = tok(gact)
        ysb = sb_decode(page_table, tok(qsb), gact3[:, :, :D_ATT], sbk, sbv, l, group=min(SB_DECODE_GROUP, n_pages),
                        depth=STREAM_DEPTH)
        sel = moba_decode_select(page_table, qmb.reshape(bs, D_ATT, 1), mbk, l,
                                 group=min(MOBA_SELECT_GROUP, n_pages), depth=STREAM_DEPTH)
        sel = sel[:, :MOBA_TOPK, :N_HEADS].reshape(bs, MOBA_TOPK * N_HEADS)
        ymb = moba_decode(page_table, sel, heads(qmbh), heads(kmb), heads(vmb), bias_d,
                          heads(gact[:, D_ATT:2 * D_ATT]), mbk, mbv, l)
        yml, c_s, n_s, m_s = mlstm_decode(tok(qml), tok(kml), tok(vml), tok(gates), gact3, mlw,
                                          state_mlstm_C, state_mlstm_n, state_mlstm_m[l].reshape(bs, 1, ML_HEADS), l)
        xs = output_projection(ysb.reshape(bs, D_ATT), ymb.reshape(bs, D_ATT), yml.reshape(bs, D_ML), xs, sm(gate),
                               wo_bf, tm=bs, rows_per_mod=1, layer=l)
        new = lambda a: a.reshape(bs, 1, N_HEADS, HEAD_DIM)
        for lst, a in zip(s_out, (new(ksb), new(vsb), new(kmb), new(vmb), c_s, n_s, m_s.reshape(bs, ML_HEADS))):
            lst.append(a)

    paged = lambda a: jnp.transpose(a.reshape(depth, bp, tp // PAGE, N_HEADS, HEAD_DIM, PAGE), (0, 1, 2, 5, 3, 4))
    P = [paged(a) for a in kv_pages] + [jnp.stack(a) for a in p_out]
    S = [jnp.stack(a) for a in s_out]
    return (xp.reshape(bp, tp, d), xs.reshape(bs, 1, d),
            P[0], P[1], P[2], P[3], P[4], P[5], P[6], S[0], S[1], S[2], S[3], S[4], S[5], S[6])
```

```python
import functools
import math

import numpy as np
import jax
import jax.numpy as jnp
from jax import lax
from jax.experimental import pallas as pl
from jax.experimental.pallas import tpu as pltpu

F32 = jnp.float32
BF16 = jnp.bfloat16
I32 = jnp.int32
HI = lax.Precision.HIGHEST

EPS = 1e-6
HEAD_DIM = 64
N_HEADS = 4
D_ATT = N_HEADS * HEAD_DIM
ML_HEADS = 4
ML_D = 128
D_ML = ML_HEADS * ML_D
Q_BLOCK = 128
KV_TILE = 256
MOBA_TOPK = 3
N_BUCKETS = 32
MAX_DISTANCE = 128
ML_CHUNK = 128
PAGE = 128
HEAD_SHIFT = 6
NEG_INF = float("-inf")

VMEM_LIMIT = 56 * 1024 * 1024

_SIZES = [D_ATT] * 8 + [D_ML] * 5 + [ML_HEADS] * 2
_OFF = np.concatenate([[0], np.cumsum(_SIZES)]).tolist()
(O_QSB, O_KSB, O_VSB, O_GSB, O_QMB, O_KMB, O_VMB, O_GMB,
 O_QML, O_KML, O_VML, O_OML, O_GML, O_IML, O_FML, O_END) = _OFF


def _params(sem, vmem=VMEM_LIMIT):
    return pltpu.CompilerParams(dimension_semantics=sem, vmem_limit_bytes=vmem)


def _dot(a, b):
    return jnp.dot(a, b, preferred_element_type=F32)


def _dot_nt(a, b, precision=None):
    return lax.dot_general(a, b, (((1,), (1,)), ((), ())), precision=precision,
                           preferred_element_type=F32)


SOFTPLUS_CLAMP = 64.0


def _softplus(z):
    return jnp.maximum(jnp.log(1.0 + jnp.exp(jnp.minimum(z, SOFTPLUS_CLAMP))), z)


def _head_of_lane(shape, axis):
    return lax.shift_right_logical(lax.broadcasted_iota(I32, shape, axis), HEAD_SHIFT)


def _ada_kernel(c_ref, w_ref, b_ref, o_ref):
    c = c_ref[...]
    a = c * jax.nn.sigmoid(c)
    o_ref[...] = jnp.dot(a, w_ref[...], precision=HI, preferred_element_type=F32) + b_ref[...]


def ada_modulation(c_all, w_ada, b_ada):
    depth, d, d3 = w_ada.shape
    rows = c_all.shape[0]
    nt = d3 // d
    return pl.pallas_call(
        _ada_kernel,
        out_shape=jax.ShapeDtypeStruct((depth, rows, d3), F32),
        grid=(depth, nt),
        in_specs=[pl.BlockSpec((rows, d), lambda l, j: (0, 0)),
                  pl.BlockSpec((None, d, d), lambda l, j: (l, 0, j)),
                  pl.BlockSpec((None, 1, d), lambda l, j: (l, 0, j))],
        out_specs=pl.BlockSpec((None, rows, d), lambda l, j: (l, 0, j)),
        compiler_params=_params(("parallel", "parallel")),
        name="ada_modulation",
    )(c_all, w_ada, b_ada.reshape(depth, 1, d3))


def _bucket_thresholds():
    d = np.arange(0, 4 * MAX_DISTANCE, dtype=np.int32)
    max_exact = N_BUCKETS // 2
    df = np.maximum(d, 1).astype(np.float32)
    large = max_exact + (np.log(df / np.float32(max_exact)) / np.float32(math.log(MAX_DISTANCE / max_exact))
                         * np.float32(N_BUCKETS - max_exact)).astype(np.int32)
    large = np.minimum(large, N_BUCKETS - 1)
    bucket = np.where(d < max_exact, d, large)
    thr = []
    for k in range(1, N_BUCKETS):
        idx = np.nonzero(bucket >= k)[0]
        thr.append(int(idx[0]))
    assert all(np.all((bucket >= k) == (d >= t)) for k, t in zip(range(1, N_BUCKETS), thr))
    return thr


_BUCKET_THR = _bucket_thresholds()

_PATTERN_OFFSETS = (0, Q_BLOCK, KV_TILE)
P_FAR = len(_PATTERN_OFFSETS)


def _bias_from_dist(dist, rb_ref, h):
    b = jnp.full(dist.shape, rb_ref[0, h], F32)
    for k, t in zip(range(1, N_BUCKETS), _BUCKET_THR):
        b = jnp.where(dist >= t, rb_ref[k, h], b)
    return jnp.where(dist >= 0, b, NEG_INF)


def _bias_kernel(rb_ref, tile_ref, dec_ref):
    j = lax.broadcasted_iota(I32, (KV_TILE, Q_BLOCK), 0)
    i = lax.broadcasted_iota(I32, (KV_TILE, Q_BLOCK), 1)
    for p, off in enumerate(_PATTERN_OFFSETS):
        for h in range(N_HEADS):
            tile_ref[p, :, h * Q_BLOCK:(h + 1) * Q_BLOCK] = _bias_from_dist(off + i - j, rb_ref, h)
    for h in range(N_HEADS):
        tile_ref[P_FAR, :, h * Q_BLOCK:(h + 1) * Q_BLOCK] = jnp.full((KV_TILE, Q_BLOCK), rb_ref[N_BUCKETS - 1, h], F32)
    jr = lax.broadcasted_iota(I32, (8, KV_TILE), 1)
    for h in range(N_HEADS):
        dec_ref[h, 0] = jnp.full((8, KV_TILE), rb_ref[N_BUCKETS - 1, h], F32)
        dec_ref[h, 1] = _bias_from_dist(KV_TILE - jr, rb_ref, h)
        dec_ref[h, 2] = jnp.full((8, KV_TILE), rb_ref[0, h], F32)


def bias_tiles(rel_bias):
    return pl.pallas_call(
        _bias_kernel,
        out_shape=(jax.ShapeDtypeStruct((P_FAR + 1, KV_TILE, N_HEADS * Q_BLOCK), F32),
                   jax.ShapeDtypeStruct((N_HEADS, 3, 8, KV_TILE), F32)),
        in_specs=[pl.BlockSpec(memory_space=pltpu.SMEM)],
        name="bias_tiles",
    )(rel_bias)


def _inproj_kernel(x_ref, nw_ref, sc_ref, sh_ref, w_ref, wlo_ref, bg_ref, qnw_ref, knw_ref, *outs, prompt, layer,
                   n_carried):
    outs = list(outs[n_carried:])

    def paged(p):
        return [p[pg * PAGE:(pg + 1) * PAGE].T for pg in range(p.shape[0] // PAGE)]

    def emit(p, *, rows_bf16=False, rows_f32=False, pages_f32=False, pages_bf16=False):
        pages = paged(p) if (pages_f32 or pages_bf16) else None
        for want, dtype, is_pages in ((pages_f32, F32, True), (rows_f32, F32, False),
                                      (pages_bf16, BF16, True), (rows_bf16, BF16, False)):
            if not want:
                continue
            ref = outs.pop(0)
            if is_pages and len(ref.shape) == 4:
                for slab in range(ref.shape[0]):
                    for pg, t in enumerate(pages):
                        ref[slab, pg] = t.astype(dtype) if slab == layer else jnp.zeros_like(t, dtype)
            elif is_pages:
                for pg, t in enumerate(pages):
                    ref[pg] = t.astype(dtype)
            else:
                ref[...] = p.astype(dtype)

    x = x_ref[...]
    ms = jnp.mean(x * x, axis=-1, keepdims=True)
    h = (x * lax.rsqrt(ms + EPS) * nw_ref[...]) * (1.0 + sc_ref[...]) + sh_ref[...]
    hb = h.astype(BF16)
    hlo = (h - hb.astype(F32)).astype(BF16)

    def proj(a, b):
        return _dot_nt(hb, w_ref[a:b, :])

    def proj3(a, b, la, lb):
        wh = w_ref[a:b, :]
        return _dot_nt(hb, wh) + (_dot_nt(hlo, wh) + _dot_nt(hb, wlo_ref[la:lb, :]))

    def silu(g):
        return g * jax.nn.sigmoid(g)

    r = lax.broadcasted_iota(I32, (D_ATT, D_ATT), 0)
    c = lax.broadcasted_iota(I32, (D_ATT, D_ATT), 1)
    same_head = lax.shift_right_logical(r, HEAD_SHIFT) == lax.shift_right_logical(c, HEAD_SHIFT)
    head_mean = jnp.where(same_head, 1.0 / HEAD_DIM, 0.0).astype(BF16)

    def head_norm(p, w):
        sq = p * p
        hi = sq.astype(BF16)
        lo = (sq - hi.astype(F32)).astype(BF16)
        ms = _dot(hi, head_mean) + _dot(lo, head_mean)
        return p * lax.rsqrt(ms + EPS) * w

    score_scale = HEAD_DIM ** -0.5
    emit(proj(O_QSB, O_KSB) * score_scale, pages_bf16=prompt, rows_bf16=not prompt)
    emit(proj(O_KSB, O_VSB), pages_f32=prompt, rows_bf16=prompt, rows_f32=not prompt)
    emit(proj(O_VSB, O_GSB), pages_f32=prompt, pages_bf16=prompt, rows_f32=not prompt)
    g_sb = silu(proj(O_GSB, O_QMB))
    qn = head_norm(proj3(O_QMB, O_KMB, 0, D_ATT), qnw_ref[...])
    if prompt:
        qt = paged(qn)
        ref_f, ref_h = outs.pop(0), outs.pop(0)
        for pg, t in enumerate(qt):
            ref_f[pg] = t
            ref_h[pg] = (t * score_scale).astype(BF16)
    else:
        emit(qn, rows_f32=True)
        emit(qn * score_scale, rows_bf16=True)
    kn = head_norm(proj3(O_KMB, O_VMB, D_ATT, 2 * D_ATT), knw_ref[...])
    emit(kn, pages_f32=prompt, rows_bf16=prompt, rows_f32=not prompt)
    emit(proj(O_VMB, O_GMB), pages_f32=prompt, pages_bf16=prompt, rows_f32=not prompt)
    g_mb = silu(proj(O_GMB, O_QML))
    emit(proj(O_QML, O_KML), pages_bf16=prompt, rows_bf16=not prompt)
    emit(proj(O_KML, O_VML) * (ML_D ** -0.5), rows_bf16=True)
    emit(proj(O_VML, O_OML), pages_bf16=prompt, rows_bf16=not prompt)
    o = proj(O_OML, O_GML)
    g_ml = jax.nn.sigmoid(o) * silu(proj(O_GML, O_IML))
    gact_ref = outs.pop(0)
    gact_ref[:, 0:D_ATT] = g_sb.astype(BF16)
    gact_ref[:, D_ATT:2 * D_ATT] = g_mb.astype(BF16)
    gact_ref[:, 2 * D_ATT:] = g_ml.astype(BF16)
    pre = proj(O_IML, O_END) + bg_ref[...]
    is_f = lax.broadcasted_iota(I32, pre.shape, 1) >= ML_HEADS
    logf = jnp.minimum(pre, 0.0) - jnp.log1p(jnp.exp(-jnp.abs(pre)))
    outs.pop(0)[...] = jnp.where(is_f, logf, pre)
    if prompt:
        kmean_ref = outs.pop(0)
        for i in range(kn.shape[0] // KV_TILE):
            kmean_ref[i] = jnp.mean(kn[i * KV_TILE:(i + 1) * KV_TILE], axis=0, keepdims=True)
    assert not outs


def input_projection(x, norm_w, scale, shift, w_bf, w_lo, b_gates, qnw, knw, *, tm, rows_per_mod, prompt,
                     layer=0, depth=1, kv_carried=()):
    m, d = x.shape
    n_tiles = m // tm
    rmod = scale.shape[1]
    row = lambda i: (i, 0)
    const = lambda i: (0, 0)
    mod = lambda i: (i // rows_per_mod, 0, 0)

    def rows(width, dtype):
        return jax.ShapeDtypeStruct((m, width), dtype), pl.BlockSpec((tm, width), row)

    def pages(dtype, width=D_ATT):
        return (jax.ShapeDtypeStruct((m // PAGE, width, PAGE), dtype),
                pl.BlockSpec((tm // PAGE, width, PAGE), lambda i: (i, 0, 0)))

    def kv_pages():
        shape = jax.ShapeDtypeStruct((depth, m // PAGE, D_ATT, PAGE), F32)
        if kv_carried:
            return shape, pl.BlockSpec((None, tm // PAGE, D_ATT, PAGE), lambda i: (layer, i, 0, 0))
        return shape, pl.BlockSpec((depth, tm // PAGE, D_ATT, PAGE), lambda i: (0, i, 0, 0))

    kv_out_index = (1, 3, 7, 9)
    if prompt:
        nb = tm // KV_TILE
        outs = [pages(BF16), kv_pages(), rows(D_ATT, BF16), kv_pages(), pages(BF16),
                pages(F32), pages(BF16), kv_pages(), rows(D_ATT, BF16), kv_pages(), pages(BF16)]
        tail = [(jax.ShapeDtypeStruct((n_tiles, nb, 1, D_ATT), F32),
                 pl.BlockSpec((None, nb, 1, D_ATT), lambda i: (i, 0, 0, 0)))]
    else:
        outs = [rows(D_ATT, BF16), rows(D_ATT, F32), rows(D_ATT, F32),
                rows(D_ATT, F32), rows(D_ATT, BF16), rows(D_ATT, F32), rows(D_ATT, F32)]
        tail = []
    ml_qv = pages(BF16, D_ML) if prompt else rows(D_ML, BF16)
    outs += [ml_qv, rows(D_ML, BF16), ml_qv,
             rows(2 * D_ATT + D_ML, BF16), rows(2 * ML_HEADS, F32)] + tail
    n_in = 9
    return pl.pallas_call(
        functools.partial(_inproj_kernel, prompt=prompt, layer=layer, n_carried=len(kv_carried)),
        out_shape=tuple(o[0] for o in outs),
        input_output_aliases={n_in + j: kv_out_index[j] for j in range(len(kv_carried))},
        grid=(n_tiles,),
        in_specs=[pl.BlockSpec((tm, d), row),
                  pl.BlockSpec((1, d), const),
                  pl.BlockSpec((None, rmod, d), mod),
                  pl.BlockSpec((None, rmod, d), mod),
                  pl.BlockSpec((None,) + w_bf.shape[1:], lambda i: (layer, 0, 0)),
                  pl.BlockSpec((None,) + w_lo.shape[1:], lambda i: (layer, 0, 0)),
                  pl.BlockSpec((1, 2 * ML_HEADS), const),
                  pl.BlockSpec((1, D_ATT), const),
                  pl.BlockSpec((1, D_ATT), const)] + [pl.BlockSpec(memory_space=pl.ANY)] * len(kv_carried),
        out_specs=tuple(o[1] for o in outs),
        compiler_params=_params(("parallel",)),
        name="input_projection",
    )(x, norm_w, scale, shift, w_bf, w_lo, b_gates, qnw, knw, *kv_carried)


def _outproj_kernel(ysb_ref, ymb_ref, yml_ref, x_ref, gate_ref, w_ref, o_ref):
    y = (_dot(ysb_ref[...], w_ref[0:D_ATT, :]) + _dot(ymb_ref[...], w_ref[D_ATT:2 * D_ATT, :])
         + _dot(yml_ref[...], w_ref[2 * D_ATT:, :]))
    o_ref[...] = x_ref[...] + gate_ref[...] * y


def output_projection(ysb, ymb, yml, x, gate, w_bf, *, tm, rows_per_mod, layer):
    m, d = x.shape
    rmod = gate.shape[1]
    row = lambda i: (i, 0)
    return pl.pallas_call(
        _outproj_kernel,
        out_shape=jax.ShapeDtypeStruct((m, d), F32),
        grid=(m // tm,),
        in_specs=[pl.BlockSpec((tm, D_ATT), row), pl.BlockSpec((tm, D_ATT), row), pl.BlockSpec((tm, D_ML), row),
                  pl.BlockSpec((tm, d), row),
                  pl.BlockSpec((None, rmod, d), lambda i: (i // rows_per_mod, 0, 0)),
                  pl.BlockSpec((None,) + w_bf.shape[1:], lambda i: (layer, 0, 0))],
        out_specs=pl.BlockSpec((tm, d), row),
        compiler_params=_params(("parallel",)),
        name="output_projection",
    )(ysb, ymb, yml, x, gate, w_bf)


def _suffix_matrix(n, transposed=False):
    r = lax.broadcasted_iota(I32, (n, n), 0)
    c = lax.broadcasted_iota(I32, (n, n), 1)
    return jnp.where((c > r) if transposed else (r > c), 1.0, 0.0).astype(BF16)


def _heads_on_lanes(qt):
    head = _head_of_lane((D_ATT, 1), 0)
    return jnp.concatenate([jnp.where(head == h, qt, jnp.zeros_like(qt)) for h in range(N_HEADS)], axis=1)


def _heads_from_lanes(acc_t, q):
    head = _head_of_lane((D_ATT, 1), 0)
    out = jnp.zeros((D_ATT, q), F32)
    for h in range(N_HEADS):
        out = out + jnp.where(head == h, acc_t[:, h * q:(h + 1) * q], 0.0)
    return out.T


def _kv_tile(k_ref, vt_ref, n):
    start = pl.multiple_of(n * KV_TILE, KV_TILE)
    vt = jnp.concatenate([vt_ref[2 * n], vt_ref[2 * n + 1]], axis=1)
    return k_ref[pl.ds(start, KV_TILE), :], vt


def _sb_tile_t(q4t, k, vt, upper, carry, mask):
    z = _dot(k, q4t)
    sp = _softplus(z)
    if mask is not None:
        sp_sum = jnp.where(mask, sp, 0.0)
    else:
        sp_sum = sp
    later = _dot(upper, sp_sum.astype(BF16))
    w = jnp.exp(((z - sp) - later) - carry)
    if mask is not None:
        w = jnp.where(mask, w, 0.0)
    return _dot(vt, w.astype(BF16)), carry + jnp.sum(sp_sum, axis=0, keepdims=True)


def _sb_pair_t(q4t, tile_a, tile_b, upper, carry, mask_a):
    (ka, vta), (kb, vtb) = tile_a, tile_b
    za, zb = _dot(ka, q4t), _dot(kb, q4t)
    spa, spb = _softplus(za), _softplus(zb)
    sum_a = spa if mask_a is None else jnp.where(mask_a, spa, 0.0)
    la, lb = _dot(upper, sum_a.astype(BF16)), _dot(upper, spb.astype(BF16))
    carry_b = carry + jnp.sum(sum_a, axis=0, keepdims=True)
    wa = jnp.exp(((za - spa) - la) - carry)
    if mask_a is not None:
        wa = jnp.where(mask_a, wa, 0.0)
    wb = jnp.exp(((zb - spb) - lb) - carry_b)
    pv = _dot(vta, wa.astype(BF16)) + _dot(vtb, wb.astype(BF16))
    return pv, carry_b + jnp.sum(spb, axis=0, keepdims=True)


def _sb_prompt_kernel(qt_ref, k_ref, vt_ref, g_ref, o_ref, acc_ref, carry_ref):
    i = pl.program_id(1)
    cols = N_HEADS * Q_BLOCK
    q4t = _heads_on_lanes(qt_ref[...])
    upper = _suffix_matrix(KV_TILE, transposed=True)
    last = (i * Q_BLOCK) // KV_TILE
    kpos = last * KV_TILE + lax.broadcasted_iota(I32, (KV_TILE, cols), 0)
    qpos = i * Q_BLOCK + (lax.broadcasted_iota(I32, (KV_TILE, cols), 1) & (Q_BLOCK - 1))
    mask = kpos < qpos
    zero = jnp.zeros((1, cols), F32)

    @pl.when(last % 2 == 0)
    def _():
        k, vt = _kv_tile(k_ref, vt_ref, last)
        acc_ref[...], carry_ref[...] = _sb_tile_t(q4t, k, vt, upper, zero, mask)

    @pl.when(last % 2 == 1)
    def _():
        acc_ref[...], carry_ref[...] = _sb_pair_t(q4t, _kv_tile(k_ref, vt_ref, last), _kv_tile(k_ref, vt_ref, last - 1),
                                                  upper, zero, mask)

    first = last - 1 - last % 2

    def body(s, carry):
        pv, carry = _sb_pair_t(q4t, _kv_tile(k_ref, vt_ref, first - 2 * s), _kv_tile(k_ref, vt_ref, first - 2 * s - 1),
                               upper, carry, None)
        acc_ref[...] += pv
        return carry

    lax.fori_loop(0, last // 2, body, carry_ref[...])
    o_ref[...] = (_heads_from_lanes(acc_ref[...], Q_BLOCK) * g_ref[...].astype(F32)).astype(BF16)


def sb_prompt(qt, k, vt, gact):
    b, t, _ = k.shape
    n_pages = t // PAGE
    blk = pl.BlockSpec((None, Q_BLOCK, D_ATT), lambda bi, i: (bi, i, 0))
    return pl.pallas_call(
        _sb_prompt_kernel,
        out_shape=jax.ShapeDtypeStruct((b, t, D_ATT), BF16),
        grid=(b, t // Q_BLOCK),
        in_specs=[pl.BlockSpec((None, None, D_ATT, PAGE), lambda bi, i: (bi, i, 0, 0)),
                  pl.BlockSpec((None, t, D_ATT), lambda bi, i: (bi, 0, 0)),
                  pl.BlockSpec((None, n_pages, D_ATT, PAGE), lambda bi, i: (bi, 0, 0, 0)),
                  blk],
        out_specs=blk,
        scratch_shapes=[pltpu.VMEM((D_ATT, N_HEADS * Q_BLOCK), F32), pltpu.VMEM((1, N_HEADS * Q_BLOCK), F32)],
        compiler_params=_params(("parallel", "arbitrary")),
        name="sb_prompt",
    )(qt, k, vt, gact)


def _moba_select_t(gate, valid, idx):
    n = gate.shape[0]
    gate = jnp.where(valid, gate, NEG_INF)
    rank = jnp.zeros(gate.shape, I32)
    for m in range(n):
        gm = gate[m:m + 1, :]
        beats = (gm > gate) | ((gm == gate) & (idx > m))
        rank = rank + beats.astype(I32)
    return valid & (rank < MOBA_TOPK)


def _moba_prompt_kernel(qft_ref, qht_ref, k_ref, vt_ref, km_ref, bias_ref, g_ref, o_ref, acc_ref, m_ref, l_ref, sel_ref):
    i = pl.program_id(1)
    nb = km_ref.shape[0]
    own = (i * Q_BLOCK) // KV_TILE
    odd = (i * Q_BLOCK) % KV_TILE != 0
    q4t = _heads_on_lanes(qht_ref[...])
    gate = jnp.dot(km_ref[...], _heads_on_lanes(qft_ref[...]), precision=HI, preferred_element_type=F32)
    idx = lax.broadcasted_iota(I32, (nb, 1), 0)
    sel = _moba_select_t(gate, idx < own, idx)
    sel_ref[...] = jnp.where(sel, 0.0, NEG_INF)

    def attend(blocks, first=False):
        tiles = [_kv_tile(k_ref, vt_ref, n) for n, _ in blocks]
        scores = []
        for (k, _), (_, terms) in zip(tiles, blocks):
            s = _dot(k, q4t)
            for t in terms:
                s = s + t
            scores.append(s)
        m_new = functools.reduce(jnp.maximum, [jnp.max(s, axis=0, keepdims=True) for s in scores])
        if not first:
            m_old = m_ref[...]
            m_new = jnp.maximum(m_old, m_new)
            alpha = jnp.exp(m_old - m_new)
        ps = [jnp.exp(s - m_new) for s in scores]
        l_new = functools.reduce(jnp.add, [jnp.sum(p, axis=0, keepdims=True) for p in ps])
        pv = functools.reduce(jnp.add, [_dot(vt, p.astype(BF16)) for (_, vt), p in zip(tiles, ps)])
        m_ref[...] = m_new
        l_ref[...] = l_new if first else alpha * l_ref[...] + l_new
        acc_ref[...] = pv if first else alpha * acc_ref[...] + pv

    attend([(own, [bias_ref[jnp.where(odd, 1, 0)]])], first=True)
    near_pat = jnp.where(odd, P_FAR, 2)
    far_row = bias_ref[P_FAR, 0:1, :]
    for n in range(0, nb - 2, 2):
        @pl.when(n + 1 < own)
        def _():
            attend([(n, [far_row + sel_ref[n:n + 1, :]]),
                    (n + 1, [bias_ref[jnp.where(n + 1 == own - 1, near_pat, P_FAR)], sel_ref[n + 1:n + 2, :]])])

    @pl.when(own % 2 == 1)
    def _():
        attend([(own - 1, [bias_ref[near_pat], sel_ref[pl.ds(own - 1, 1), :]])])

    y = _heads_from_lanes(acc_ref[...] / l_ref[...], Q_BLOCK)
    o_ref[...] = (y * g_ref[...].astype(F32)).astype(BF16)


def moba_prompt(qft, qht, k, vt, kmean, bias, gact):
    b, t, _ = k.shape
    nb = t // KV_TILE
    cols = N_HEADS * Q_BLOCK
    blk = pl.BlockSpec((None, Q_BLOCK, D_ATT), lambda bi, i: (bi, i, 0))
    qpage = pl.BlockSpec((None, None, D_ATT, PAGE), lambda bi, i: (bi, i, 0, 0))
    return pl.pallas_call(
        _moba_prompt_kernel,
        out_shape=jax.ShapeDtypeStruct((b, t, D_ATT), BF16),
        grid=(b, t // Q_BLOCK),
        in_specs=[qpage, qpage,
                  pl.BlockSpec((None, t, D_ATT), lambda bi, i: (bi, 0, 0)),
                  pl.BlockSpec((None, t // PAGE, D_ATT, PAGE), lambda bi, i: (bi, 0, 0, 0)),
                  pl.BlockSpec((None, nb, D_ATT), lambda bi, i: (bi, 0, 0)),
                  pl.BlockSpec(bias.shape, lambda bi, i: (0, 0, 0)),
                  pl.BlockSpec((None, Q_BLOCK, D_ATT), lambda bi, i: (bi, i, 1))],
        out_specs=blk,
        scratch_shapes=[pltpu.VMEM((D_ATT, cols), F32), pltpu.VMEM((1, cols), F32), pltpu.VMEM((1, cols), F32),
                        pltpu.VMEM((nb, cols), F32)],
        compiler_params=_params(("parallel", "arbitrary")),
        name="moba_prompt",
    )(qft, qht, k, vt, kmean, bias, gact)


def _gated_head_norm(h, w, g):
    hn = h * lax.rsqrt(jnp.mean(h * h, axis=-1, keepdims=True) + EPS) * w
    return (hn * g.astype(F32)).astype(BF16)


def _mlstm_prompt_kernel(qt_ref, k_ref, vt_ref, gc_ref, gr_ref, g_ref, nw_ref, h_ref, c_ref, n_ref, m_ref,
                         cst_ref, nst_ref, ms_ref):
    ci = pl.program_id(1)
    L = ML_CHUNK

    @pl.when(ci == 0)
    def _():
        cst_ref[...] = jnp.zeros_like(cst_ref)
        nst_ref[...] = jnp.zeros_like(nst_ref)
        ms_ref[...] = jnp.zeros_like(ms_ref)

    r = lax.broadcasted_iota(I32, (L, L), 0)
    c = lax.broadcasted_iota(I32, (L, L), 1)
    lower = jnp.where(c <= r, 1.0, 0.0).astype(F32)
    upper = jnp.where(r <= c, 1.0, 0.0).astype(F32)
    src_before_tgt = r <= c
    gc = gc_ref[...]
    gr = gr_ref[...]
    bh_c = jnp.dot(lower, gc, precision=HI, preferred_element_type=F32)
    bh_r = jnp.dot(gr, upper, precision=HI, preferred_element_type=F32)

    for h in range(ML_HEADS):
        sl = slice(h * ML_D, (h + 1) * ML_D)
        qt, k, vt = qt_ref[sl, :], k_ref[:, sl], vt_ref[sl, :]
        m_prev = ms_ref[h][0:1, 0:1]
        b_r = bh_r[ML_HEADS + h:ML_HEADS + h + 1, :]
        a_r = gr[h:h + 1, :] - b_r
        a_c = gc[:, h:h + 1] - bh_c[:, ML_HEADS + h:ML_HEADS + h + 1]
        dm = jnp.where(src_before_tgt, a_c + b_r, NEG_INF)
        g = b_r + m_prev
        m_row = jnp.maximum(g, jnp.max(dm, axis=0, keepdims=True))
        w_inter = jnp.exp(g - m_row)
        sc = _dot(k, qt) * jnp.exp(dm - m_row)
        cst, nst = cst_ref[h], nst_ref[h]
        num = _dot(vt, sc.astype(BF16)) + w_inter * _dot(cst.astype(BF16), qt)
        den = jnp.sum(sc, axis=0, keepdims=True) + w_inter * _dot(nst.astype(BF16), qt)[0:1]
        hout = (num / jnp.maximum(jnp.abs(den), jnp.exp(-m_row))).T
        h_ref[:, sl] = _gated_head_norm(hout, nw_ref[:, sl], g_ref[:, sl])

        b_last = b_r[:, L - 1:L]
        m_new = jnp.maximum(b_last + m_prev, jnp.max(b_last + a_r, axis=-1, keepdims=True))
        ws = jnp.exp(b_last + a_r - m_new)
        wc = jnp.exp(b_last + m_prev - m_new)
        cst_ref[h] = wc * cst + _dot((vt.astype(F32) * ws).astype(BF16), k)
        nst_ref[h] = wc * nst + _dot(jnp.broadcast_to(ws, (8, L)).astype(BF16), k)
        ms_ref[h] = jnp.broadcast_to(m_new, ms_ref.shape[1:])

    @pl.when(ci == pl.num_programs(1) - 1)
    def _():
        for h in range(ML_HEADS):
            c_ref[h] = cst_ref[h]
            n_ref[h:h + 1, :] = nst_ref[h][0:1]
            m_ref[:, h:h + 1] = ms_ref[h][0:1, 0:1]


def mlstm_prompt(qt, k, vt, gates, gates_t, gact, norm_w):
    b, t, _ = k.shape
    blk = pl.BlockSpec((None, ML_CHUNK, D_ML), lambda bi, c: (bi, c, 0))
    page = pl.BlockSpec((None, None, D_ML, ML_CHUNK), lambda bi, c: (bi, c, 0, 0))
    return pl.pallas_call(
        _mlstm_prompt_kernel,
        out_shape=(jax.ShapeDtypeStruct((b, t, D_ML), BF16),
                   jax.ShapeDtypeStruct((b, ML_HEADS, ML_D, ML_D), F32),
                   jax.ShapeDtypeStruct((b, ML_HEADS, ML_D), F32),
                   jax.ShapeDtypeStruct((b, 1, ML_HEADS), F32)),
        grid=(b, t // ML_CHUNK),
        in_specs=[page, blk, page,
                  pl.BlockSpec((None, ML_CHUNK, 2 * ML_HEADS), lambda bi, c: (bi, c, 0)),
                  pl.BlockSpec((None, 2 * ML_HEADS, ML_CHUNK), lambda bi, c: (bi, 0, c)),
                  pl.BlockSpec((None, ML_CHUNK, D_ML), lambda bi, c: (bi, c, 1)),
                  pl.BlockSpec((1, D_ML), lambda bi, c: (0, 0))],
        out_specs=(blk,
                   pl.BlockSpec((None, ML_HEADS, ML_D, ML_D), lambda bi, c: (bi, 0, 0, 0)),
                   pl.BlockSpec((None, ML_HEADS, ML_D), lambda bi, c: (bi, 0, 0)),
                   pl.BlockSpec((None, 1, ML_HEADS), lambda bi, c: (bi, 0, 0))),
        scratch_shapes=[pltpu.VMEM((ML_HEADS, ML_D, ML_D), F32), pltpu.VMEM((ML_HEADS, 8, ML_D), F32),
                        pltpu.VMEM((ML_HEADS, 8, 128), F32)],
        compiler_params=_params(("parallel", "arbitrary")),
        name="mlstm_prompt",
    )(qt, k, vt, gates, gates_t, gact, norm_w)


def _page_stream(pt_ref, hbm_refs, bufs, sems, *, layer, group, depth, reverse, consume, init):
    n_seq, n_pages = pt_ref.shape
    per_seq = n_pages // group
    total = n_seq * per_seq

    def copies(g, slot):
        seq, c = g // per_seq, g % per_seq
        out = []
        for r in range(group):
            walk = c * group + r
            page = pt_ref[seq, n_pages - 1 - walk if reverse else walk]
            for hbm, buf, sem in zip(hbm_refs, bufs, sems):
                out.append(pltpu.make_async_copy(hbm.at[layer, page], buf.at[slot, r], sem.at[slot]))
        return out

    for g in range(depth - 1):
        for cp in copies(g, g):
            cp.start()

    def body(g, carry):
        slot = lax.rem(g, depth)
        ahead = g + depth - 1

        @pl.when(ahead < total)
        def _():
            for cp in copies(ahead, lax.rem(ahead, depth)):
                cp.start()

        for cp in copies(g, slot):
            cp.wait()
        return consume(g // per_seq, g % per_seq, slot, carry)

    return lax.fori_loop(0, total, body, init)


def _sb_decode_kernel(pt_ref, q_ref, g_ref, k_hbm, v_hbm, o_ref, kbuf, vbuf, ksem, vsem, *, layer, group, depth):
    tiles = group // 2
    last_chunk = pt_ref.shape[1] // group - 1
    rowi = lax.broadcasted_iota(I32, (8, D_ATT), 0)
    own_head = _head_of_lane((8, D_ATT), 1) == rowi
    suffix = _suffix_matrix(KV_TILE)

    def consume(seq, c, slot, state):
        fresh = c == 0
        acc = jnp.where(fresh, 0.0, state[0])
        carry = jnp.where(fresh, 0.0, state[1])
        qm = jnp.where(own_head, jnp.broadcast_to(q_ref[seq].astype(F32), (8, D_ATT)), 0.0).astype(BF16)

        def tile(buf, t):
            lo, hi = buf[slot, 2 * t + 1].reshape(D_ATT, PAGE), buf[slot, 2 * t].reshape(D_ATT, PAGE)
            return jnp.concatenate([lo, hi], axis=1).astype(BF16)

        z = jnp.concatenate([_dot(qm, tile(kbuf, t)) for t in range(tiles)], axis=0)
        sp = _softplus(z)
        lf = -sp
        hi = lf.astype(BF16).astype(F32)
        both = _dot(jnp.concatenate([hi, lf - hi], axis=0).astype(BF16), suffix)
        later = both[:8 * tiles] + both[8 * tiles:]
        tot = jnp.sum(lf, axis=-1, keepdims=True)
        carries = [carry]
        for t in range(tiles):
            carries.append(carries[-1] + tot[8 * t:8 * (t + 1)])
        w = jnp.exp((z - sp) + later + jnp.concatenate(carries[:tiles], axis=0))
        for t in range(tiles):
            acc = acc + _dot_nt(w[8 * t:8 * (t + 1)].astype(BF16), tile(vbuf, t))

        @pl.when(c == last_chunk)
        def _():
            y = jnp.sum(jnp.where(own_head, acc, 0.0), axis=0, keepdims=True)
            o_ref[seq] = (y * g_ref[seq].astype(F32)).astype(BF16)

        return acc, carries[tiles]

    _page_stream(pt_ref, (k_hbm, v_hbm), (kbuf, vbuf), (ksem, vsem), layer=layer, group=group, depth=depth,
                 reverse=True, consume=consume, init=(jnp.zeros((8, D_ATT), F32), jnp.zeros((8, 1), F32)))


def _page_buffers(depth, group):
    return pltpu.VMEM((depth, group, N_HEADS, HEAD_DIM, PAGE), F32)


def sb_decode(page_table, q, gact, cache_k, cache_v, layer, *, group, depth):
    b = page_table.shape[0]
    rows = pl.BlockSpec((b, 1, D_ATT), lambda i, pt: (0, 0, 0))
    hbm = pl.BlockSpec(memory_space=pl.ANY)
    grid_spec = pltpu.PrefetchScalarGridSpec(
        num_scalar_prefetch=1,
        grid=(1,),
        in_specs=[rows, rows, hbm, hbm],
        out_specs=rows,
        scratch_shapes=[_page_buffers(depth, group), _page_buffers(depth, group),
                        pltpu.SemaphoreType.DMA((depth,)), pltpu.SemaphoreType.DMA((depth,))],
    )
    return pl.pallas_call(
        functools.partial(_sb_decode_kernel, layer=layer, group=group, depth=depth),
        out_shape=jax.ShapeDtypeStruct((b, 1, D_ATT), BF16),
        grid_spec=grid_spec,
        compiler_params=_params(("arbitrary",)),
        name="sb_decode",
    )(page_table, q, gact, cache_k, cache_v)


def _moba_gate_kernel(pt_ref, q_ref, k_hbm, idx_ref, kbuf, ksem, gs_ref, *, layer, group, depth):
    per_chunk = group // 2
    last_chunk = pt_ref.shape[1] // group - 1

    def consume(seq, c, slot, carry):
        qc = q_ref[seq]
        for r in range(per_chunk):
            pair = kbuf[slot, 2 * r].reshape(D_ATT, PAGE) + kbuf[slot, 2 * r + 1].reshape(D_ATT, PAGE)
            prod = pair * qc
            for h in range(N_HEADS):
                gs_ref[h, pl.ds(c * per_chunk + r, 1), :] = jnp.sum(prod[h * HEAD_DIM:(h + 1) * HEAD_DIM], axis=0,
                                                                    keepdims=True)

        @pl.when(c == last_chunk)
        def _():
            nb = gs_ref.shape[1]
            lane = lax.broadcasted_iota(I32, (nb, 128), 1)
            gate = jnp.zeros((nb, 128), F32)
            for h in range(N_HEADS):
                gate = gate + jnp.where(lane == h, jnp.sum(gs_ref[h], axis=-1, keepdims=True) * (1.0 / KV_TILE), 0.0)
            blk = lax.broadcasted_iota(I32, (nb, 128), 0)
            rank = jnp.zeros((nb, 128), I32)
            for m in range(nb):
                gm = gate[m:m + 1, :]
                rank = rank + ((gm > gate) | ((gm == gate) & (blk > m))).astype(I32)
            blk_f = blk.astype(F32)
            rows = [jnp.sum(jnp.where(rank == t, blk_f, 0.0), axis=0, keepdims=True) for t in range(MOBA_TOPK)]
            idx_ref[seq] = jnp.concatenate(rows + [jnp.zeros((8 - MOBA_TOPK, 128), F32)], axis=0).astype(I32)

        return carry

    _page_stream(pt_ref, (k_hbm,), (kbuf,), (ksem,), layer=layer, group=group, depth=depth, reverse=False,
                 consume=consume, init=0)


def moba_decode_select(page_table, q_col, cache_k, layer, *, group, depth):
    b, n_pages = page_table.shape
    nb = n_pages * PAGE // KV_TILE
    grid_spec = pltpu.PrefetchScalarGridSpec(
        num_scalar_prefetch=1,
        grid=(1,),
        in_specs=[pl.BlockSpec((b, D_ATT, 1), lambda i, pt: (0, 0, 0)), pl.BlockSpec(memory_space=pl.ANY)],
        out_specs=pl.BlockSpec((b, 8, 128), lambda i, pt: (0, 0, 0)),
        scratch_shapes=[_page_buffers(depth, group), pltpu.SemaphoreType.DMA((depth,)),
                        pltpu.VMEM((N_HEADS, nb, PAGE), F32)],
    )
    return pl.pallas_call(
        functools.partial(_moba_gate_kernel, layer=layer, group=group, depth=depth),
        out_shape=jax.ShapeDtypeStruct((b, 8, 128), I32),
        grid_spec=grid_spec,
        compiler_params=_params(("arbitrary",)),
        name="moba_decode_select",
    )(page_table, q_col, cache_k)


def _moba_decode_kernel(pt_ref, sel_ref, q_ref, kn_ref, vn_ref, bias_ref, g_ref, k_hbm, v_hbm, o_ref,
                        kbuf, vbuf, ksem, vsem, *, layer, last_block):
    n_seq = pt_ref.shape[0]

    def piece(t, h, half):
        return (t * N_HEADS + h) * 2 + half

    def copies(seq, slot):
        out = []
        for t in range(MOBA_TOPK):
            for h in range(N_HEADS):
                blk = sel_ref[seq, t * N_HEADS + h]
                for half in range(2):
                    page = pt_ref[seq, 2 * blk + half]
                    i = piece(t, h, half)
                    out.append(pltpu.make_async_copy(k_hbm.at[layer, page, h], kbuf.at[slot, i], ksem.at[slot]))
                    out.append(pltpu.make_async_copy(v_hbm.at[layer, page, h], vbuf.at[slot, i], vsem.at[slot]))
        return out

    for cp in copies(0, 0):
        cp.start()

    def body(seq, carry):
        slot = lax.rem(seq, 2)

        @pl.when(seq + 1 < n_seq)
        def _():
            for cp in copies(seq + 1, 1 - slot):
                cp.start()

        for cp in copies(seq, slot):
            cp.wait()
        for h in range(N_HEADS):
            qf = jnp.broadcast_to(q_ref[seq, h].astype(F32), (8, HEAD_DIM))
            q = qf.astype(BF16)
            s_self = (jnp.sum(qf * kn_ref[seq, h].astype(BF16).astype(F32), axis=-1, keepdims=True)
                      + bias_ref[h, 2][:, 0:1])
            scores = []
            for t in range(MOBA_TOPK):
                kt = jnp.concatenate([kbuf[slot, piece(t, h, 0)], kbuf[slot, piece(t, h, 1)]], axis=1).astype(BF16)
                near = sel_ref[seq, t * N_HEADS + h] == last_block
                scores.append(_dot(q, kt) + bias_ref[h, jnp.where(near, 1, 0)])
            m = s_self
            for s in scores:
                m = jnp.maximum(m, jnp.max(s, axis=-1, keepdims=True))
            p_self = jnp.exp(s_self - m)
            l = p_self
            acc = p_self * vn_ref[seq, h].astype(BF16).astype(F32)
            for t, s in enumerate(scores):
                p = jnp.exp(s - m)
                l = l + jnp.sum(p, axis=-1, keepdims=True)
                vt = jnp.concatenate([vbuf[slot, piece(t, h, 0)], vbuf[slot, piece(t, h, 1)]], axis=1).astype(BF16)
                acc = acc + _dot_nt(p.astype(BF16), vt)
            o_ref[seq, h] = ((acc / l)[0:1] * g_ref[seq, h].astype(F32)).astype(BF16)
        return carry

    lax.fori_loop(0, n_seq, body, 0)


def moba_decode(page_table, sel, qh, k_new, v_new, dec_bias, gact_mb, cache_k, cache_v, layer):
    b, n_pages = page_table.shape
    last_block = n_pages * PAGE // KV_TILE - 1

    pieces = 2 * MOBA_TOPK * N_HEADS
    vec = pl.BlockSpec((b, N_HEADS, 1, HEAD_DIM), lambda i, pt, sl: (0, 0, 0, 0))
    hbm = pl.BlockSpec(memory_space=pl.ANY)
    grid_spec = pltpu.PrefetchScalarGridSpec(
        num_scalar_prefetch=2,
        grid=(1,),
        in_specs=[vec, vec, vec, pl.BlockSpec(dec_bias.shape, lambda i, pt, sl: (0, 0, 0, 0)), vec, hbm, hbm],
        out_specs=vec,
        scratch_shapes=[pltpu.VMEM((2, pieces, HEAD_DIM, PAGE), F32), pltpu.VMEM((2, pieces, HEAD_DIM, PAGE), F32),
                        pltpu.SemaphoreType.DMA((2,)), pltpu.SemaphoreType.DMA((2,))],
    )
    return pl.pallas_call(
        functools.partial(_moba_decode_kernel, layer=layer, last_block=last_block),
        out_shape=jax.ShapeDtypeStruct((b, N_HEADS, 1, HEAD_DIM), BF16),
        grid_spec=grid_spec,
        compiler_params=_params(("arbitrary",)),
        name="moba_decode",
    )(page_table, sel, qh, k_new, v_new, dec_bias, gact_mb, cache_k, cache_v)


def _mlstm_decode_kernel(q_ref, k_ref, v_ref, gt_ref, g_ref, nw_ref, c_ref, n_ref, m_ref,
                         h_ref, co_ref, no_ref, mo_ref):
    r = lax.broadcasted_iota(I32, (ML_D, ML_D), 0)
    c = lax.broadcasted_iota(I32, (ML_D, ML_D), 1)
    eye = r == c
    for h in range(ML_HEADS):
        sl = slice(h * ML_D, (h + 1) * ML_D)
        q = q_ref[:, sl].astype(F32)
        k = k_ref[:, sl].astype(F32)
        v = v_ref[:, sl].astype(F32)
        i_pre = gt_ref[:, h:h + 1]
        logf = gt_ref[:, ML_HEADS + h:ML_HEADS + h + 1]
        m_prev = m_ref[:, h:h + 1]
        cm = c_ref[h]
        nv = n_ref[h:h + 1, :]
        g = logf + m_prev
        m_new = jnp.maximum(g, i_pre)
        w_in = jnp.exp(i_pre - m_new)
        w_st = jnp.exp(g - m_new)
        qk = jnp.sum(q * k, axis=-1, keepdims=True)
        cq = _dot_nt(jnp.broadcast_to(q, (8, ML_D)), cm, precision=HI)[0:1]
        nq = jnp.sum(nv * q, axis=-1, keepdims=True)
        num = (qk * w_in) * v + w_st * cq
        den = qk * w_in + w_st * nq
        hout = num / jnp.maximum(jnp.abs(den), jnp.exp(-m_new))
        h_ref[:, sl] = _gated_head_norm(hout, nw_ref[:, sl], g_ref[:, sl])
        v_diag = jnp.where(eye, jnp.broadcast_to(v, (ML_D, ML_D)), 0.0)
        outer = jnp.dot(v_diag, jnp.broadcast_to(k, (ML_D, ML_D)), precision=HI, preferred_element_type=F32)
        co_ref[h] = w_st * cm + w_in * outer
        no_ref[h:h + 1, :] = w_st * nv + w_in * k
        mo_ref[:, h:h + 1] = m_new


def mlstm_decode(q, k, v, gates, gact, norm_w, c0, n0, m0, layer):
    b = q.shape[0]
    vec = pl.BlockSpec((None, 1, D_ML), lambda bi: (bi, 0, 0))
    cs = pl.BlockSpec((None, ML_HEADS, ML_D, ML_D), lambda bi: (bi, 0, 0, 0))
    ns = pl.BlockSpec((None, ML_HEADS, ML_D), lambda bi: (bi, 0, 0))
    msp = pl.BlockSpec((None, 1, ML_HEADS), lambda bi: (bi, 0, 0))
    return pl.pallas_call(
        _mlstm_decode_kernel,
        out_shape=(jax.ShapeDtypeStruct((b, 1, D_ML), BF16),
                   jax.ShapeDtypeStruct(c0.shape[1:], F32),
                   jax.ShapeDtypeStruct(n0.shape[1:], F32),
                   jax.ShapeDtypeStruct(m0.shape, F32)),
        grid=(b,),
        in_specs=[vec, vec, vec,
                  pl.BlockSpec((None, 1, 2 * ML_HEADS), lambda bi: (bi, 0, 0)),
                  pl.BlockSpec((None, 1, D_ML), lambda bi: (bi, 0, 1)),
                  pl.BlockSpec((1, D_ML), lambda bi: (0, 0)),
                  pl.BlockSpec((None, None, ML_HEADS, ML_D, ML_D), lambda bi: (layer, bi, 0, 0, 0)),
                  pl.BlockSpec((None, None, ML_HEADS, ML_D), lambda bi: (layer, bi, 0, 0)),
                  msp],
        out_specs=(vec, cs, ns, msp),
        compiler_params=_params(("parallel",)),
        name="mlstm_decode",
    )(q, k, v, gates, gact, norm_w, c0, n0, m0)


SB_DECODE_GROUP = 16
MOBA_SELECT_GROUP = 32
STREAM_DEPTH = 4
PROMPT_TM = 512


def _bf16_weights(w_in, w_out):
    w_t = jnp.swapaxes(w_in, 1, 2)
    w_bf = w_t.astype(BF16)
    w_lo = (w_t[:, O_QMB:O_VMB] - w_bf[:, O_QMB:O_VMB].astype(F32)).astype(BF16)
    return w_bf, w_lo, w_out.astype(BF16)


def kernel(x_prompt, x_sample, cache_sb_k, cache_sb_v, cache_moba_k, cache_moba_v, state_mlstm_C, state_mlstm_n,
           state_mlstm_m, page_table, c_prompt, c_sample, norm_w, w_ada, b_ada, w_in, b_gates, q_norm_w, k_norm_w,
           rel_bias, ml_norm_w, w_out):
    bp, tp, d = x_prompt.shape
    bs = x_sample.shape[0]
    depth = w_in.shape[0]
    mp = bp * tp
    n_pages = page_table.shape[1]

    mod = ada_modulation(jnp.concatenate([c_prompt, c_sample], axis=0), w_ada, b_ada)
    bias_p, bias_d = bias_tiles(rel_bias)
    lanes_last = lambda a: jnp.transpose(a, (0, 1, 3, 4, 2))
    sbk, sbv, mbk, mbv = (lanes_last(a) for a in (cache_sb_k, cache_sb_v, cache_moba_k, cache_moba_v))

    xp = x_prompt.reshape(mp, d)
    xs = x_sample.reshape(bs, d)
    p_out = [[] for _ in range(3)]
    s_out = [[] for _ in range(7)]
    kv_pages = ()
    w_bf, w_lo, wo_bf = _bf16_weights(w_in, w_out)
    for l in range(depth):
        nw = norm_w[l].reshape(1, d)
        bg = b_gates[l].reshape(1, 2 * ML_HEADS)
        qnw = jnp.tile(q_norm_w[l], N_HEADS).reshape(1, D_ATT)
        knw = jnp.tile(k_norm_w[l], N_HEADS).reshape(1, D_ATT)
        mlw = ml_norm_w[l].reshape(1, D_ML)
        shift, scale, gate = jnp.split(mod[l], 3, axis=-1)

        pm = lambda a: a[:bp].reshape(bp, 1, d)
        (qsb_t, ksb_t, ksb, vsb_t, vsb_th, qmb_t, qmb_th, kmb_t, kmb, vmb_t, vmb_th,
         qml, kml, vml, gact, gates, kmean) = \
            input_projection(xp, nw, pm(scale), pm(shift), w_bf, w_lo, bg, qnw, knw,
                             tm=PROMPT_TM, rows_per_mod=tp // PROMPT_TM, prompt=True,
                             layer=l, depth=depth, kv_carried=kv_pages)
        kv_pages = (ksb_t, vsb_t, kmb_t, vmb_t)
        seq = lambda a: a.reshape(bp, tp, a.shape[-1])
        pages = lambda a: a.reshape(bp, tp // PAGE, D_ATT, PAGE)
        gact3 = seq(gact)
        ysb = sb_prompt(pages(qsb_t), seq(ksb), pages(vsb_th), gact3)
        ymb = moba_prompt(pages(qmb_t), pages(qmb_th), seq(kmb), pages(vmb_th),
                          kmean.reshape(bp, tp // KV_TILE, D_ATT), bias_p, gact3)
        gates3 = seq(gates)
        ml_pages = lambda a: a.reshape(bp, tp // ML_CHUNK, D_ML, ML_CHUNK)
        yml, c_p, n_p, m_p = mlstm_prompt(ml_pages(qml), seq(kml), ml_pages(vml), gates3, gates3.swapaxes(1, 2),
                                          gact3, mlw)
        xp = output_projection(ysb.reshape(mp, D_ATT), ymb.reshape(mp, D_ATT), yml.reshape(mp, D_ML), xp, pm(gate),
                               wo_bf, tm=PROMPT_TM, rows_per_mod=tp // PROMPT_TM, layer=l)
        for lst, a in zip(p_out, (c_p, n_p, m_p.reshape(bp, ML_HEADS))):
            lst.append(a)

        sm = lambda a: a[bp:].reshape(1, bs, d)
        (qsb, ksb, vsb, qmb, qmbh, kmb, vmb, qml, kml, vml, gact, gates) = \
            input_projection(xs, nw, sm(scale), sm(shift), w_bf, w_lo, bg, qnw, knw,
                             tm=bs, rows_per_mod=1, prompt=False, layer=l)
        tok = lambda a: a.reshape(bs, 1, a.shape[-1])
        heads = lambda a: a.reshape(bs, N_HEADS, 1, HEAD_DIM)
        gact3 = tok(gact)
        ysb = sb_decode(page_table, tok(qsb), gact3[:, :, :D_ATT], sbk, sbv, l, group=min(SB_DECODE_GROUP, n_pages),
                        depth=STREAM_DEPTH)
        sel = moba_decode_select(page_table, qmb.reshape(bs, D_ATT, 1), mbk, l,
                                 group=min(MOBA_SELECT_GROUP, n_pages), depth=STREAM_DEPTH)
        sel = sel[:, :MOBA_TOPK, :N_HEADS].reshape(bs, MOBA_TOPK * N_HEADS)
        ymb = moba_decode(page_table, sel, heads(qmbh), heads(kmb), heads(vmb), bias_d,
                          heads(gact[:, D_ATT:2 * D_ATT]), mbk, mbv, l)
        yml, c_s, n_s, m_s = mlstm_decode(tok(qml), tok(kml), tok(vml), tok(gates), gact3, mlw,
                                          state_mlstm_C, state_mlstm_n, state_mlstm_m[l].reshape(bs, 1, ML_HEADS), l)
        xs = output_projection(ysb.reshape(bs, D_ATT), ymb.reshape(bs, D_ATT), yml.reshape(bs, D_ML), xs, sm(gate),
                               wo_bf, tm=bs, rows_per_mod=1, layer=l)
        new = lambda a: a.reshape(bs, 1, N_HEADS, HEAD_DIM)
        for lst, a in zip(s_out, (new(ksb), new(vsb), new(kmb), new(vmb), c_s, n_s, m_s.reshape(bs, ML_HEADS))):
            lst.append(a)

    paged = lambda a: jnp.transpose(a.reshape(depth, bp, tp // PAGE, N_HEADS, HEAD_DIM, PAGE), (0, 1, 2, 5, 3, 4))
    P = [paged(a) for a in kv_pages] + [jnp.stack(a) for a in p_out]
    S = [jnp.stack(a) for a in s_out]
    return (xp.reshape(bp, tp, d), xs.reshape(bs, 1, d),
            P[0], P[1], P[2], P[3], P[4], P[5], P[6], S[0], S[1], S[2], S[3], S[4], S[5], S[6])
```

```python
import functools
import math

import numpy as np
import jax
import jax.numpy as jnp
from jax import lax
from jax.experimental import pallas as pl
from jax.experimental.pallas import tpu as pltpu

F32 = jnp.float32
BF16 = jnp.bfloat16
I32 = jnp.int32
HI = lax.Precision.HIGHEST

EPS = 1e-6
HEAD_DIM = 64
N_HEADS = 4
D_ATT = N_HEADS * HEAD_DIM
ML_HEADS = 4
ML_D = 128
D_ML = ML_HEADS * ML_D
Q_BLOCK = 128
KV_TILE = 256
MOBA_TOPK = 3
N_BUCKETS = 32
MAX_DISTANCE = 128
ML_CHUNK = 128
PAGE = 128
HEAD_SHIFT = 6
NEG_INF = float("-inf")

VMEM_LIMIT = 56 * 1024 * 1024

_SIZES = [D_ATT] * 8 + [D_ML] * 5 + [ML_HEADS] * 2
_OFF = np.concatenate([[0], np.cumsum(_SIZES)]).tolist()
(O_QSB, O_KSB, O_VSB, O_GSB, O_QMB, O_KMB, O_VMB, O_GMB,
 O_QML, O_KML, O_VML, O_OML, O_GML, O_IML, O_FML, O_END) = _OFF


def _params(sem, vmem=VMEM_LIMIT):
    return pltpu.CompilerParams(dimension_semantics=sem, vmem_limit_bytes=vmem)


def _dot(a, b):
    return jnp.dot(a, b, preferred_element_type=F32)


def _dot_nt(a, b, precision=None):
    return lax.dot_general(a, b, (((1,), (1,)), ((), ())), precision=precision,
                           preferred_element_type=F32)


SOFTPLUS_CLAMP = 64.0


def _softplus(z):
    return jnp.maximum(jnp.log(1.0 + jnp.exp(jnp.minimum(z, SOFTPLUS_CLAMP))), z)


def _head_of_lane(shape, axis):
    return lax.shift_right_logical(lax.broadcasted_iota(I32, shape, axis), HEAD_SHIFT)


def _ada_kernel(c_ref, w_ref, b_ref, o_ref):
    c = c_ref[...]
    a = c * jax.nn.sigmoid(c)
    o_ref[...] = jnp.dot(a, w_ref[...], precision=HI, preferred_element_type=F32) + b_ref[...]


def ada_modulation(c_all, w_ada, b_ada):
    depth, d, d3 = w_ada.shape
    rows = c_all.shape[0]
    nt = d3 // d
    return pl.pallas_call(
        _ada_kernel,
        out_shape=jax.ShapeDtypeStruct((depth, rows, d3), F32),
        grid=(depth, nt),
        in_specs=[pl.BlockSpec((rows, d), lambda l, j: (0, 0)),
                  pl.BlockSpec((None, d, d), lambda l, j: (l, 0, j)),
                  pl.BlockSpec((None, 1, d), lambda l, j: (l, 0, j))],
        out_specs=pl.BlockSpec((None, rows, d), lambda l, j: (l, 0, j)),
        compiler_params=_params(("parallel", "parallel")),
        name="ada_modulation",
    )(c_all, w_ada, b_ada.reshape(depth, 1, d3))


def _bucket_thresholds():
    d = np.arange(0, 4 * MAX_DISTANCE, dtype=np.int32)
    max_exact = N_BUCKETS // 2
    df = np.maximum(d, 1).astype(np.float32)
    large = max_exact + (np.log(df / np.float32(max_exact)) / np.float32(math.log(MAX_DISTANCE / max_exact))
                         * np.float32(N_BUCKETS - max_exact)).astype(np.int32)
    large = np.minimum(large, N_BUCKETS - 1)
    bucket = np.where(d < max_exact, d, large)
    thr = []
    for k in range(1, N_BUCKETS):
        idx = np.nonzero(bucket >= k)[0]
        thr.append(int(idx[0]))
    assert all(np.all((bucket >= k) == (d >= t)) for k, t in zip(range(1, N_BUCKETS), thr))
    return thr


_BUCKET_THR = _bucket_thresholds()

_PATTERN_OFFSETS = (0, Q_BLOCK, KV_TILE)
P_FAR = len(_PATTERN_OFFSETS)


def _bias_from_dist(dist, rb_ref, h):
    b = jnp.full(dist.shape, rb_ref[0, h], F32)
    for k, t in zip(range(1, N_BUCKETS), _BUCKET_THR):
        b = jnp.where(dist >= t, rb_ref[k, h], b)
    return jnp.where(dist >= 0, b, NEG_INF)


def _bias_kernel(rb_ref, tile_ref, dec_ref):
    j = lax.broadcasted_iota(I32, (KV_TILE, Q_BLOCK), 0)
    i = lax.broadcasted_iota(I32, (KV_TILE, Q_BLOCK), 1)
    for p, off in enumerate(_PATTERN_OFFSETS):
        for h in range(N_HEADS):
            tile_ref[p, :, h * Q_BLOCK:(h + 1) * Q_BLOCK] = _bias_from_dist(off + i - j, rb_ref, h)
    for h in range(N_HEADS):
        tile_ref[P_FAR, :, h * Q_BLOCK:(h + 1) * Q_BLOCK] = jnp.full((KV_TILE, Q_BLOCK), rb_ref[N_BUCKETS - 1, h], F32)
    jr = lax.broadcasted_iota(I32, (8, KV_TILE), 1)
    for h in range(N_HEADS):
        dec_ref[h, 0] = jnp.full((8, KV_TILE), rb_ref[N_BUCKETS - 1, h], F32)
        dec_ref[h, 1] = _bias_from_dist(KV_TILE - jr, rb_ref, h)
        dec_ref[h, 2] = jnp.full((8, KV_TILE), rb_ref[0, h], F32)


def bias_tiles(rel_bias):
    return pl.pallas_call(
        _bias_kernel,
        out_shape=(jax.ShapeDtypeStruct((P_FAR + 1, KV_TILE, N_HEADS * Q_BLOCK), F32),
                   jax.ShapeDtypeStruct((N_HEADS, 3, 8, KV_TILE), F32)),
        in_specs=[pl.BlockSpec(memory_space=pltpu.SMEM)],
        name="bias_tiles",
    )(rel_bias)


def _inproj_kernel(x_ref, nw_ref, sc_ref, sh_ref, w_ref, wlo_ref, bg_ref, qnw_ref, knw_ref, *outs, prompt, layer,
                   n_carried):
    outs = list(outs[n_carried:])

    def paged(p):
        return [p[pg * PAGE:(pg + 1) * PAGE].T for pg in range(p.shape[0] // PAGE)]

    def emit(p, *, rows_bf16=False, rows_f32=False, pages_f32=False, pages_bf16=False):
        pages = paged(p) if (pages_f32 or pages_bf16) else None
        for want, dtype, is_pages in ((pages_f32, F32, True), (rows_f32, F32, False),
                                      (pages_bf16, BF16, True), (rows_bf16, BF16, False)):
            if not want:
                continue
            ref = outs.pop(0)
            if is_pages and len(ref.shape) == 4:
                for slab in range(ref.shape[0]):
                    for pg, t in enumerate(pages):
                        ref[slab, pg] = t.astype(dtype) if slab == layer else jnp.zeros_like(t, dtype)
            elif is_pages:
                for pg, t in enumerate(pages):
                    ref[pg] = t.astype(dtype)
            else:
                ref[...] = p.astype(dtype)

    x = x_ref[...]
    ms = jnp.mean(x * x, axis=-1, keepdims=True)
    h = (x * lax.rsqrt(ms + EPS) * nw_ref[...]) * (1.0 + sc_ref[...]) + sh_ref[...]
    hb = h.astype(BF16)
    hlo = (h - hb.astype(F32)).astype(BF16)

    def proj(a, b):
        return _dot_nt(hb, w_ref[a:b, :])

    def proj3(a, b, la, lb):
        wh = w_ref[a:b, :]
        return _dot_nt(hb, wh) + (_dot_nt(hlo, wh) + _dot_nt(hb, wlo_ref[la:lb, :]))

    def silu(g):
        return g * jax.nn.sigmoid(g)

    r = lax.broadcasted_iota(I32, (D_ATT, D_ATT), 0)
    c = lax.broadcasted_iota(I32, (D_ATT, D_ATT), 1)
    same_head = lax.shift_right_logical(r, HEAD_SHIFT) == lax.shift_right_logical(c, HEAD_SHIFT)
    head_mean = jnp.where(same_head, 1.0 / HEAD_DIM, 0.0).astype(BF16)

    def head_norm(p, w):
        sq = p * p
        hi = sq.astype(BF16)
        lo = (sq - hi.astype(F32)).astype(BF16)
        ms = _dot(hi, head_mean) + _dot(lo, head_mean)
        return p * lax.rsqrt(ms + EPS) * w

    score_scale = HEAD_DIM ** -0.5
    emit(proj(O_QSB, O_KSB) * score_scale, pages_bf16=prompt, rows_bf16=not prompt)
    emit(proj(O_KSB, O_VSB), pages_f32=prompt, rows_bf16=prompt, rows_f32=not prompt)
    emit(proj(O_VSB, O_GSB), pages_f32=prompt, pages_bf16=prompt, rows_f32=not prompt)
    g_sb = silu(proj(O_GSB, O_QMB))
    qn = head_norm(proj3(O_QMB, O_KMB, 0, D_ATT), qnw_ref[...])
    if prompt:
        qt = paged(qn)
        ref_f, ref_h = outs.pop(0), outs.pop(0)
        for pg, t in enumerate(qt):
            ref_f[pg] = t
            ref_h[pg] = (t * score_scale).astype(BF16)
    else:
        emit(qn, rows_f32=True)
        emit(qn * score_scale, rows_bf16=True)
    kn = head_norm(proj3(O_KMB, O_VMB, D_ATT, 2 * D_ATT), knw_ref[...])
    emit(kn, pages_f32=prompt, rows_bf16=prompt, rows_f32=not prompt)
    emit(proj(O_VMB, O_GMB), pages_f32=prompt, pages_bf16=prompt, rows_f32=not prompt)
    g_mb = silu(proj(O_GMB, O_QML))
    emit(proj(O_QML, O_KML), pages_bf16=prompt, rows_bf16=not prompt)
    emit(proj(O_KML, O_VML) * (ML_D ** -0.5), rows_bf16=True)
    emit(proj(O_VML, O_OML), pages_bf16=prompt, rows_bf16=not prompt)
    o = proj(O_OML, O_GML)
    g_ml = jax.nn.sigmoid(o) * silu(proj(O_GML, O_IML))
    gact_ref = outs.pop(0)
    gact_ref[:, 0:D_ATT] = g_sb.astype(BF16)
    gact_ref[:, D_ATT:2 * D_ATT] = g_mb.astype(BF16)
    gact_ref[:, 2 * D_ATT:] = g_ml.astype(BF16)
    pre = proj(O_IML, O_END) + bg_ref[...]
    is_f = lax.broadcasted_iota(I32, pre.shape, 1) >= ML_HEADS
    logf = jnp.minimum(pre, 0.0) - jnp.log1p(jnp.exp(-jnp.abs(pre)))
    outs.pop(0)[...] = jnp.where(is_f, logf, pre)
    if prompt:
        kmean_ref = outs.pop(0)
        for i in range(kn.shape[0] // KV_TILE):
            kmean_ref[i] = jnp.mean(kn[i * KV_TILE:(i + 1) * KV_TILE], axis=0, keepdims=True)
    assert not outs


def input_projection(x, norm_w, scale, shift, w_bf, w_lo, b_gates, qnw, knw, *, tm, rows_per_mod, prompt,
                     layer=0, depth=1, kv_carried=()):
    m, d = x.shape
    n_tiles = m // tm
    rmod = scale.shape[1]
    row = lambda i: (i, 0)
    const = lambda i: (0, 0)
    mod = lambda i: (i // rows_per_mod, 0, 0)

    def rows(width, dtype):
        return jax.ShapeDtypeStruct((m, width), dtype), pl.BlockSpec((tm, width), row)

    def pages(dtype, width=D_ATT):
        return (jax.ShapeDtypeStruct((m // PAGE, width, PAGE), dtype),
                pl.BlockSpec((tm // PAGE, width, PAGE), lambda i: (i, 0, 0)))

    def kv_pages():
        shape = jax.ShapeDtypeStruct((depth, m // PAGE, D_ATT, PAGE), F32)
        if kv_carried:
            return shape, pl.BlockSpec((None, tm // PAGE, D_ATT, PAGE), lambda i: (layer, i, 0, 0))
        return shape, pl.BlockSpec((depth, tm // PAGE, D_ATT, PAGE), lambda i: (0, i, 0, 0))

    kv_out_index = (1, 3, 7, 9)
    if prompt:
        nb = tm // KV_TILE
        outs = [pages(BF16), kv_pages(), rows(D_ATT, BF16), kv_pages(), pages(BF16),
                pages(F32), pages(BF16), kv_pages(), rows(D_ATT, BF16), kv_pages(), pages(BF16)]
        tail = [(jax.ShapeDtypeStruct((n_tiles, nb, 1, D_ATT), F32),
                 pl.BlockSpec((None, nb, 1, D_ATT), lambda i: (i, 0, 0, 0)))]
    else:
        outs = [rows(D_ATT, BF16), rows(D_ATT, F32), rows(D_ATT, F32),
                rows(D_ATT, F32), rows(D_ATT, BF16), rows(D_ATT, F32), rows(D_ATT, F32)]
        tail = []
    ml_qv = pages(BF16, D_ML) if prompt else rows(D_ML, BF16)
    outs += [ml_qv, rows(D_ML, BF16), ml_qv,
             rows(2 * D_ATT + D_ML, BF16), rows(2 * ML_HEADS, F32)] + tail
    n_in = 9
    return pl.pallas_call(
        functools.partial(_inproj_kernel, prompt=prompt, layer=layer, n_carried=len(kv_carried)),
        out_shape=tuple(o[0] for o in outs),
        input_output_aliases={n_in + j: kv_out_index[j] for j in range(len(kv_carried))},
        grid=(n_tiles,),
        in_specs=[pl.BlockSpec((tm, d), row),
                  pl.BlockSpec((1, d), const),
                  pl.BlockSpec((None, rmod, d), mod),
                  pl.BlockSpec((None, rmod, d), mod),
                  pl.BlockSpec((None,) + w_bf.shape[1:], lambda i: (layer, 0, 0)),
                  pl.BlockSpec((None,) + w_lo.shape[1:], lambda i: (layer, 0, 0)),
                  pl.BlockSpec((1, 2 * ML_HEADS), const),
                  pl.BlockSpec((1, D_ATT), const),
                  pl.BlockSpec((1, D_ATT), const)] + [pl.BlockSpec(memory_space=pl.ANY)] * len(kv_carried),
        out_specs=tuple(o[1] for o in outs),
        compiler_params=_params(("parallel",)),
        name="input_projection",
    )(x, norm_w, scale, shift, w_bf, w_lo, b_gates, qnw, knw, *kv_carried)


def _outproj_kernel(ysb_ref, ymb_ref, yml_ref, x_ref, gate_ref, w_ref, o_ref):
    y = (_dot(ysb_ref[...], w_ref[0:D_ATT, :]) + _dot(ymb_ref[...], w_ref[D_ATT:2 * D_ATT, :])
         + _dot(yml_ref[...], w_ref[2 * D_ATT:, :]))
    o_ref[...] = x_ref[...] + gate_ref[...] * y


def output_projection(ysb, ymb, yml, x, gate, w_bf, *, tm, rows_per_mod, layer):
    m, d = x.shape
    rmod = gate.shape[1]
    row = lambda i: (i, 0)
    return pl.pallas_call(
        _outproj_kernel,
        out_shape=jax.ShapeDtypeStruct((m, d), F32),
        grid=(m // tm,),
        in_specs=[pl.BlockSpec((tm, D_ATT), row), pl.BlockSpec((tm, D_ATT), row), pl.BlockSpec((tm, D_ML), row),
                  pl.BlockSpec((tm, d), row),
                  pl.BlockSpec((None, rmod, d), lambda i: (i // rows_per_mod, 0, 0)),
                  pl.BlockSpec((None,) + w_bf.shape[1:], lambda i: (layer, 0, 0))],
        out_specs=pl.BlockSpec((tm, d), row),
        compiler_params=_params(("parallel",)),
        name="output_projection",
    )(ysb, ymb, yml, x, gate, w_bf)


def _suffix_matrix(n, transposed=False):
    r = lax.broadcasted_iota(I32, (n, n), 0)
    c = lax.broadcasted_iota(I32, (n, n), 1)
    return jnp.where((c > r) if transposed else (r > c), 1.0, 0.0).astype(BF16)


def _heads_on_lanes(qt):
    head = _head_of_lane((D_ATT, 1), 0)
    return jnp.concatenate([jnp.where(head == h, qt, jnp.zeros_like(qt)) for h in range(N_HEADS)], axis=1)


def _heads_from_lanes(acc_t, q):
    head = _head_of_lane((D_ATT, 1), 0)
    out = jnp.zeros((D_ATT, q), F32)
    for h in range(N_HEADS):
        out = out + jnp.where(head == h, acc_t[:, h * q:(h + 1) * q], 0.0)
    return out.T


def _kv_tile(k_ref, vt_ref, n, first_page_only=False):
    start = pl.multiple_of(n * KV_TILE, KV_TILE)
    if first_page_only:
        return k_ref[pl.ds(start, PAGE), :], vt_ref[2 * n]
    vt = jnp.concatenate([vt_ref[2 * n], vt_ref[2 * n + 1]], axis=1)
    return k_ref[pl.ds(start, KV_TILE), :], vt


def _sb_tile_t(q4t, k, vt, upper, carry, mask):
    z = _dot(k, q4t)
    sp = _softplus(z)
    if mask is not None:
        sp_sum = jnp.where(mask, sp, 0.0)
    else:
        sp_sum = sp
    keys = z.shape[0]
    later = _dot(upper[:keys, :keys], sp_sum.astype(BF16))
    w = jnp.exp(((z - sp) - later) - carry)
    if mask is not None:
        w = jnp.where(mask, w, 0.0)
    return _dot(vt, w.astype(BF16)), carry + jnp.sum(sp_sum, axis=0, keepdims=True)


def _sb_pair_t(q4t, tile_a, tile_b, upper, carry, mask_a):
    (ka, vta), (kb, vtb) = tile_a, tile_b
    za, zb = _dot(ka, q4t), _dot(kb, q4t)
    spa, spb = _softplus(za), _softplus(zb)
    sum_a = spa if mask_a is None else jnp.where(mask_a, spa, 0.0)
    keys_a = za.shape[0]
    la, lb = _dot(upper[:keys_a, :keys_a], sum_a.astype(BF16)), _dot(upper, spb.astype(BF16))
    carry_b = carry + jnp.sum(sum_a, axis=0, keepdims=True)
    wa = jnp.exp(((za - spa) - la) - carry)
    if mask_a is not None:
        wa = jnp.where(mask_a, wa, 0.0)
    wb = jnp.exp(((zb - spb) - lb) - carry_b)
    pv = _dot(vta, wa.astype(BF16)) + _dot(vtb, wb.astype(BF16))
    return pv, carry_b + jnp.sum(spb, axis=0, keepdims=True)


def _sb_prompt_kernel(qt_ref, k_ref, vt_ref, g_ref, o_ref, acc_ref, carry_ref):
    i = pl.program_id(1)
    cols = N_HEADS * Q_BLOCK
    q4t = _heads_on_lanes(qt_ref[...])
    upper = _suffix_matrix(KV_TILE, transposed=True)
    last = (i * Q_BLOCK) // KV_TILE
    zero = jnp.zeros((1, cols), F32)

    for opens_tile in (True, False):
        keys = PAGE if opens_tile else KV_TILE
        kpos = last * KV_TILE + lax.broadcasted_iota(I32, (keys, cols), 0)
        qpos = i * Q_BLOCK + (lax.broadcasted_iota(I32, (keys, cols), 1) & (Q_BLOCK - 1))
        mask = kpos < qpos
        here = ((i * Q_BLOCK) % KV_TILE == 0) == opens_tile

        @pl.when(here & (last % 2 == 0))
        def _():
            k, vt = _kv_tile(k_ref, vt_ref, last, opens_tile)
            acc_ref[...], carry_ref[...] = _sb_tile_t(q4t, k, vt, upper, zero, mask)

        @pl.when(here & (last % 2 == 1))
        def _():
            acc_ref[...], carry_ref[...] = _sb_pair_t(q4t, _kv_tile(k_ref, vt_ref, last, opens_tile),
                                                      _kv_tile(k_ref, vt_ref, last - 1), upper, zero, mask)

    first = last - 1 - last % 2

    def body(s, carry):
        pv, carry = _sb_pair_t(q4t, _kv_tile(k_ref, vt_ref, first - 2 * s), _kv_tile(k_ref, vt_ref, first - 2 * s - 1),
                               upper, carry, None)
        acc_ref[...] += pv
        return carry

    lax.fori_loop(0, last // 2, body, carry_ref[...])
    o_ref[...] = (_heads_from_lanes(acc_ref[...], Q_BLOCK) * g_ref[...].astype(F32)).astype(BF16)


def sb_prompt(qt, k, vt, gact):
    b, t, _ = k.shape
    n_pages = t // PAGE
    blk = pl.BlockSpec((None, Q_BLOCK, D_ATT), lambda bi, i: (bi, i, 0))
    return pl.pallas_call(
        _sb_prompt_kernel,
        out_shape=jax.ShapeDtypeStruct((b, t, D_ATT), BF16),
        grid=(b, t // Q_BLOCK),
        in_specs=[pl.BlockSpec((None, None, D_ATT, PAGE), lambda bi, i: (bi, i, 0, 0)),
                  pl.BlockSpec((None, t, D_ATT), lambda bi, i: (bi, 0, 0)),
                  pl.BlockSpec((None, n_pages, D_ATT, PAGE), lambda bi, i: (bi, 0, 0, 0)),
                  blk],
        out_specs=blk,
        scratch_shapes=[pltpu.VMEM((D_ATT, N_HEADS * Q_BLOCK), F32), pltpu.VMEM((1, N_HEADS * Q_BLOCK), F32)],
        compiler_params=_params(("parallel", "arbitrary")),
        name="sb_prompt",
    )(qt, k, vt, gact)


def _moba_select_t(gate, valid, idx):
    n = gate.shape[0]
    gate = jnp.where(valid, gate, NEG_INF)
    rank = jnp.zeros(gate.shape, I32)
    for m in range(n):
        gm = gate[m:m + 1, :]
        beats = (gm > gate) | ((gm == gate) & (idx > m))
        rank = rank + beats.astype(I32)
    return valid & (rank < MOBA_TOPK)


def _moba_prompt_kernel(qft_ref, qht_ref, k_ref, vt_ref, km_ref, bias_ref, g_ref, o_ref, acc_ref, m_ref, l_ref, sel_ref):
    i = pl.program_id(1)
    nb = km_ref.shape[0]
    own = (i * Q_BLOCK) // KV_TILE
    odd = (i * Q_BLOCK) % KV_TILE != 0
    q4t = _heads_on_lanes(qht_ref[...])
    gate = jnp.dot(km_ref[...], _heads_on_lanes(qft_ref[...]), precision=HI, preferred_element_type=F32)
    idx = lax.broadcasted_iota(I32, (nb, 1), 0)
    sel = _moba_select_t(gate, idx < own, idx)
    sel_ref[...] = jnp.where(sel, 0.0, NEG_INF)

    def attend(blocks, first=False):
        tiles = [_kv_tile(k_ref, vt_ref, n) for n, _ in blocks]
        scores = []
        for (k, _), (_, terms) in zip(tiles, blocks):
            s = _dot(k, q4t)
            for t in terms:
                s = s + t
            scores.append(s)
        m_new = functools.reduce(jnp.maximum, [jnp.max(s, axis=0, keepdims=True) for s in scores])
        if not first:
            m_old = m_ref[...]
            m_new = jnp.maximum(m_old, m_new)
            alpha = jnp.exp(m_old - m_new)
        ps = [jnp.exp(s - m_new) for s in scores]
        l_new = functools.reduce(jnp.add, [jnp.sum(p, axis=0, keepdims=True) for p in ps])
        pv = functools.reduce(jnp.add, [_dot(vt, p.astype(BF16)) for (_, vt), p in zip(tiles, ps)])
        m_ref[...] = m_new
        l_ref[...] = l_new if first else alpha * l_ref[...] + l_new
        acc_ref[...] = pv if first else alpha * acc_ref[...] + pv

    attend([(own, [bias_ref[jnp.where(odd, 1, 0)]])], first=True)
    near_pat = jnp.where(odd, P_FAR, 2)
    far_row = bias_ref[P_FAR, 0:1, :]
    for n in range(0, nb - 2, 2):
        @pl.when(n + 1 < own)
        def _():
            attend([(n, [far_row + sel_ref[n:n + 1, :]]),
                    (n + 1, [bias_ref[jnp.where(n + 1 == own - 1, near_pat, P_FAR)], sel_ref[n + 1:n + 2, :]])])

    @pl.when(own % 2 == 1)
    def _():
        attend([(own - 1, [bias_ref[near_pat], sel_ref[pl.ds(own - 1, 1), :]])])

    y = _heads_from_lanes(acc_ref[...] / l_ref[...], Q_BLOCK)
    o_ref[...] = (y * g_ref[...].astype(F32)).astype(BF16)


def moba_prompt(qft, qht, k, vt, kmean, bias, gact):
    b, t, _ = k.shape
    nb = t // KV_TILE
    cols = N_HEADS * Q_BLOCK
    blk = pl.BlockSpec((None, Q_BLOCK, D_ATT), lambda bi, i: (bi, i, 0))
    qpage = pl.BlockSpec((None, None, D_ATT, PAGE), lambda bi, i: (bi, i, 0, 0))
    return pl.pallas_call(
        _moba_prompt_kernel,
        out_shape=jax.ShapeDtypeStruct((b, t, D_ATT), BF16),
        grid=(b, t // Q_BLOCK),
        in_specs=[qpage, qpage,
                  pl.BlockSpec((None, t, D_ATT), lambda bi, i: (bi, 0, 0)),
                  pl.BlockSpec((None, t // PAGE, D_ATT, PAGE), lambda bi, i: (bi, 0, 0, 0)),
                  pl.BlockSpec((None, nb, D_ATT), lambda bi, i: (bi, 0, 0)),
                  pl.BlockSpec(bias.shape, lambda bi, i: (0, 0, 0)),
                  pl.BlockSpec((None, Q_BLOCK, D_ATT), lambda bi, i: (bi, i, 1))],
        out_specs=blk,
        scratch_shapes=[pltpu.VMEM((D_ATT, cols), F32), pltpu.VMEM((1, cols), F32), pltpu.VMEM((1, cols), F32),
                        pltpu.VMEM((nb, cols), F32)],
        compiler_params=_params(("parallel", "arbitrary")),
        name="moba_prompt",
    )(qft, qht, k, vt, kmean, bias, gact)


def _gated_head_norm(h, w, g):
    hn = h * lax.rsqrt(jnp.mean(h * h, axis=-1, keepdims=True) + EPS) * w
    return (hn * g.astype(F32)).astype(BF16)


def _mlstm_prompt_kernel(qt_ref, k_ref, vt_ref, gc_ref, gr_ref, g_ref, nw_ref, h_ref, c_ref, n_ref, m_ref,
                         cst_ref, nst_ref, ms_ref):
    ci = pl.program_id(1)
    L = ML_CHUNK

    @pl.when(ci == 0)
    def _():
        cst_ref[...] = jnp.zeros_like(cst_ref)
        nst_ref[...] = jnp.zeros_like(nst_ref)
        ms_ref[...] = jnp.zeros_like(ms_ref)

    r = lax.broadcasted_iota(I32, (L, L), 0)
    c = lax.broadcasted_iota(I32, (L, L), 1)
    lower = jnp.where(c <= r, 1.0, 0.0).astype(F32)
    upper = jnp.where(r <= c, 1.0, 0.0).astype(F32)
    src_before_tgt = r <= c
    gc = gc_ref[...]
    gr = gr_ref[...]
    bh_c = jnp.dot(lower, gc, precision=HI, preferred_element_type=F32)
    bh_r = jnp.dot(gr, upper, precision=HI, preferred_element_type=F32)

    for h in range(ML_HEADS):
        sl = slice(h * ML_D, (h + 1) * ML_D)
        qt, k, vt = qt_ref[sl, :], k_ref[:, sl], vt_ref[sl, :]
        m_prev = ms_ref[h][0:1, 0:1]
        b_r = bh_r[ML_HEADS + h:ML_HEADS + h + 1, :]
        a_r = gr[h:h + 1, :] - b_r
        a_c = gc[:, h:h + 1] - bh_c[:, ML_HEADS + h:ML_HEADS + h + 1]
        dm = jnp.where(src_before_tgt, a_c + b_r, NEG_INF)
        g = b_r + m_prev
        m_row = jnp.maximum(g, jnp.max(dm, axis=0, keepdims=True))
        w_inter = jnp.exp(g - m_row)
        sc = _dot(k, qt) * jnp.exp(dm - m_row)
        cst, nst = cst_ref[h], nst_ref[h]
        num = _dot(vt, sc.astype(BF16)) + w_inter * _dot(cst.astype(BF16), qt)
        den = jnp.sum(sc, axis=0, keepdims=True) + w_inter * _dot(nst.astype(BF16), qt)[0:1]
        hout = (num / jnp.maximum(jnp.abs(den), jnp.exp(-m_row))).T
        h_ref[:, sl] = _gated_head_norm(hout, nw_ref[:, sl], g_ref[:, sl])

        b_last = b_r[:, L - 1:L]
        m_new = jnp.maximum(b_last + m_prev, jnp.max(b_last + a_r, axis=-1, keepdims=True))
        ws = jnp.exp(b_last + a_r - m_new)
        wc = jnp.exp(b_last + m_prev - m_new)
        cst_ref[h] = wc * cst + _dot((vt.astype(F32) * ws).astype(BF16), k)
        nst_ref[h] = wc * nst + _dot(jnp.broadcast_to(ws, (8, L)).astype(BF16), k)
        ms_ref[h] = jnp.broadcast_to(m_new, ms_ref.shape[1:])

    @pl.when(ci == pl.num_programs(1) - 1)
    def _():
        for h in range(ML_HEADS):
            c_ref[h] = cst_ref[h]
            n_ref[h:h + 1, :] = nst_ref[h][0:1]
            m_ref[:, h:h + 1] = ms_ref[h][0:1, 0:1]


def mlstm_prompt(qt, k, vt, gates, gates_t, gact, norm_w):
    b, t, _ = k.shape
    blk = pl.BlockSpec((None, ML_CHUNK, D_ML), lambda bi, c: (bi, c, 0))
    page = pl.BlockSpec((None, None, D_ML, ML_CHUNK), lambda bi, c: (bi, c, 0, 0))
    return pl.pallas_call(
        _mlstm_prompt_kernel,
        out_shape=(jax.ShapeDtypeStruct((b, t, D_ML), BF16),
                   jax.ShapeDtypeStruct((b, ML_HEADS, ML_D, ML_D), F32),
                   jax.ShapeDtypeStruct((b, ML_HEADS, ML_D), F32),
                   jax.ShapeDtypeStruct((b, 1, ML_HEADS), F32)),
        grid=(b, t // ML_CHUNK),
        in_specs=[page, blk, page,
                  pl.BlockSpec((None, ML_CHUNK, 2 * ML_HEADS), lambda bi, c: (bi, c, 0)),
                  pl.BlockSpec((None, 2 * ML_HEADS, ML_CHUNK), lambda bi, c: (bi, 0, c)),
                  pl.BlockSpec((None, ML_CHUNK, D_ML), lambda bi, c: (bi, c, 1)),
                  pl.BlockSpec((1, D_ML), lambda bi, c: (0, 0))],
        out_specs=(blk,
                   pl.BlockSpec((None, ML_HEADS, ML_D, ML_D), lambda bi, c: (bi, 0, 0, 0)),
                   pl.BlockSpec((None, ML_HEADS, ML_D), lambda bi, c: (bi, 0, 0)),
                   pl.BlockSpec((None, 1, ML_HEADS), lambda bi, c: (bi, 0, 0))),
        scratch_shapes=[pltpu.VMEM((ML_HEADS, ML_D, ML_D), F32), pltpu.VMEM((ML_HEADS, 8, ML_D), F32),
                        pltpu.VMEM((ML_HEADS, 8, 128), F32)],
        compiler_params=_params(("parallel", "arbitrary")),
        name="mlstm_prompt",
    )(qt, k, vt, gates, gates_t, gact, norm_w)


def _page_stream(pt_ref, hbm_refs, bufs, sems, *, layer, group, depth, reverse, consume, init):
    n_seq, n_pages = pt_ref.shape
    per_seq = n_pages // group
    total = n_seq * per_seq

    def copies(g, slot):
        seq, c = g // per_seq, g % per_seq
        out = []
        for r in range(group):
            walk = c * group + r
            page = pt_ref[seq, n_pages - 1 - walk if reverse else walk]
            for hbm, buf, sem in zip(hbm_refs, bufs, sems):
                out.append(pltpu.make_async_copy(hbm.at[layer, page], buf.at[slot, r], sem.at[slot]))
        return out

    for g in range(depth - 1):
        for cp in copies(g, g):
            cp.start()

    def body(g, carry):
        slot = lax.rem(g, depth)
        ahead = g + depth - 1

        @pl.when(ahead < total)
        def _():
            for cp in copies(ahead, lax.rem(ahead, depth)):
                cp.start()

        for cp in copies(g, slot):
            cp.wait()
        return consume(g // per_seq, g % per_seq, slot, carry)

    return lax.fori_loop(0, total, body, init)


def _sb_decode_kernel(pt_ref, q_ref, g_ref, k_hbm, v_hbm, o_ref, kbuf, vbuf, ksem, vsem, *, layer, group, depth):
    tiles = group // 2
    last_chunk = pt_ref.shape[1] // group - 1
    rowi = lax.broadcasted_iota(I32, (8, D_ATT), 0)
    own_head = _head_of_lane((8, D_ATT), 1) == rowi
    suffix = _suffix_matrix(KV_TILE)

    def consume(seq, c, slot, state):
        fresh = c == 0
        acc = jnp.where(fresh, 0.0, state[0])
        carry = jnp.where(fresh, 0.0, state[1])
        qm = jnp.where(own_head, jnp.broadcast_to(q_ref[seq].astype(F32), (8, D_ATT)), 0.0).astype(BF16)

        def tile(buf, t):
            lo, hi = buf[slot, 2 * t + 1].reshape(D_ATT, PAGE), buf[slot, 2 * t].reshape(D_ATT, PAGE)
            return jnp.concatenate([lo, hi], axis=1).astype(BF16)

        z = jnp.concatenate([_dot(qm, tile(kbuf, t)) for t in range(tiles)], axis=0)
        sp = _softplus(z)
        lf = -sp
        hi = lf.astype(BF16).astype(F32)
        both = _dot(jnp.concatenate([hi, lf - hi], axis=0).astype(BF16), suffix)
        later = both[:8 * tiles] + both[8 * tiles:]
        tot = jnp.sum(lf, axis=-1, keepdims=True)
        carries = [carry]
        for t in range(tiles):
            carries.append(carries[-1] + tot[8 * t:8 * (t + 1)])
        w = jnp.exp((z - sp) + later + jnp.concatenate(carries[:tiles], axis=0))
        for t in range(tiles):
            acc = acc + _dot_nt(w[8 * t:8 * (t + 1)].astype(BF16), tile(vbuf, t))

        @pl.when(c == last_chunk)
        def _():
            y = jnp.sum(jnp.where(own_head, acc, 0.0), axis=0, keepdims=True)
            o_ref[seq] = (y * g_ref[seq].astype(F32)).astype(BF16)

        return acc, carries[tiles]

    _page_stream(pt_ref, (k_hbm, v_hbm), (kbuf, vbuf), (ksem, vsem), layer=layer, group=group, depth=depth,
                 reverse=True, consume=consume, init=(jnp.zeros((8, D_ATT), F32), jnp.zeros((8, 1), F32)))


def _page_buffers(depth, group):
    return pltpu.VMEM((depth, group, N_HEADS, HEAD_DIM, PAGE), F32)


def sb_decode(page_table, q, gact, cache_k, cache_v, layer, *, group, depth):
    b = page_table.shape[0]
    rows = pl.BlockSpec((b, 1, D_ATT), lambda i, pt: (0, 0, 0))
    hbm = pl.BlockSpec(memory_space=pl.ANY)
    grid_spec = pltpu.PrefetchScalarGridSpec(
        num_scalar_prefetch=1,
        grid=(1,),
        in_specs=[rows, rows, hbm, hbm],
        out_specs=rows,
        scratch_shapes=[_page_buffers(depth, group), _page_buffers(depth, group),
                        pltpu.SemaphoreType.DMA((depth,)), pltpu.SemaphoreType.DMA((depth,))],
    )
    return pl.pallas_call(
        functools.partial(_sb_decode_kernel, layer=layer, group=group, depth=depth),
        out_shape=jax.ShapeDtypeStruct((b, 1, D_ATT), BF16),
        grid_spec=grid_spec,
        compiler_params=_params(("arbitrary",)),
        name="sb_decode",
    )(page_table, q, gact, cache_k, cache_v)


def _moba_gate_kernel(pt_ref, q_ref, k_hbm, idx_ref, kbuf, ksem, gs_ref, *, layer, group, depth):
    per_chunk = group // 2
    last_chunk = pt_ref.shape[1] // group - 1

    def consume(seq, c, slot, carry):
        qc = q_ref[seq]
        for r in range(per_chunk):
            pair = kbuf[slot, 2 * r].reshape(D_ATT, PAGE) + kbuf[slot, 2 * r + 1].reshape(D_ATT, PAGE)
            prod = pair * qc
            for h in range(N_HEADS):
                gs_ref[h, pl.ds(c * per_chunk + r, 1), :] = jnp.sum(prod[h * HEAD_DIM:(h + 1) * HEAD_DIM], axis=0,
                                                                    keepdims=True)

        @pl.when(c == last_chunk)
        def _():
            nb = gs_ref.shape[1]
            lane = lax.broadcasted_iota(I32, (nb, 128), 1)
            gate = jnp.zeros((nb, 128), F32)
            for h in range(N_HEADS):
                gate = gate + jnp.where(lane == h, jnp.sum(gs_ref[h], axis=-1, keepdims=True) * (1.0 / KV_TILE), 0.0)
            blk = lax.broadcasted_iota(I32, (nb, 128), 0)
            rank = jnp.zeros((nb, 128), I32)
            for m in range(nb):
                gm = gate[m:m + 1, :]
                rank = rank + ((gm > gate) | ((gm == gate) & (blk > m))).astype(I32)
            blk_f = blk.astype(F32)
            rows = [jnp.sum(jnp.where(rank == t, blk_f, 0.0), axis=0, keepdims=True) for t in range(MOBA_TOPK)]
            idx_ref[seq] = jnp.concatenate(rows + [jnp.zeros((8 - MOBA_TOPK, 128), F32)], axis=0).astype(I32)

        return carry

    _page_stream(pt_ref, (k_hbm,), (kbuf,), (ksem,), layer=layer, group=group, depth=depth, reverse=False,
                 consume=consume, init=0)


def moba_decode_select(page_table, q_col, cache_k, layer, *, group, depth):
    b, n_pages = page_table.shape
    nb = n_pages * PAGE // KV_TILE
    grid_spec = pltpu.PrefetchScalarGridSpec(
        num_scalar_prefetch=1,
        grid=(1,),
        in_specs=[pl.BlockSpec((b, D_ATT, 1), lambda i, pt: (0, 0, 0)), pl.BlockSpec(memory_space=pl.ANY)],
        out_specs=pl.BlockSpec((b, 8, 128), lambda i, pt: (0, 0, 0)),
        scratch_shapes=[_page_buffers(depth, group), pltpu.SemaphoreType.DMA((depth,)),
                        pltpu.VMEM((N_HEADS, nb, PAGE), F32)],
    )
    return pl.pallas_call(
        functools.partial(_moba_gate_kernel, layer=layer, group=group, depth=depth),
        out_shape=jax.ShapeDtypeStruct((b, 8, 128), I32),
        grid_spec=grid_spec,
        compiler_params=_params(("arbitrary",)),
        name="moba_decode_select",
    )(page_table, q_col, cache_k)


def _moba_decode_kernel(pt_ref, sel_ref, q_ref, kn_ref, vn_ref, bias_ref, g_ref, k_hbm, v_hbm, o_ref,
                        kbuf, vbuf, ksem, vsem, *, layer, last_block):
    n_seq = pt_ref.shape[0]

    def piece(t, h, half):
        return (t * N_HEADS + h) * 2 + half

    def copies(seq, slot):
        out = []
        for t in range(MOBA_TOPK):
            for h in range(N_HEADS):
                blk = sel_ref[seq, t * N_HEADS + h]
                for half in range(2):
                    page = pt_ref[seq, 2 * blk + half]
                    i = piece(t, h, half)
                    out.append(pltpu.make_async_copy(k_hbm.at[layer, page, h], kbuf.at[slot, i], ksem.at[slot]))
                    out.append(pltpu.make_async_copy(v_hbm.at[layer, page, h], vbuf.at[slot, i], vsem.at[slot]))
        return out

    for cp in copies(0, 0):
        cp.start()

    def body(seq, carry):
        slot = lax.rem(seq, 2)

        @pl.when(seq + 1 < n_seq)
        def _():
            for cp in copies(seq + 1, 1 - slot):
                cp.start()

        for cp in copies(seq, slot):
            cp.wait()
        for h in range(N_HEADS):
            qf = jnp.broadcast_to(q_ref[seq, h].astype(F32), (8, HEAD_DIM))
            q = qf.astype(BF16)
            s_self = (jnp.sum(qf * kn_ref[seq, h].astype(BF16).astype(F32), axis=-1, keepdims=True)
                      + bias_ref[h, 2][:, 0:1])
            scores = []
            for t in range(MOBA_TOPK):
                kt = jnp.concatenate([kbuf[slot, piece(t, h, 0)], kbuf[slot, piece(t, h, 1)]], axis=1).astype(BF16)
                near = sel_ref[seq, t * N_HEADS + h] == last_block
                scores.append(_dot(q, kt) + bias_ref[h, jnp.where(near, 1, 0)])
            m = s_self
            for s in scores:
                m = jnp.maximum(m, jnp.max(s, axis=-1, keepdims=True))
            p_self = jnp.exp(s_self - m)
            l = p_self
            acc = p_self * vn_ref[seq, h].astype(BF16).astype(F32)
            for t, s in enumerate(scores):
                p = jnp.exp(s - m)
                l = l + jnp.sum(p, axis=-1, keepdims=True)
                vt = jnp.concatenate([vbuf[slot, piece(t, h, 0)], vbuf[slot, piece(t, h, 1)]], axis=1).astype(BF16)
                acc = acc + _dot_nt(p.astype(BF16), vt)
            o_ref[seq, h] = ((acc / l)[0:1] * g_ref[seq, h].astype(F32)).astype(BF16)
        return carry

    lax.fori_loop(0, n_seq, body, 0)


def moba_decode(page_table, sel, qh, k_new, v_new, dec_bias, gact_mb, cache_k, cache_v, layer):
    b, n_pages = page_table.shape
    last_block = n_pages * PAGE // KV_TILE - 1

    pieces = 2 * MOBA_TOPK * N_HEADS
    vec = pl.BlockSpec((b, N_HEADS, 1, HEAD_DIM), lambda i, pt, sl: (0, 0, 0, 0))
    hbm = pl.BlockSpec(memory_space=pl.ANY)
    grid_spec = pltpu.PrefetchScalarGridSpec(
        num_scalar_prefetch=2,
        grid=(1,),
        in_specs=[vec, vec, vec, pl.BlockSpec(dec_bias.shape, lambda i, pt, sl: (0, 0, 0, 0)), vec, hbm, hbm],
        out_specs=vec,
        scratch_shapes=[pltpu.VMEM((2, pieces, HEAD_DIM, PAGE), F32), pltpu.VMEM((2, pieces, HEAD_DIM, PAGE), F32),
                        pltpu.SemaphoreType.DMA((2,)), pltpu.SemaphoreType.DMA((2,))],
    )
    return pl.pallas_call(
        functools.partial(_moba_decode_kernel, layer=layer, last_block=last_block),
        out_shape=jax.ShapeDtypeStruct((b, N_HEADS, 1, HEAD_DIM), BF16),
        grid_spec=grid_spec,
        compiler_params=_params(("arbitrary",)),
        name="moba_decode",
    )(page_table, sel, qh, k_new, v_new, dec_bias, gact_mb, cache_k, cache_v)


def _mlstm_decode_kernel(q_ref, k_ref, v_ref, gt_ref, g_ref, nw_ref, c_ref, n_ref, m_ref,
                         h_ref, co_ref, no_ref, mo_ref):
    r = lax.broadcasted_iota(I32, (ML_D, ML_D), 0)
    c = lax.broadcasted_iota(I32, (ML_D, ML_D), 1)
    eye = r == c
    for h in range(ML_HEADS):
        sl = slice(h * ML_D, (h + 1) * ML_D)
        q = q_ref[:, sl].astype(F32)
        k = k_ref[:, sl].astype(F32)
        v = v_ref[:, sl].astype(F32)
        i_pre = gt_ref[:, h:h + 1]
        logf = gt_ref[:, ML_HEADS + h:ML_HEADS + h + 1]
        m_prev = m_ref[:, h:h + 1]
        cm = c_ref[h]
        nv = n_ref[h:h + 1, :]
        g = logf + m_prev
        m_new = jnp.maximum(g, i_pre)
        w_in = jnp.exp(i_pre - m_new)
        w_st = jnp.exp(g - m_new)
        qk = jnp.sum(q * k, axis=-1, keepdims=True)
        cq = _dot_nt(jnp.broadcast_to(q, (8, ML_D)), cm, precision=HI)[0:1]
        nq = jnp.sum(nv * q, axis=-1, keepdims=True)
        num = (qk * w_in) * v + w_st * cq
        den = qk * w_in + w_st * nq
        hout = num / jnp.maximum(jnp.abs(den), jnp.exp(-m_new))
        h_ref[:, sl] = _gated_head_norm(hout, nw_ref[:, sl], g_ref[:, sl])
        v_diag = jnp.where(eye, jnp.broadcast_to(v, (ML_D, ML_D)), 0.0)
        outer = jnp.dot(v_diag, jnp.broadcast_to(k, (ML_D, ML_D)), precision=HI, preferred_element_type=F32)
        co_ref[h] = w_st * cm + w_in * outer
        no_ref[h:h + 1, :] = w_st * nv + w_in * k
        mo_ref[:, h:h + 1] = m_new


def mlstm_decode(q, k, v, gates, gact, norm_w, c0, n0, m0, layer):
    b = q.shape[0]
    vec = pl.BlockSpec((None, 1, D_ML), lambda bi: (bi, 0, 0))
    cs = pl.BlockSpec((None, ML_HEADS, ML_D, ML_D), lambda bi: (bi, 0, 0, 0))
    ns = pl.BlockSpec((None, ML_HEADS, ML_D), lambda bi: (bi, 0, 0))
    msp = pl.BlockSpec((None, 1, ML_HEADS), lambda bi: (bi, 0, 0))
    return pl.pallas_call(
        _mlstm_decode_kernel,
        out_shape=(jax.ShapeDtypeStruct((b, 1, D_ML), BF16),
                   jax.ShapeDtypeStruct(c0.shape[1:], F32),
                   jax.ShapeDtypeStruct(n0.shape[1:], F32),
                   jax.ShapeDtypeStruct(m0.shape, F32)),
        grid=(b,),
        in_specs=[vec, vec, vec,
                  pl.BlockSpec((None, 1, 2 * ML_HEADS), lambda bi: (bi, 0, 0)),
                  pl.BlockSpec((None, 1, D_ML), lambda bi: (bi, 0, 1)),
                  pl.BlockSpec((1, D_ML), lambda bi: (0, 0)),
                  pl.BlockSpec((None, None, ML_HEADS, ML_D, ML_D), lambda bi: (layer, bi, 0, 0, 0)),
                  pl.BlockSpec((None, None, ML_HEADS, ML_D), lambda bi: (layer, bi, 0, 0)),
                  msp],
        out_specs=(vec, cs, ns, msp),
        compiler_params=_params(("parallel",)),
        name="mlstm_decode",
    )(q, k, v, gates, gact, norm_w, c0, n0, m0)


SB_DECODE_GROUP = 16
MOBA_SELECT_GROUP = 32
STREAM_DEPTH = 4
PROMPT_TM = 512


def _bf16_weights(w_in, w_out):
    w_t = jnp.swapaxes(w_in, 1, 2)
    w_bf = w_t.astype(BF16)
    w_lo = (w_t[:, O_QMB:O_VMB] - w_bf[:, O_QMB:O_VMB].astype(F32)).astype(BF16)
    return w_bf, w_lo, w_out.astype(BF16)


def kernel(x_prompt, x_sample, cache_sb_k, cache_sb_v, cache_moba_k, cache_moba_v, state_mlstm_C, state_mlstm_n,
           state_mlstm_m, page_table, c_prompt, c_sample, norm_w, w_ada, b_ada, w_in, b_gates, q_norm_w, k_norm_w,
           rel_bias, ml_norm_w, w_out):
    bp, tp, d = x_prompt.shape
    bs = x_sample.shape[0]
    depth = w_in.shape[0]
    mp = bp * tp
    n_pages = page_table.shape[1]

    mod = ada_modulation(jnp.concatenate([c_prompt, c_sample], axis=0), w_ada, b_ada)
    bias_p, bias_d = bias_tiles(rel_bias)
    lanes_last = lambda a: jnp.transpose(a, (0, 1, 3, 4, 2))
    sbk, sbv, mbk, mbv = (lanes_last(a) for a in (cache_sb_k, cache_sb_v, cache_moba_k, cache_moba_v))

    xp = x_prompt.reshape(mp, d)
    xs = x_sample.reshape(bs, d)
    p_out = [[] for _ in range(3)]
    s_out = [[] for _ in range(7)]
    kv_pages = ()
    w_bf, w_lo, wo_bf = _bf16_weights(w_in, w_out)
    for l in range(depth):
        nw = norm_w[l].reshape(1, d)
        bg = b_gates[l].reshape(1, 2 * ML_HEADS)
        qnw = jnp.tile(q_norm_w[l], N_HEADS).reshape(1, D_ATT)
        knw = jnp.tile(k_norm_w[l], N_HEADS).reshape(1, D_ATT)
        mlw = ml_norm_w[l].reshape(1, D_ML)
        shift, scale, gate = jnp.split(mod[l], 3, axis=-1)

        pm = lambda a: a[:bp].reshape(bp, 1, d)
        (qsb_t, ksb_t, ksb, vsb_t, vsb_th, qmb_t, qmb_th, kmb_t, kmb, vmb_t, vmb_th,
         qml, kml, vml, gact, gates, kmean) = \
            input_projection(xp, nw, pm(scale), pm(shift), w_bf, w_lo, bg, qnw, knw,
                             tm=PROMPT_TM, rows_per_mod=tp // PROMPT_TM, prompt=True,
                             layer=l, depth=depth, kv_carried=kv_pages)
        kv_pages = (ksb_t, vsb_t, kmb_t, vmb_t)
        seq = lambda a: a.reshape(bp, tp, a.shape[-1])
        pages = lambda a: a.reshape(bp, tp // PAGE, D_ATT, PAGE)
        gact3 = seq(gact)
        ysb = sb_prompt(pages(qsb_t), seq(ksb), pages(vsb_th), gact3)
        ymb = moba_prompt(pages(qmb_t), pages(qmb_th), seq(kmb), pages(vmb_th),
                          kmean.reshape(bp, tp // KV_TILE, D_ATT), bias_p, gact3)
        gates3 = seq(gates)
        ml_pages = lambda a: a.reshape(bp, tp // ML_CHUNK, D_ML, ML_CHUNK)
        yml, c_p, n_p, m_p = mlstm_prompt(ml_pages(qml), seq(kml), ml_pages(vml), gates3, gates3.swapaxes(1, 2),
                                          gact3, mlw)
        xp = output_projection(ysb.reshape(mp, D_ATT), ymb.reshape(mp, D_ATT), yml.reshape(mp, D_ML), xp, pm(gate),
                               wo_bf, tm=PROMPT_TM, rows_per_mod=tp // PROMPT_TM, layer=l)
        for lst, a in zip(p_out, (c_p, n_p, m_p.reshape(bp, ML_HEADS))):
            lst.append(a)

        sm = lambda a: a[bp:].reshape(1, bs, d)
        (qsb, ksb, vsb, qmb, qmbh, kmb, vmb, qml, kml, vml, gact, gates) = \
            input_projection(xs, nw, sm(scale), sm(shift), w_bf, w_lo, bg, qnw, knw,
                             tm=bs, rows_per_mod=1, prompt=False, layer=l)
        tok = lambda a: a.reshape(bs, 1, a.shape[-1])
        heads = lambda a: a.reshape(bs, N_HEADS, 1, HEAD_DIM)
        gact3 = tok(gact)
        ysb = sb_decode(page_table, tok(qsb), gact3[:, :, :D_ATT], sbk, sbv, l, group=min(SB_DECODE_GROUP, n_pages),
                        depth=STREAM_DEPTH)
        sel = moba_decode_select(page_table, qmb.reshape(bs, D_ATT, 1), mbk, l,
                                 group=min(MOBA_SELECT_GROUP, n_pages), depth=STREAM_DEPTH)
        sel = sel[:, :MOBA_TOPK, :N_HEADS].reshape(bs, MOBA_TOPK * N_HEADS)
        ymb = moba_decode(page_table, sel, heads(qmbh), heads(kmb), heads(vmb), bias_d,
                          heads(gact[:, D_ATT:2 * D_ATT]), mbk, mbv, l)
        yml, c_s, n_s, m_s = mlstm_decode(tok(qml), tok(kml), tok(vml), tok(gates), gact3, mlw,
                                          state_mlstm_C, state_mlstm_n, state_mlstm_m[l].reshape(bs, 1, ML_HEADS), l)
        xs = output_projection(ysb.reshape(bs, D_ATT), ymb.reshape(bs, D_ATT), yml.reshape(bs, D_ML), xs, sm(gate),
                               wo_bf, tm=bs, rows_per_mod=1, layer=l)
        new = lambda a: a.reshape(bs, 1, N_HEADS, HEAD_DIM)
        for lst, a in zip(s_out, (new(ksb), new(vsb), new(kmb), new(vmb), c_s, n_s, m_s.reshape(bs, ML_HEADS))):
            lst.append(a)

    paged = lambda a: jnp.transpose(a.reshape(depth, bp, tp // PAGE, N_HEADS, HEAD_DIM, PAGE), (0, 1, 2, 5, 3, 4))
    P = [paged(a) for a in kv_pages] + [jnp.stack(a) for a in p_out]
    S = [jnp.stack(a) for a in s_out]
    return (xp.reshape(bp, tp, d), xs.reshape(bs, 1, d),
            P[0], P[1], P[2], P[3], P[4], P[5], P[6], S[0], S[1], S[2], S[3], S[4], S[5], S[6])
```

```python
import functools
import math

import numpy as np
import jax
import jax.numpy as jnp
from jax import lax
from jax.experimental import pallas as pl
from jax.experimental.pallas import tpu as pltpu

F32 = jnp.float32
BF16 = jnp.bfloat16
I32 = jnp.int32
HI = lax.Precision.HIGHEST

EPS = 1e-6
HEAD_DIM = 64
N_HEADS = 4
D_ATT = N_HEADS * HEAD_DIM
ML_HEADS = 4
ML_D = 128
D_ML = ML_HEADS * ML_D
Q_BLOCK = 128
KV_TILE = 256
MOBA_TOPK = 3
N_BUCKETS = 32
MAX_DISTANCE = 128
ML_CHUNK = 128
PAGE = 128
HEAD_SHIFT = 6
NEG_INF = float("-inf")

VMEM_LIMIT = 56 * 1024 * 1024

_SIZES = [D_ATT] * 8 + [D_ML] * 5 + [ML_HEADS] * 2
_OFF = np.concatenate([[0], np.cumsum(_SIZES)]).tolist()
(O_QSB, O_KSB, O_VSB, O_GSB, O_QMB, O_KMB, O_VMB, O_GMB,
 O_QML, O_KML, O_VML, O_OML, O_GML, O_IML, O_FML, O_END) = _OFF


def _params(sem, vmem=VMEM_LIMIT):
    return pltpu.CompilerParams(dimension_semantics=sem, vmem_limit_bytes=vmem)


def _dot(a, b):
    return jnp.dot(a, b, preferred_element_type=F32)


def _dot_nt(a, b, precision=None):
    return lax.dot_general(a, b, (((1,), (1,)), ((), ())), precision=precision,
                           preferred_element_type=F32)


SOFTPLUS_CLAMP = 64.0


def _softplus(z):
    return jnp.maximum(jnp.log(1.0 + jnp.exp(jnp.minimum(z, SOFTPLUS_CLAMP))), z)


def _head_of_lane(shape, axis):
    return lax.shift_right_logical(lax.broadcasted_iota(I32, shape, axis), HEAD_SHIFT)


def _ada_kernel(c_ref, w_ref, b_ref, o_ref):
    c = c_ref[...]
    a = c * jax.nn.sigmoid(c)
    o_ref[...] = jnp.dot(a, w_ref[...], precision=HI, preferred_element_type=F32) + b_ref[...]


def ada_modulation(c_all, w_ada, b_ada):
    depth, d, d3 = w_ada.shape
    rows = c_all.shape[0]
    nt = d3 // d
    return pl.pallas_call(
        _ada_kernel,
        out_shape=jax.ShapeDtypeStruct((depth, rows, d3), F32),
        grid=(depth, nt),
        in_specs=[pl.BlockSpec((rows, d), lambda l, j: (0, 0)),
                  pl.BlockSpec((None, d, d), lambda l, j: (l, 0, j)),
                  pl.BlockSpec((None, 1, d), lambda l, j: (l, 0, j))],
        out_specs=pl.BlockSpec((None, rows, d), lambda l, j: (l, 0, j)),
        compiler_params=_params(("parallel", "parallel")),
        name="ada_modulation",
    )(c_all, w_ada, b_ada.reshape(depth, 1, d3))


def _bucket_thresholds():
    d = np.arange(0, 4 * MAX_DISTANCE, dtype=np.int32)
    max_exact = N_BUCKETS // 2
    df = np.maximum(d, 1).astype(np.float32)
    large = max_exact + (np.log(df / np.float32(max_exact)) / np.float32(math.log(MAX_DISTANCE / max_exact))
                         * np.float32(N_BUCKETS - max_exact)).astype(np.int32)
    large = np.minimum(large, N_BUCKETS - 1)
    bucket = np.where(d < max_exact, d, large)
    thr = []
    for k in range(1, N_BUCKETS):
        idx = np.nonzero(bucket >= k)[0]
        thr.append(int(idx[0]))
    assert all(np.all((bucket >= k) == (d >= t)) for k, t in zip(range(1, N_BUCKETS), thr))
    return thr


_BUCKET_THR = _bucket_thresholds()

_PATTERN_OFFSETS = (0, Q_BLOCK, KV_TILE)
P_FAR = len(_PATTERN_OFFSETS)


def _bias_from_dist(dist, rb_ref, h):
    b = jnp.full(dist.shape, rb_ref[0, h], F32)
    for k, t in zip(range(1, N_BUCKETS), _BUCKET_THR):
        b = jnp.where(dist >= t, rb_ref[k, h], b)
    return jnp.where(dist >= 0, b, NEG_INF)


def _bias_kernel(rb_ref, tile_ref, dec_ref):
    j = lax.broadcasted_iota(I32, (KV_TILE, Q_BLOCK), 0)
    i = lax.broadcasted_iota(I32, (KV_TILE, Q_BLOCK), 1)
    for p, off in enumerate(_PATTERN_OFFSETS):
        for h in range(N_HEADS):
            tile_ref[p, :, h * Q_BLOCK:(h + 1) * Q_BLOCK] = _bias_from_dist(off + i - j, rb_ref, h)
    for h in range(N_HEADS):
        tile_ref[P_FAR, :, h * Q_BLOCK:(h + 1) * Q_BLOCK] = jnp.full((KV_TILE, Q_BLOCK), rb_ref[N_BUCKETS - 1, h], F32)
    jr = lax.broadcasted_iota(I32, (8, KV_TILE), 1)
    for h in range(N_HEADS):
        dec_ref[h, 0] = jnp.full((8, KV_TILE), rb_ref[N_BUCKETS - 1, h], F32)
        dec_ref[h, 1] = _bias_from_dist(KV_TILE - jr, rb_ref, h)
        dec_ref[h, 2] = jnp.full((8, KV_TILE), rb_ref[0, h], F32)


def bias_tiles(rel_bias):
    return pl.pallas_call(
        _bias_kernel,
        out_shape=(jax.ShapeDtypeStruct((P_FAR + 1, KV_TILE, N_HEADS * Q_BLOCK), F32),
                   jax.ShapeDtypeStruct((N_HEADS, 3, 8, KV_TILE), F32)),
        in_specs=[pl.BlockSpec(memory_space=pltpu.SMEM)],
        name="bias_tiles",
    )(rel_bias)


def _inproj_kernel(x_ref, nw_ref, sc_ref, sh_ref, w_ref, wlo_ref, bg_ref, qnw_ref, knw_ref, *outs, prompt, layer,
                   n_carried):
    outs = list(outs[n_carried:])

    def paged(p):
        return [p[pg * PAGE:(pg + 1) * PAGE].T for pg in range(p.shape[0] // PAGE)]

    def emit(p, *, rows_bf16=False, rows_f32=False, pages_f32=False, pages_bf16=False):
        pages = paged(p) if (pages_f32 or pages_bf16) else None
        for want, dtype, is_pages in ((pages_f32, F32, True), (rows_f32, F32, False),
                                      (pages_bf16, BF16, True), (rows_bf16, BF16, False)):
            if not want:
                continue
            ref = outs.pop(0)
            if is_pages and len(ref.shape) == 4:
                for slab in range(ref.shape[0]):
                    for pg, t in enumerate(pages):
                        ref[slab, pg] = t.astype(dtype) if slab == layer else jnp.zeros_like(t, dtype)
            elif is_pages:
                for pg, t in enumerate(pages):
                    ref[pg] = t.astype(dtype)
            else:
                ref[...] = p.astype(dtype)

    x = x_ref[...]
    ms = jnp.mean(x * x, axis=-1, keepdims=True)
    h = (x * lax.rsqrt(ms + EPS) * nw_ref[...]) * (1.0 + sc_ref[...]) + sh_ref[...]
    hb = h.astype(BF16)
    hlo = (h - hb.astype(F32)).astype(BF16)

    def proj(a, b):
        return _dot_nt(hb, w_ref[a:b, :])

    def proj3(a, b, la, lb):
        wh = w_ref[a:b, :]
        return _dot_nt(hb, wh) + (_dot_nt(hlo, wh) + _dot_nt(hb, wlo_ref[la:lb, :]))

    def silu(g):
        return g * jax.nn.sigmoid(g)

    r = lax.broadcasted_iota(I32, (D_ATT, D_ATT), 0)
    c = lax.broadcasted_iota(I32, (D_ATT, D_ATT), 1)
    same_head = lax.shift_right_logical(r, HEAD_SHIFT) == lax.shift_right_logical(c, HEAD_SHIFT)
    head_mean = jnp.where(same_head, 1.0 / HEAD_DIM, 0.0).astype(BF16)

    def head_norm(p, w):
        sq = p * p
        hi = sq.astype(BF16)
        lo = (sq - hi.astype(F32)).astype(BF16)
        ms = _dot(hi, head_mean) + _dot(lo, head_mean)
        return p * lax.rsqrt(ms + EPS) * w

    score_scale = HEAD_DIM ** -0.5
    emit(proj(O_QSB, O_KSB) * score_scale, pages_bf16=prompt, rows_bf16=not prompt)
    emit(proj(O_KSB, O_VSB), pages_f32=prompt, rows_bf16=prompt, rows_f32=not prompt)
    emit(proj(O_VSB, O_GSB), pages_f32=prompt, pages_bf16=prompt, rows_f32=not prompt)
    g_sb = silu(proj(O_GSB, O_QMB))
    qn = head_norm(proj3(O_QMB, O_KMB, 0, D_ATT), qnw_ref[...])
    if prompt:
        qt = paged(qn)
        ref_f, ref_h = outs.pop(0), outs.pop(0)
        for pg, t in enumerate(qt):
            ref_f[pg] = t
            ref_h[pg] = (t * score_scale).astype(BF16)
    else:
        emit(qn, rows_f32=True)
        emit(qn * score_scale, rows_bf16=True)
    kn = head_norm(proj3(O_KMB, O_VMB, D_ATT, 2 * D_ATT), knw_ref[...])
    emit(kn, pages_f32=prompt, rows_bf16=prompt, rows_f32=not prompt)
    emit(proj(O_VMB, O_GMB), pages_f32=prompt, pages_bf16=prompt, rows_f32=not prompt)
    g_mb = silu(proj(O_GMB, O_QML))
    emit(proj(O_QML, O_KML), pages_bf16=prompt, rows_bf16=not prompt)
    emit(proj(O_KML, O_VML) * (ML_D ** -0.5), rows_bf16=True)
    emit(proj(O_VML, O_OML), pages_bf16=prompt, rows_bf16=not prompt)
    o = proj(O_OML, O_GML)
    g_ml = jax.nn.sigmoid(o) * silu(proj(O_GML, O_IML))
    gact_ref = outs.pop(0)
    gact_ref[:, 0:D_ATT] = g_sb.astype(BF16)
    gact_ref[:, D_ATT:2 * D_ATT] = g_mb.astype(BF16)
    gact_ref[:, 2 * D_ATT:] = g_ml.astype(BF16)
    pre = proj(O_IML, O_END) + bg_ref[...]
    is_f = lax.broadcasted_iota(I32, pre.shape, 1) >= ML_HEADS
    logf = jnp.minimum(pre, 0.0) - jnp.log1p(jnp.exp(-jnp.abs(pre)))
    outs.pop(0)[...] = jnp.where(is_f, logf, pre)
    if prompt:
        kmean_ref = outs.pop(0)
        for i in range(kn.shape[0] // KV_TILE):
            kmean_ref[i] = jnp.mean(kn[i * KV_TILE:(i + 1) * KV_TILE], axis=0, keepdims=True)
    assert not outs


def input_projection(x, norm_w, scale, shift, w_bf, w_lo, b_gates, qnw, knw, *, tm, rows_per_mod, prompt,
                     layer=0, depth=1, kv_carried=()):
    m, d = x.shape
    n_tiles = m // tm
    rmod = scale.shape[1]
    row = lambda i: (i, 0)
    const = lambda i: (0, 0)
    mod = lambda i: (i // rows_per_mod, 0, 0)

    def rows(width, dtype):
        return jax.ShapeDtypeStruct((m, width), dtype), pl.BlockSpec((tm, width), row)

    def pages(dtype, width=D_ATT):
        return (jax.ShapeDtypeStruct((m // PAGE, width, PAGE), dtype),
                pl.BlockSpec((tm // PAGE, width, PAGE), lambda i: (i, 0, 0)))

    def kv_pages():
        shape = jax.ShapeDtypeStruct((depth, m // PAGE, D_ATT, PAGE), F32)
        if kv_carried:
            return shape, pl.BlockSpec((None, tm // PAGE, D_ATT, PAGE), lambda i: (layer, i, 0, 0))
        return shape, pl.BlockSpec((depth, tm // PAGE, D_ATT, PAGE), lambda i: (0, i, 0, 0))

    kv_out_index = (1, 3, 7, 9)
    if prompt:
        nb = tm // KV_TILE
        outs = [pages(BF16), kv_pages(), rows(D_ATT, BF16), kv_pages(), pages(BF16),
                pages(F32), pages(BF16), kv_pages(), rows(D_ATT, BF16), kv_pages(), pages(BF16)]
        tail = [(jax.ShapeDtypeStruct((n_tiles, nb, 1, D_ATT), F32),
                 pl.BlockSpec((None, nb, 1, D_ATT), lambda i: (i, 0, 0, 0)))]
    else:
        outs = [rows(D_ATT, BF16), rows(D_ATT, F32), rows(D_ATT, F32),
                rows(D_ATT, F32), rows(D_ATT, BF16), rows(D_ATT, F32), rows(D_ATT, F32)]
        tail = []
    ml_qv = pages(BF16, D_ML) if prompt else rows(D_ML, BF16)
    outs += [ml_qv, rows(D_ML, BF16), ml_qv,
             rows(2 * D_ATT + D_ML, BF16), rows(2 * ML_HEADS, F32)] + tail
    n_in = 9
    return pl.pallas_call(
        functools.partial(_inproj_kernel, prompt=prompt, layer=layer, n_carried=len(kv_carried)),
        out_shape=tuple(o[0] for o in outs),
        input_output_aliases={n_in + j: kv_out_index[j] for j in range(len(kv_carried))},
        grid=(n_tiles,),
        in_specs=[pl.BlockSpec((tm, d), row),
                  pl.BlockSpec((1, d), const),
                  pl.BlockSpec((None, rmod, d), mod),
                  pl.BlockSpec((None, rmod, d), mod),
                  pl.BlockSpec((None,) + w_bf.shape[1:], lambda i: (layer, 0, 0)),
                  pl.BlockSpec((None,) + w_lo.shape[1:], lambda i: (layer, 0, 0)),
                  pl.BlockSpec((1, 2 * ML_HEADS), const),
                  pl.BlockSpec((1, D_ATT), const),
                  pl.BlockSpec((1, D_ATT), const)] + [pl.BlockSpec(memory_space=pl.ANY)] * len(kv_carried),
        out_specs=tuple(o[1] for o in outs),
        compiler_params=_params(("parallel",)),
        name="input_projection",
    )(x, norm_w, scale, shift, w_bf, w_lo, b_gates, qnw, knw, *kv_carried)


def _outproj_kernel(ysb_ref, ymb_ref, yml_ref, x_ref, gate_ref, w_ref, o_ref):
    y = (_dot(ysb_ref[...], w_ref[0:D_ATT, :]) + _dot(ymb_ref[...], w_ref[D_ATT:2 * D_ATT, :])
         + _dot(yml_ref[...], w_ref[2 * D_ATT:, :]))
    o_ref[...] = x_ref[...] + gate_ref[...] * y


def output_projection(ysb, ymb, yml, x, gate, w_bf, *, tm, rows_per_mod, layer):
    m, d = x.shape
    rmod = gate.shape[1]
    row = lambda i: (i, 0)
    return pl.pallas_call(
        _outproj_kernel,
        out_shape=jax.ShapeDtypeStruct((m, d), F32),
        grid=(m // tm,),
        in_specs=[pl.BlockSpec((tm, D_ATT), row), pl.BlockSpec((tm, D_ATT), row), pl.BlockSpec((tm, D_ML), row),
                  pl.BlockSpec((tm, d), row),
                  pl.BlockSpec((None, rmod, d), lambda i: (i // rows_per_mod, 0, 0)),
                  pl.BlockSpec((None,) + w_bf.shape[1:], lambda i: (layer, 0, 0))],
        out_specs=pl.BlockSpec((tm, d), row),
        compiler_params=_params(("parallel",)),
        name="output_projection",
    )(ysb, ymb, yml, x, gate, w_bf)


def _suffix_matrix(n, transposed=False):
    r = lax.broadcasted_iota(I32, (n, n), 0)
    c = lax.broadcasted_iota(I32, (n, n), 1)
    return jnp.where((c > r) if transposed else (r > c), 1.0, 0.0).astype(BF16)


def _heads_on_lanes(qt):
    head = _head_of_lane((D_ATT, 1), 0)
    return jnp.concatenate([jnp.where(head == h, qt, jnp.zeros_like(qt)) for h in range(N_HEADS)], axis=1)


def _heads_from_lanes(acc_t, q):
    head = _head_of_lane((D_ATT, 1), 0)
    out = jnp.zeros((D_ATT, q), F32)
    for h in range(N_HEADS):
        out = out + jnp.where(head == h, acc_t[:, h * q:(h + 1) * q], 0.0)
    return out.T


def _kv_tile(k_ref, vt_ref, n):
    start = pl.multiple_of(n * KV_TILE, KV_TILE)
    vt = jnp.concatenate([vt_ref[2 * n], vt_ref[2 * n + 1]], axis=1)
    return k_ref[pl.ds(start, KV_TILE), :], vt


def _sb_tile_t(q4t, k, vt, upper, carry, mask):
    z = _dot(k, q4t)
    sp = _softplus(z)
    if mask is not None:
        sp_sum = jnp.where(mask, sp, 0.0)
    else:
        sp_sum = sp
    later = _dot(upper, sp_sum.astype(BF16))
    w = jnp.exp(((z - sp) - later) - carry)
    if mask is not None:
        w = jnp.where(mask, w, 0.0)
    return _dot(vt, w.astype(BF16)), carry + jnp.sum(sp_sum, axis=0, keepdims=True)


def _sb_pair_t(q4t, tile_a, tile_b, upper, carry, mask_a):
    (ka, vta), (kb, vtb) = tile_a, tile_b
    za, zb = _dot(ka, q4t), _dot(kb, q4t)
    spa, spb = _softplus(za), _softplus(zb)
    sum_a = spa if mask_a is None else jnp.where(mask_a, spa, 0.0)
    la, lb = _dot(upper, sum_a.astype(BF16)), _dot(upper, spb.astype(BF16))
    carry_b = carry + jnp.sum(sum_a, axis=0, keepdims=True)
    wa = jnp.exp(((za - spa) - la) - carry)
    if mask_a is not None:
        wa = jnp.where(mask_a, wa, 0.0)
    wb = jnp.exp(((zb - spb) - lb) - carry_b)
    pv = _dot(vta, wa.astype(BF16)) + _dot(vtb, wb.astype(BF16))
    return pv, carry_b + jnp.sum(spb, axis=0, keepdims=True)


def _sb_prompt_kernel(qt_ref, k_ref, vt_ref, g_ref, o_ref, acc_ref, carry_ref):
    i = pl.program_id(1)
    cols = N_HEADS * Q_BLOCK
    q4t = _heads_on_lanes(qt_ref[...])
    upper = _suffix_matrix(KV_TILE, transposed=True)
    last = (i * Q_BLOCK) // KV_TILE
    kpos = last * KV_TILE + lax.broadcasted_iota(I32, (KV_TILE, cols), 0)
    qpos = i * Q_BLOCK + (lax.broadcasted_iota(I32, (KV_TILE, cols), 1) & (Q_BLOCK - 1))
    mask = kpos < qpos
    zero = jnp.zeros((1, cols), F32)

    @pl.when(last % 2 == 0)
    def _():
        k, vt = _kv_tile(k_ref, vt_ref, last)
        acc_ref[...], carry_ref[...] = _sb_tile_t(q4t, k, vt, upper, zero, mask)

    @pl.when(last % 2 == 1)
    def _():
        acc_ref[...], carry_ref[...] = _sb_pair_t(q4t, _kv_tile(k_ref, vt_ref, last), _kv_tile(k_ref, vt_ref, last - 1),
                                                  upper, zero, mask)

    first = last - 1 - last % 2

    def body(s, carry):
        tiles = [_kv_tile(k_ref, vt_ref, first - 4 * s - j) for j in range(4)]
        zs = [_dot(k, q4t) for k, _ in tiles]
        sps = [_softplus(z) for z in zs]
        laters = [_dot(upper, sp.astype(BF16)) for sp in sps]
        pv = None
        for (_, vt), z, sp, later in zip(tiles, zs, sps, laters):
            w = jnp.exp(((z - sp) - later) - carry)
            carry = carry + jnp.sum(sp, axis=0, keepdims=True)
            part = _dot(vt, w.astype(BF16))
            pv = part if pv is None else pv + part
        acc_ref[...] += pv
        return carry

    pairs = last // 2
    carry_ref[...] = lax.fori_loop(0, pairs // 2, body, carry_ref[...])

    @pl.when(pairs % 2 == 1)
    def _():
        pv, _ = _sb_pair_t(q4t, _kv_tile(k_ref, vt_ref, 1), _kv_tile(k_ref, vt_ref, 0), upper, carry_ref[...], None)
        acc_ref[...] += pv
    o_ref[...] = (_heads_from_lanes(acc_ref[...], Q_BLOCK) * g_ref[...].astype(F32)).astype(BF16)


def sb_prompt(qt, k, vt, gact):
    b, t, _ = k.shape
    n_pages = t // PAGE
    blk = pl.BlockSpec((None, Q_BLOCK, D_ATT), lambda bi, i: (bi, i, 0))
    return pl.pallas_call(
        _sb_prompt_kernel,
        out_shape=jax.ShapeDtypeStruct((b, t, D_ATT), BF16),
        grid=(b, t // Q_BLOCK),
        in_specs=[pl.BlockSpec((None, None, D_ATT, PAGE), lambda bi, i: (bi, i, 0, 0)),
                  pl.BlockSpec((None, t, D_ATT), lambda bi, i: (bi, 0, 0)),
                  pl.BlockSpec((None, n_pages, D_ATT, PAGE), lambda bi, i: (bi, 0, 0, 0)),
                  blk],
        out_specs=blk,
        scratch_shapes=[pltpu.VMEM((D_ATT, N_HEADS * Q_BLOCK), F32), pltpu.VMEM((1, N_HEADS * Q_BLOCK), F32)],
        compiler_params=_params(("parallel", "arbitrary")),
        name="sb_prompt",
    )(qt, k, vt, gact)


def _moba_select_t(gate, valid, idx):
    n = gate.shape[0]
    gate = jnp.where(valid, gate, NEG_INF)
    rank = jnp.zeros(gate.shape, I32)
    for m in range(n):
        gm = gate[m:m + 1, :]
        beats = (gm > gate) | ((gm == gate) & (idx > m))
        rank = rank + beats.astype(I32)
    return valid & (rank < MOBA_TOPK)


def _moba_prompt_kernel(qft_ref, qht_ref, k_ref, vt_ref, km_ref, bias_ref, g_ref, o_ref, acc_ref, m_ref, l_ref, sel_ref):
    i = pl.program_id(1)
    nb = km_ref.shape[0]
    own = (i * Q_BLOCK) // KV_TILE
    odd = (i * Q_BLOCK) % KV_TILE != 0
    q4t = _heads_on_lanes(qht_ref[...])
    gate = jnp.dot(km_ref[...], _heads_on_lanes(qft_ref[...]), precision=HI, preferred_element_type=F32)
    idx = lax.broadcasted_iota(I32, (nb, 1), 0)
    sel = _moba_select_t(gate, idx < own, idx)
    sel_ref[...] = jnp.where(sel, 0.0, NEG_INF)

    def attend(blocks, first=False):
        tiles = [_kv_tile(k_ref, vt_ref, n) for n, _ in blocks]
        scores = []
        for (k, _), (_, terms) in zip(tiles, blocks):
            s = _dot(k, q4t)
            for t in terms:
                s = s + t
            scores.append(s)
        m_new = functools.reduce(jnp.maximum, [jnp.max(s, axis=0, keepdims=True) for s in scores])
        if not first:
            m_old = m_ref[...]
            m_new = jnp.maximum(m_old, m_new)
            alpha = jnp.exp(m_old - m_new)
        ps = [jnp.exp(s - m_new) for s in scores]
        l_new = functools.reduce(jnp.add, [jnp.sum(p, axis=0, keepdims=True) for p in ps])
        pv = functools.reduce(jnp.add, [_dot(vt, p.astype(BF16)) for (_, vt), p in zip(tiles, ps)])
        m_ref[...] = m_new
        l_ref[...] = l_new if first else alpha * l_ref[...] + l_new
        acc_ref[...] = pv if first else alpha * acc_ref[...] + pv

    attend([(own, [bias_ref[jnp.where(odd, 1, 0)]])], first=True)
    near_pat = jnp.where(odd, P_FAR, 2)
    far_row = bias_ref[P_FAR, 0:1, :]
    for n in range(0, nb - 2, 2):
        @pl.when(n + 1 < own)
        def _():
            attend([(n, [far_row + sel_ref[n:n + 1, :]]),
                    (n + 1, [bias_ref[jnp.where(n + 1 == own - 1, near_pat, P_FAR)], sel_ref[n + 1:n + 2, :]])])

    @pl.when(own % 2 == 1)
    def _():
        attend([(own - 1, [bias_ref[near_pat], sel_ref[pl.ds(own - 1, 1), :]])])

    y = _heads_from_lanes(acc_ref[...] / l_ref[...], Q_BLOCK)
    o_ref[...] = (y * g_ref[...].astype(F32)).astype(BF16)


def moba_prompt(qft, qht, k, vt, kmean, bias, gact):
    b, t, _ = k.shape
    nb = t // KV_TILE
    cols = N_HEADS * Q_BLOCK
    blk = pl.BlockSpec((None, Q_BLOCK, D_ATT), lambda bi, i: (bi, i, 0))
    qpage = pl.BlockSpec((None, None, D_ATT, PAGE), lambda bi, i: (bi, i, 0, 0))
    return pl.pallas_call(
        _moba_prompt_kernel,
        out_shape=jax.ShapeDtypeStruct((b, t, D_ATT), BF16),
        grid=(b, t // Q_BLOCK),
        in_specs=[qpage, qpage,
                  pl.BlockSpec((None, t, D_ATT), lambda bi, i: (bi, 0, 0)),
                  pl.BlockSpec((None, t // PAGE, D_ATT, PAGE), lambda bi, i: (bi, 0, 0, 0)),
                  pl.BlockSpec((None, nb, D_ATT), lambda bi, i: (bi, 0, 0)),
                  pl.BlockSpec(bias.shape, lambda bi, i: (0, 0, 0)),
                  pl.BlockSpec((None, Q_BLOCK, D_ATT), lambda bi, i: (bi, i, 1))],
        out_specs=blk,
        scratch_shapes=[pltpu.VMEM((D_ATT, cols), F32), pltpu.VMEM((1, cols), F32), pltpu.VMEM((1, cols), F32),
                        pltpu.VMEM((nb, cols), F32)],
        compiler_params=_params(("parallel", "arbitrary")),
        name="moba_prompt",
    )(qft, qht, k, vt, kmean, bias, gact)


def _gated_head_norm(h, w, g):
    hn = h * lax.rsqrt(jnp.mean(h * h, axis=-1, keepdims=True) + EPS) * w
    return (hn * g.astype(F32)).astype(BF16)


def _mlstm_prompt_kernel(qt_ref, k_ref, vt_ref, gc_ref, gr_ref, g_ref, nw_ref, h_ref, c_ref, n_ref, m_ref,
                         cst_ref, nst_ref, ms_ref):
    ci = pl.program_id(1)
    L = ML_CHUNK

    @pl.when(ci == 0)
    def _():
        cst_ref[...] = jnp.zeros_like(cst_ref)
        nst_ref[...] = jnp.zeros_like(nst_ref)
        ms_ref[...] = jnp.zeros_like(ms_ref)

    r = lax.broadcasted_iota(I32, (L, L), 0)
    c = lax.broadcasted_iota(I32, (L, L), 1)
    lower = jnp.where(c <= r, 1.0, 0.0).astype(F32)
    upper = jnp.where(r <= c, 1.0, 0.0).astype(F32)
    src_before_tgt = r <= c
    gc = gc_ref[...]
    gr = gr_ref[...]
    bh_c = jnp.dot(lower, gc, precision=HI, preferred_element_type=F32)
    bh_r = jnp.dot(gr, upper, precision=HI, preferred_element_type=F32)

    for h in range(ML_HEADS):
        sl = slice(h * ML_D, (h + 1) * ML_D)
        qt, k, vt = qt_ref[sl, :], k_ref[:, sl], vt_ref[sl, :]
        m_prev = ms_ref[h][0:1, 0:1]
        b_r = bh_r[ML_HEADS + h:ML_HEADS + h + 1, :]
        a_r = gr[h:h + 1, :] - b_r
        a_c = gc[:, h:h + 1] - bh_c[:, ML_HEADS + h:ML_HEADS + h + 1]
        dm = jnp.where(src_before_tgt, a_c + b_r, NEG_INF)
        g = b_r + m_prev
        m_row = jnp.maximum(g, jnp.max(dm, axis=0, keepdims=True))
        w_inter = jnp.exp(g - m_row)
        sc = _dot(k, qt) * jnp.exp(dm - m_row)
        cst, nst = cst_ref[h], nst_ref[h]
        num = _dot(vt, sc.astype(BF16)) + w_inter * _dot(cst.astype(BF16), qt)
        den = jnp.sum(sc, axis=0, keepdims=True) + w_inter * _dot(nst.astype(BF16), qt)[0:1]
        hout = (num / jnp.maximum(jnp.abs(den), jnp.exp(-m_row))).T
        h_ref[:, sl] = _gated_head_norm(hout, nw_ref[:, sl], g_ref[:, sl])

        b_last = b_r[:, L - 1:L]
        m_new = jnp.maximum(b_last + m_prev, jnp.max(b_last + a_r, axis=-1, keepdims=True))
        ws = jnp.exp(b_last + a_r - m_new)
        wc = jnp.exp(b_last + m_prev - m_new)
        cst_ref[h] = wc * cst + _dot((vt.astype(F32) * ws).astype(BF16), k)
        nst_ref[h] = wc * nst + _dot(jnp.broadcast_to(ws, (8, L)).astype(BF16), k)
        ms_ref[h] = jnp.broadcast_to(m_new, ms_ref.shape[1:])

    @pl.when(ci == pl.num_programs(1) - 1)
    def _():
        for h in range(ML_HEADS):
            c_ref[h] = cst_ref[h]
            n_ref[h:h + 1, :] = nst_ref[h][0:1]
            m_ref[:, h:h + 1] = ms_ref[h][0:1, 0:1]


def mlstm_prompt(qt, k, vt, gates, gates_t, gact, norm_w):
    b, t, _ = k.shape
    blk = pl.BlockSpec((None, ML_CHUNK, D_ML), lambda bi, c: (bi, c, 0))
    page = pl.BlockSpec((None, None, D_ML, ML_CHUNK), lambda bi, c: (bi, c, 0, 0))
    return pl.pallas_call(
        _mlstm_prompt_kernel,
        out_shape=(jax.ShapeDtypeStruct((b, t, D_ML), BF16),
                   jax.ShapeDtypeStruct((b, ML_HEADS, ML_D, ML_D), F32),
                   jax.ShapeDtypeStruct((b, ML_HEADS, ML_D), F32),
                   jax.ShapeDtypeStruct((b, 1, ML_HEADS), F32)),
        grid=(b, t // ML_CHUNK),
        in_specs=[page, blk, page,
                  pl.BlockSpec((None, ML_CHUNK, 2 * ML_HEADS), lambda bi, c: (bi, c, 0)),
                  pl.BlockSpec((None, 2 * ML_HEADS, ML_CHUNK), lambda bi, c: (bi, 0, c)),
                  pl.BlockSpec((None, ML_CHUNK, D_ML), lambda bi, c: (bi, c, 1)),
                  pl.BlockSpec((1, D_ML), lambda bi, c: (0, 0))],
        out_specs=(blk,
                   pl.BlockSpec((None, ML_HEADS, ML_D, ML_D), lambda bi, c: (bi, 0, 0, 0)),
                   pl.BlockSpec((None, ML_HEADS, ML_D), lambda bi, c: (bi, 0, 0)),
                   pl.BlockSpec((None, 1, ML_HEADS), lambda bi, c: (bi, 0, 0))),
        scratch_shapes=[pltpu.VMEM((ML_HEADS, ML_D, ML_D), F32), pltpu.VMEM((ML_HEADS, 8, ML_D), F32),
                        pltpu.VMEM((ML_HEADS, 8, 128), F32)],
        compiler_params=_params(("parallel", "arbitrary")),
        name="mlstm_prompt",
    )(qt, k, vt, gates, gates_t, gact, norm_w)


def _page_stream(pt_ref, hbm_refs, bufs, sems, *, layer, group, depth, reverse, consume, init):
    n_seq, n_pages = pt_ref.shape
    per_seq = n_pages // group
    total = n_seq * per_seq

    def copies(g, slot):
        seq, c = g // per_seq, g % per_seq
        out = []
        for r in range(group):
            walk = c * group + r
            page = pt_ref[seq, n_pages - 1 - walk if reverse else walk]
            for hbm, buf, sem in zip(hbm_refs, bufs, sems):
                out.append(pltpu.make_async_copy(hbm.at[layer, page], buf.at[slot, r], sem.at[slot]))
        return out

    for g in range(depth - 1):
        for cp in copies(g, g):
            cp.start()

    def body(g, carry):
        slot = lax.rem(g, depth)
        ahead = g + depth - 1

        @pl.when(ahead < total)
        def _():
            for cp in copies(ahead, lax.rem(ahead, depth)):
                cp.start()

        for cp in copies(g, slot):
            cp.wait()
        return consume(g // per_seq, g % per_seq, slot, carry)

    return lax.fori_loop(0, total, body, init)


def _sb_decode_kernel(pt_ref, q_ref, g_ref, k_hbm, v_hbm, o_ref, kbuf, vbuf, ksem, vsem, *, layer, group, depth):
    tiles = group // 2
    last_chunk = pt_ref.shape[1] // group - 1
    rowi = lax.broadcasted_iota(I32, (8, D_ATT), 0)
    own_head = _head_of_lane((8, D_ATT), 1) == rowi
    suffix = _suffix_matrix(KV_TILE)

    def consume(seq, c, slot, state):
        fresh = c == 0
        acc = jnp.where(fresh, 0.0, state[0])
        carry = jnp.where(fresh, 0.0, state[1])
        qm = jnp.where(own_head, jnp.broadcast_to(q_ref[seq].astype(F32), (8, D_ATT)), 0.0).astype(BF16)

        def tile(buf, t):
            lo, hi = buf[slot, 2 * t + 1].reshape(D_ATT, PAGE), buf[slot, 2 * t].reshape(D_ATT, PAGE)
            return jnp.concatenate([lo, hi], axis=1).astype(BF16)

        z = jnp.concatenate([_dot(qm, tile(kbuf, t)) for t in range(tiles)], axis=0)
        sp = _softplus(z)
        lf = -sp
        hi = lf.astype(BF16).astype(F32)
        both = _dot(jnp.concatenate([hi, lf - hi], axis=0).astype(BF16), suffix)
        later = both[:8 * tiles] + both[8 * tiles:]
        tot = jnp.sum(lf, axis=-1, keepdims=True)
        carries = [carry]
        for t in range(tiles):
            carries.append(carries[-1] + tot[8 * t:8 * (t + 1)])
        w = jnp.exp((z - sp) + later + jnp.concatenate(carries[:tiles], axis=0))
        for t in range(tiles):
            acc = acc + _dot_nt(w[8 * t:8 * (t + 1)].astype(BF16), tile(vbuf, t))

        @pl.when(c == last_chunk)
        def _():
            y = jnp.sum(jnp.where(own_head, acc, 0.0), axis=0, keepdims=True)
            o_ref[seq] = (y * g_ref[seq].astype(F32)).astype(BF16)

        return acc, carries[tiles]

    _page_stream(pt_ref, (k_hbm, v_hbm), (kbuf, vbuf), (ksem, vsem), layer=layer, group=group, depth=depth,
                 reverse=True, consume=consume, init=(jnp.zeros((8, D_ATT), F32), jnp.zeros((8, 1), F32)))


def _page_buffers(depth, group):
    return pltpu.VMEM((depth, group, N_HEADS, HEAD_DIM, PAGE), F32)


def sb_decode(page_table, q, gact, cache_k, cache_v, layer, *, group, depth):
    b = page_table.shape[0]
    rows = pl.BlockSpec((b, 1, D_ATT), lambda i, pt: (0, 0, 0))
    hbm = pl.BlockSpec(memory_space=pl.ANY)
    grid_spec = pltpu.PrefetchScalarGridSpec(
        num_scalar_prefetch=1,
        grid=(1,),
        in_specs=[rows, rows, hbm, hbm],
        out_specs=rows,
        scratch_shapes=[_page_buffers(depth, group), _page_buffers(depth, group),
                        pltpu.SemaphoreType.DMA((depth,)), pltpu.SemaphoreType.DMA((depth,))],
    )
    return pl.pallas_call(
        functools.partial(_sb_decode_kernel, layer=layer, group=group, depth=depth),
        out_shape=jax.ShapeDtypeStruct((b, 1, D_ATT), BF16),
        grid_spec=grid_spec,
        compiler_params=_params(("arbitrary",)),
        name="sb_decode",
    )(page_table, q, gact, cache_k, cache_v)


def _moba_gate_kernel(pt_ref, q_ref, k_hbm, idx_ref, kbuf, ksem, gs_ref, *, layer, group, depth):
    per_chunk = group // 2
    last_chunk = pt_ref.shape[1] // group - 1

    def consume(seq, c, slot, carry):
        qc = q_ref[seq]
        for r in range(per_chunk):
            pair = kbuf[slot, 2 * r].reshape(D_ATT, PAGE) + kbuf[slot, 2 * r + 1].reshape(D_ATT, PAGE)
            prod = pair * qc
            for h in range(N_HEADS):
                gs_ref[h, pl.ds(c * per_chunk + r, 1), :] = jnp.sum(prod[h * HEAD_DIM:(h + 1) * HEAD_DIM], axis=0,
                                                                    keepdims=True)

        @pl.when(c == last_chunk)
        def _():
            nb = gs_ref.shape[1]
            lane = lax.broadcasted_iota(I32, (nb, 128), 1)
            gate = jnp.zeros((nb, 128), F32)
            for h in range(N_HEADS):
                gate = gate + jnp.where(lane == h, jnp.sum(gs_ref[h], axis=-1, keepdims=True) * (1.0 / KV_TILE), 0.0)
            blk = lax.broadcasted_iota(I32, (nb, 128), 0)
            rank = jnp.zeros((nb, 128), I32)
            for m in range(nb):
                gm = gate[m:m + 1, :]
                rank = rank + ((gm > gate) | ((gm == gate) & (blk > m))).astype(I32)
            blk_f = blk.astype(F32)
            rows = [jnp.sum(jnp.where(rank == t, blk_f, 0.0), axis=0, keepdims=True) for t in range(MOBA_TOPK)]
            idx_ref[seq] = jnp.concatenate(rows + [jnp.zeros((8 - MOBA_TOPK, 128), F32)], axis=0).astype(I32)

        return carry

    _page_stream(pt_ref, (k_hbm,), (kbuf,), (ksem,), layer=layer, group=group, depth=depth, reverse=False,
                 consume=consume, init=0)


def moba_decode_select(page_table, q_col, cache_k, layer, *, group, depth):
    b, n_pages = page_table.shape
    nb = n_pages * PAGE // KV_TILE
    grid_spec = pltpu.PrefetchScalarGridSpec(
        num_scalar_prefetch=1,
        grid=(1,),
        in_specs=[pl.BlockSpec((b, D_ATT, 1), lambda i, pt: (0, 0, 0)), pl.BlockSpec(memory_space=pl.ANY)],
        out_specs=pl.BlockSpec((b, 8, 128), lambda i, pt: (0, 0, 0)),
        scratch_shapes=[_page_buffers(depth, group), pltpu.SemaphoreType.DMA((depth,)),
                        pltpu.VMEM((N_HEADS, nb, PAGE), F32)],
    )
    return pl.pallas_call(
        functools.partial(_moba_gate_kernel, layer=layer, group=group, depth=depth),
        out_shape=jax.ShapeDtypeStruct((b, 8, 128), I32),
        grid_spec=grid_spec,
        compiler_params=_params(("arbitrary",)),
        name="moba_decode_select",
    )(page_table, q_col, cache_k)


def _moba_decode_kernel(pt_ref, sel_ref, q_ref, kn_ref, vn_ref, bias_ref, g_ref, k_hbm, v_hbm, o_ref,
                        kbuf, vbuf, ksem, vsem, *, layer, last_block):
    n_seq = pt_ref.shape[0]

    def piece(t, h, half):
        return (t * N_HEADS + h) * 2 + half

    def copies(seq, slot):
        out = []
        for t in range(MOBA_TOPK):
            for h in range(N_HEADS):
                blk = sel_ref[seq, t * N_HEADS + h]
                for half in range(2):
                    page = pt_ref[seq, 2 * blk + half]
                    i = piece(t, h, half)
                    out.append(pltpu.make_async_copy(k_hbm.at[layer, page, h], kbuf.at[slot, i], ksem.at[slot]))
                    out.append(pltpu.make_async_copy(v_hbm.at[layer, page, h], vbuf.at[slot, i], vsem.at[slot]))
        return out

    for cp in copies(0, 0):
        cp.start()

    def body(seq, carry):
        slot = lax.rem(seq, 2)

        @pl.when(seq + 1 < n_seq)
        def _():
            for cp in copies(seq + 1, 1 - slot):
                cp.start()

        for cp in copies(seq, slot):
            cp.wait()
        for h in range(N_HEADS):
            qf = jnp.broadcast_to(q_ref[seq, h].astype(F32), (8, HEAD_DIM))
            q = qf.astype(BF16)
            s_self = (jnp.sum(qf * kn_ref[seq, h].astype(BF16).astype(F32), axis=-1, keepdims=True)
                      + bias_ref[h, 2][:, 0:1])
            scores = []
            for t in range(MOBA_TOPK):
                kt = jnp.concatenate([kbuf[slot, piece(t, h, 0)], kbuf[slot, piece(t, h, 1)]], axis=1).astype(BF16)
                near = sel_ref[seq, t * N_HEADS + h] == last_block
                scores.append(_dot(q, kt) + bias_ref[h, jnp.where(near, 1, 0)])
            m = s_self
            for s in scores:
                m = jnp.maximum(m, jnp.max(s, axis=-1, keepdims=True))
            p_self = jnp.exp(s_self - m)
            l = p_self
            acc = p_self * vn_ref[seq, h].astype(BF16).astype(F32)
            for t, s in enumerate(scores):
                p = jnp.exp(s - m)
                l = l + jnp.sum(p, axis=-1, keepdims=True)
                vt = jnp.concatenate([vbuf[slot, piece(t, h, 0)], vbuf[slot, piece(t, h, 1)]], axis=1).astype(BF16)
                acc = acc + _dot_nt(p.astype(BF16), vt)
            o_ref[seq, h] = ((acc / l)[0:1] * g_ref[seq, h].astype(F32)).astype(BF16)
        return carry

    lax.fori_loop(0, n_seq, body, 0)


def moba_decode(page_table, sel, qh, k_new, v_new, dec_bias, gact_mb, cache_k, cache_v, layer):
    b, n_pages = page_table.shape
    last_block = n_pages * PAGE // KV_TILE - 1

    pieces = 2 * MOBA_TOPK * N_HEADS
    vec = pl.BlockSpec((b, N_HEADS, 1, HEAD_DIM), lambda i, pt, sl: (0, 0, 0, 0))
    hbm = pl.BlockSpec(memory_space=pl.ANY)
    grid_spec = pltpu.PrefetchScalarGridSpec(
        num_scalar_prefetch=2,
        grid=(1,),
        in_specs=[vec, vec, vec, pl.BlockSpec(dec_bias.shape, lambda i, pt, sl: (0, 0, 0, 0)), vec, hbm, hbm],
        out_specs=vec,
        scratch_shapes=[pltpu.VMEM((2, pieces, HEAD_DIM, PAGE), F32), pltpu.VMEM((2, pieces, HEAD_DIM, PAGE), F32),
                        pltpu.SemaphoreType.DMA((2,)), pltpu.SemaphoreType.DMA((2,))],
    )
    return pl.pallas_call(
        functools.partial(_moba_decode_kernel, layer=layer, last_block=last_block),
        out_shape=jax.ShapeDtypeStruct((b, N_HEADS, 1, HEAD_DIM), BF16),
        grid_spec=grid_spec,
        compiler_params=_params(("arbitrary",)),
        name="moba_decode",
    )(page_table, sel, qh, k_new, v_new, dec_bias, gact_mb, cache_k, cache_v)


def _mlstm_decode_kernel(q_ref, k_ref, v_ref, gt_ref, g_ref, nw_ref, c_ref, n_ref, m_ref,
                         h_ref, co_ref, no_ref, mo_ref):
    r = lax.broadcasted_iota(I32, (ML_D, ML_D), 0)
    c = lax.broadcasted_iota(I32, (ML_D, ML_D), 1)
    eye = r == c
    for h in range(ML_HEADS):
        sl = slice(h * ML_D, (h + 1) * ML_D)
        q = q_ref[:, sl].astype(F32)
        k = k_ref[:, sl].astype(F32)
        v = v_ref[:, sl].astype(F32)
        i_pre = gt_ref[:, h:h + 1]
        logf = gt_ref[:, ML_HEADS + h:ML_HEADS + h + 1]
        m_prev = m_ref[:, h:h + 1]
        cm = c_ref[h]
        nv = n_ref[h:h + 1, :]
        g = logf + m_prev
        m_new = jnp.maximum(g, i_pre)
        w_in = jnp.exp(i_pre - m_new)
        w_st = jnp.exp(g - m_new)
        qk = jnp.sum(q * k, axis=-1, keepdims=True)
        cq = _dot_nt(jnp.broadcast_to(q, (8, ML_D)), cm, precision=HI)[0:1]
        nq = jnp.sum(nv * q, axis=-1, keepdims=True)
        num = (qk * w_in) * v + w_st * cq
        den = qk * w_in + w_st * nq
        hout = num / jnp.maximum(jnp.abs(den), jnp.exp(-m_new))
        h_ref[:, sl] = _gated_head_norm(hout, nw_ref[:, sl], g_ref[:, sl])
        v_diag = jnp.where(eye, jnp.broadcast_to(v, (ML_D, ML_D)), 0.0)
        outer = jnp.dot(v_diag, jnp.broadcast_to(k, (ML_D, ML_D)), precision=HI, preferred_element_type=F32)
        co_ref[h] = w_st * cm + w_in * outer
        no_ref[h:h + 1, :] = w_st * nv + w_in * k
        mo_ref[:, h:h + 1] = m_new


def mlstm_decode(q, k, v, gates, gact, norm_w, c0, n0, m0, layer):
    b = q.shape[0]
    vec = pl.BlockSpec((None, 1, D_ML), lambda bi: (bi, 0, 0))
    cs = pl.BlockSpec((None, ML_HEADS, ML_D, ML_D), lambda bi: (bi, 0, 0, 0))
    ns = pl.BlockSpec((None, ML_HEADS, ML_D), lambda bi: (bi, 0, 0))
    msp = pl.BlockSpec((None, 1, ML_HEADS), lambda bi: (bi, 0, 0))
    return pl.pallas_call(
        _mlstm_decode_kernel,
        out_shape=(jax.ShapeDtypeStruct((b, 1, D_ML), BF16),
                   jax.ShapeDtypeStruct(c0.shape[1:], F32),
                   jax.ShapeDtypeStruct(n0.shape[1:], F32),
                   jax.ShapeDtypeStruct(m0.shape, F32)),
        grid=(b,),
        in_specs=[vec, vec, vec,
                  pl.BlockSpec((None, 1, 2 * ML_HEADS), lambda bi: (bi, 0, 0)),
                  pl.BlockSpec((None, 1, D_ML), lambda bi: (bi, 0, 1)),
                  pl.BlockSpec((1, D_ML), lambda bi: (0, 0)),
                  pl.BlockSpec((None, None, ML_HEADS, ML_D, ML_D), lambda bi: (layer, bi, 0, 0, 0)),
                  pl.BlockSpec((None, None, ML_HEADS, ML_D), lambda bi: (layer, bi, 0, 0)),
                  msp],
        out_specs=(vec, cs, ns, msp),
        compiler_params=_params(("parallel",)),
        name="mlstm_decode",
    )(q, k, v, gates, gact, norm_w, c0, n0, m0)


SB_DECODE_GROUP = 16
MOBA_SELECT_GROUP = 32
STREAM_DEPTH = 4
PROMPT_TM = 512


def _bf16_weights(w_in, w_out):
    w_t = jnp.swapaxes(w_in, 1, 2)
    w_bf = w_t.astype(BF16)
    w_lo = (w_t[:, O_QMB:O_VMB] - w_bf[:, O_QMB:O_VMB].astype(F32)).astype(BF16)
    return w_bf, w_lo, w_out.astype(BF16)


def kernel(x_prompt, x_sample, cache_sb_k, cache_sb_v, cache_moba_k, cache_moba_v, state_mlstm_C, state_mlstm_n,
           state_mlstm_m, page_table, c_prompt, c_sample, norm_w, w_ada, b_ada, w_in, b_gates, q_norm_w, k_norm_w,
           rel_bias, ml_norm_w, w_out):
    bp, tp, d = x_prompt.shape
    bs = x_sample.shape[0]
    depth = w_in.shape[0]
    mp = bp * tp
    n_pages = page_table.shape[1]

    mod = ada_modulation(jnp.concatenate([c_prompt, c_sample], axis=0), w_ada, b_ada)
    bias_p, bias_d = bias_tiles(rel_bias)
    lanes_last = lambda a: jnp.transpose(a, (0, 1, 3, 4, 2))
    sbk, sbv, mbk, mbv = (lanes_last(a) for a in (cache_sb_k, cache_sb_v, cache_moba_k, cache_moba_v))

    xp = x_prompt.reshape(mp, d)
    xs = x_sample.reshape(bs, d)
    p_out = [[] for _ in range(3)]
    s_out = [[] for _ in range(7)]
    kv_pages = ()
    w_bf, w_lo, wo_bf = _bf16_weights(w_in, w_out)
    for l in range(depth):
        nw = norm_w[l].reshape(1, d)
        bg = b_gates[l].reshape(1, 2 * ML_HEADS)
        qnw = jnp.tile(q_norm_w[l], N_HEADS).reshape(1, D_ATT)
        knw = jnp.tile(k_norm_w[l], N_HEADS).reshape(1, D_ATT)
        mlw = ml_norm_w[l].reshape(1, D_ML)
        shift, scale, gate = jnp.split(mod[l], 3, axis=-1)

        pm = lambda a: a[:bp].reshape(bp, 1, d)
        (qsb_t, ksb_t, ksb, vsb_t, vsb_th, qmb_t, qmb_th, kmb_t, kmb, vmb_t, vmb_th,
         qml, kml, vml, gact, gates, kmean) = \
            input_projection(xp, nw, pm(scale), pm(shift), w_bf, w_lo, bg, qnw, knw,
                             tm=PROMPT_TM, rows_per_mod=tp // PROMPT_TM, prompt=True,
                             layer=l, depth=depth, kv_carried=kv_pages)
        kv_pages = (ksb_t, vsb_t, kmb_t, vmb_t)
        seq = lambda a: a.reshape(bp, tp, a.shape[-1])
        pages = lambda a: a.reshape(bp, tp // PAGE, D_ATT, PAGE)
        gact3 = seq(gact)
        ysb = sb_prompt(pages(qsb_t), seq(ksb), pages(vsb_th), gact3)
        ymb = moba_prompt(pages(qmb_t), pages(qmb_th), seq(kmb), pages(vmb_th),
                          kmean.reshape(bp, tp // KV_TILE, D_ATT), bias_p, gact3)
        gates3 = seq(gates)
        ml_pages = lambda a: a.reshape(bp, tp // ML_CHUNK, D_ML, ML_CHUNK)
        yml, c_p, n_p, m_p = mlstm_prompt(ml_pages(qml), seq(kml), ml_pages(vml), gates3, gates3.swapaxes(1, 2),
                                          gact3, mlw)
        xp = output_projection(ysb.reshape(mp, D_ATT), ymb.reshape(mp, D_ATT), yml.reshape(mp, D_ML), xp, pm(gate),
                               wo_bf, tm=PROMPT_TM, rows_per_mod=tp // PROMPT_TM, layer=l)
        for lst, a in zip(p_out, (c_p, n_p, m_p.reshape(bp, ML_HEADS))):
            lst.append(a)

        sm = lambda a: a[bp:].reshape(1, bs, d)
        (qsb, ksb, vsb, qmb, qmbh, kmb, vmb, qml, kml, vml, gact, gates) = \
            input_projection(xs, nw, sm(scale), sm(shift), w_bf, w_lo, bg, qnw, knw,
                             tm=bs, rows_per_mod=1, prompt=False, layer=l)
        tok = lambda a: a.reshape(bs, 1, a.shape[-1])
        heads = lambda a: a.reshape(bs, N_HEADS, 1, HEAD_DIM)
        gact3 = tok(gact)
        ysb = sb_decode(page_table, tok(qsb), gact3[:, :, :D_ATT], sbk, sbv, l, group=min(SB_DECODE_GROUP, n_pages),
                        depth=STREAM_DEPTH)
        sel = moba_decode_select(page_table, qmb.reshape(bs, D_ATT, 1), mbk, l,
                                 group=min(MOBA_SELECT_GROUP, n_pages), depth=STREAM_DEPTH)
        sel = sel[:, :MOBA_TOPK, :N_HEADS].reshape(bs, MOBA_TOPK * N_HEADS)
        ymb = moba_decode(page_table, sel, heads(qmbh), heads(kmb), heads(vmb), bias_d,
                          heads(gact[:, D_ATT:2 * D_ATT]), mbk, mbv, l)
        yml, c_s, n_s, m_s = mlstm_decode(tok(qml), tok(kml), tok(vml), tok(gates), gact3, mlw,
                                          state_mlstm_C, state_mlstm_n, state_mlstm_m[l].reshape(bs, 1, ML_HEADS), l)
        xs = output_projection(ysb.reshape(bs, D_ATT), ymb.reshape(bs, D_ATT), yml.reshape(bs, D_ML), xs, sm(gate),
                               wo_bf, tm=bs, rows_per_mod=1, layer=l)
        new = lambda a: a.reshape(bs, 1, N_HEADS, HEAD_DIM)
        for lst, a in zip(s_out, (new(ksb), new(vsb), new(kmb), new(vmb), c_s, n_s, m_s.reshape(bs, ML_HEADS))):
            lst.append(a)

    paged = lambda a: jnp.transpose(a.reshape(depth, bp, tp // PAGE, N_HEADS, HEAD_DIM, PAGE), (0, 1, 2, 5, 3, 4))
    P = [paged(a) for a in kv_pages] + [jnp.stack(a) for a in p_out]
    S = [jnp.stack(a) for a in s_out]
    return (xp.reshape(bp, tp, d), xs.reshape(bs, 1, d),
            P[0], P[1], P[2], P[3], P[4], P[5], P[6], S[0], S[1], S[2], S[3], S[4], S[5], S[6])
```
